```python
import math
import jax, jax.numpy as jnp
from jax import lax
import numpy as np

D_MODEL = 1024
BATCH = 8
SEQ = 8192
DEPTH = 2

RET_HEADS = 4
RET_QK_DIM = D_MODEL // RET_HEADS
RET_V_DIM = 2 * RET_QK_DIM
RET_QK_TOTAL = RET_HEADS * RET_QK_DIM
RET_V_TOTAL = RET_HEADS * RET_V_DIM
RET_IN_COLS = 2 * RET_QK_TOTAL + 2 * RET_V_TOTAL
RET_CHUNK = 128
ROPE_BASE = 10000.0
CONV_WIDTH = 3
D_FF = 7 * D_MODEL // 2
N_EXPERTS = 8
TOP_K = 2
NORM_EPS = 1e-6

kernel_name = "hybrid_retention_shortconv_moe"


def _rmsnorm(x, g):
    xf = x.astype(jnp.float32)
    y = xf * lax.rsqrt(jnp.mean(xf * xf, axis=-1, keepdims=True) + NORM_EPS)
    return (y * g.astype(jnp.float32)).astype(x.dtype)


def _rotary(t, pos):
    d = t.shape[-1]
    inv_freq = ROPE_BASE ** (-jnp.arange(0, d, 2, dtype=jnp.float32) / d)
    ang = pos[:, None] * inv_freq[None, :]
    cos = jnp.cos(ang)[None, :, None, :]
    sin = jnp.sin(ang)[None, :, None, :]
    t1, t2 = t[..., : d // 2], t[..., d // 2:]
    return jnp.concatenate([t1 * cos - t2 * sin, t1 * sin + t2 * cos], axis=-1)


def _retention(xn, w_in, w_out):
    b, s, _ = xn.shape
    proj = (xn @ w_in).astype(jnp.float32)
    q = proj[..., :RET_QK_TOTAL].reshape(b, s, RET_HEADS, RET_QK_DIM)
    k = proj[..., RET_QK_TOTAL:2 * RET_QK_TOTAL].reshape(b, s, RET_HEADS, RET_QK_DIM)
    v = proj[..., 2 * RET_QK_TOTAL:2 * RET_QK_TOTAL + RET_V_TOTAL].reshape(b, s, RET_HEADS, RET_V_DIM)
    g = proj[..., 2 * RET_QK_TOTAL + RET_V_TOTAL:]
    pos = jnp.arange(s, dtype=jnp.float32)
    q = _rotary(q, pos)
    k = _rotary(k, pos) * (RET_QK_DIM ** -0.5)

    nc = s // RET_CHUNK

    def to_chunks(t):
        return t.reshape(b, nc, RET_CHUNK, RET_HEADS, t.shape[-1]).transpose(1, 0, 3, 2, 4)

    qc, kc, vc = to_chunks(q), to_chunks(k), to_chunks(v)

    log_gamma = jnp.log(1.0 - 2.0 ** (-5.0 - jnp.arange(RET_HEADS, dtype=jnp.float32)))
    idx = jnp.arange(RET_CHUNK, dtype=jnp.float32)
    diff = idx[:, None] - idx[None, :]
    causal = diff >= 0
    intra_decay = jnp.where(causal[None], jnp.exp(jnp.where(causal, diff, 0.0)[None] * log_gamma[:, None, None]), 0.0)
    q_decay = jnp.exp((idx[None, :] + 1.0) * log_gamma[:, None])
    k_decay = jnp.exp((RET_CHUNK - 1.0 - idx[None, :]) * log_gamma[:, None])
    chunk_decay = jnp.exp(RET_CHUNK * log_gamma)

    def step(state, inp):
        q_i, k_i, v_i = inp
        scores = jnp.einsum('bhid,bhjd->bhij', q_i, k_i) * intra_decay[None]
        inner = jnp.einsum('bhij,bhje->bhie', scores, v_i)
        cross = jnp.einsum('bhid,bhde->bhie', q_i * q_decay[None, :, :, None], state)
        new_state = state * chunk_decay[None, :, None, None] + jnp.einsum(
            'bhjd,bhje->bhde', k_i * k_decay[None, :, :, None], v_i)
        return new_state, inner + cross

    state0 = jnp.zeros((b, RET_HEADS, RET_QK_DIM, RET_V_DIM), jnp.float32)
    _, o = lax.scan(step, state0, (qc, kc, vc))
    o = o.transpose(1, 0, 3, 2, 4).reshape(b, s, RET_HEADS, RET_V_DIM)
    o = o * lax.rsqrt(jnp.mean(o * o, axis=-1, keepdims=True) + NORM_EPS)
    o = jax.nn.silu(g) * o.reshape(b, s, RET_V_TOTAL)
    return o.astype(xn.dtype) @ w_out


def _short_conv(xn, w_in, w_conv, w_out):
    s = xn.shape[1]
    proj = xn @ w_in
    gate_b = proj[..., :D_MODEL]
    gate_c = proj[..., D_MODEL:2 * D_MODEL]
    h = proj[..., 2 * D_MODEL:]
    u = (gate_c * h).astype(jnp.float32)
    u_pad = jnp.pad(u, ((0, 0), (CONV_WIDTH - 1, 0), (0, 0)))
    wc = w_conv.astype(jnp.float32)
    y = sum(wc[j][None, None, :] * u_pad[:, j:j + s, :] for j in range(CONV_WIDTH))
    y = gate_b.astype(jnp.float32) * y
    return y.astype(xn.dtype) @ w_out


def _swiglu(xn, w_gate, w_up, w_down):
    return (jax.nn.silu(xn @ w_gate) * (xn @ w_up)) @ w_down


def _moe_swiglu(xn, w_router, w_gate_e, w_up_e, w_down_e):
    b, s, d = xn.shape
    t = xn.reshape(b * s, d)
    logits = (t @ w_router).astype(jnp.float32)
    top_vals, top_idx = lax.top_k(logits, TOP_K)
    top_w = jax.nn.softmax(top_vals, axis=-1)
    gates = jnp.sum(jax.nn.one_hot(top_idx, N_EXPERTS, dtype=jnp.float32) * top_w[..., None], axis=1)
    out = jnp.zeros((b * s, d), jnp.float32)
    for e in range(N_EXPERTS):
        h = jax.nn.silu(t @ w_gate_e[e]) * (t @ w_up_e[e])
        out = out + gates[:, e:e + 1] * (h @ w_down_e[e]).astype(jnp.float32)
    return out.astype(xn.dtype).reshape(b, s, d)


def setup_inputs(seed: int = 0) -> dict:
    key = jax.random.key(seed)
    ks = jax.random.split(key, 24)

    def w(k, shape, fan_in):
        return jax.random.normal(k, shape, jnp.float32) * (fan_in ** -0.5)

    def gain(k):
        return 1.0 + 0.02 * jax.random.normal(k, (D_MODEL,), jnp.float32)

    return {
        "x": jax.random.normal(ks[0], (BATCH, SEQ, D_MODEL), jnp.float32),
        "norm_mix0": gain(ks[1]),
        "ret_w_in": w(ks[2], (D_MODEL, RET_IN_COLS), D_MODEL),
        "ret_w_out": w(ks[3], (RET_V_TOTAL, D_MODEL), RET_V_TOTAL),
        "norm_ffn0": gain(ks[4]),
        "ffn_w_gate": w(ks[5], (D_MODEL, D_FF), D_MODEL),
        "ffn_w_up": w(ks[6], (D_MODEL, D_FF), D_MODEL),
        "ffn_w_down": w(ks[7], (D_FF, D_MODEL), D_FF),
        "norm_mix1": gain(ks[8]),
        "conv_w_in": w(ks[9], (D_MODEL, 3 * D_MODEL), D_MODEL),
        "conv_w": w(ks[10], (CONV_WIDTH, D_MODEL), CONV_WIDTH),
        "conv_w_out": w(ks[11], (D_MODEL, D_MODEL), D_MODEL),
        "norm_ffn1": gain(ks[12]),
        "moe_router": w(ks[13], (D_MODEL, N_EXPERTS), D_MODEL),
        "moe_w_gate": w(ks[14], (N_EXPERTS, D_MODEL, D_FF), D_MODEL),
        "moe_w_up": w(ks[15], (N_EXPERTS, D_MODEL, D_FF), D_MODEL),
        "moe_w_down": w(ks[16], (N_EXPERTS, D_FF, D_MODEL), D_FF),
        "norm_final": gain(ks[17]),
    }


def reference(x, norm_mix0, ret_w_in, ret_w_out, norm_ffn0, ffn_w_gate, ffn_w_up, ffn_w_down,
              norm_mix1, conv_w_in, conv_w, conv_w_out, norm_ffn1, moe_router, moe_w_gate,
              moe_w_up, moe_w_down, norm_final):
    mixer_params = [(norm_mix0, ret_w_in, ret_w_out), (norm_mix1, conv_w_in, conv_w, conv_w_out)]
    ffn_params = [(norm_ffn0, ffn_w_gate, ffn_w_up, ffn_w_down),
                  (norm_ffn1, moe_router, moe_w_gate, moe_w_up, moe_w_down)]
    h = x
    for i in range(DEPTH):
        mp = mixer_params[i]
        if i % 2 == 0:
            h = h + _retention(_rmsnorm(h, mp[0]), mp[1], mp[2])
        else:
            h = h + _short_conv(_rmsnorm(h, mp[0]), mp[1], mp[2], mp[3])
        fp = ffn_params[i]
        if i % 2 == 0:
            h = h + _swiglu(_rmsnorm(h, fp[0]), fp[1], fp[2], fp[3])
        else:
            h = h + _moe_swiglu(_rmsnorm(h, fp[0]), fp[1], fp[2], fp[3], fp[4])
    return _rmsnorm(h, norm_final)
```

```python
import functools

import jax
import jax.numpy as jnp
from jax import lax
from jax.experimental import pallas as pl
from jax.experimental.pallas import tpu as pltpu

D_MODEL = 1024
RET_HEADS = 4
RET_QK_DIM = D_MODEL // RET_HEADS
RET_V_DIM = 2 * RET_QK_DIM
RET_QK_TOTAL = RET_HEADS * RET_QK_DIM
RET_V_TOTAL = RET_HEADS * RET_V_DIM
RET_IN_COLS = 2 * RET_QK_TOTAL + 2 * RET_V_TOTAL
ROPE_BASE = 10000.0
CONV_WIDTH = 3
D_FF = 7 * D_MODEL // 2
N_EXPERTS = 8
NORM_EPS = 1e-6

RET_CHUNK = 256
HALF = RET_QK_DIM // 2

VMEM_LIMIT_BYTES = 56 * 1024 * 1024

TOKEN_TILE = 1024
RET_SEQ_TILE = 1024
CONV_SEQ_TILE = 512
FF_TILE = 896
IN_COL_TILE = 1024

BF16 = jnp.bfloat16
F32 = jnp.float32


def _params(*semantics):
    return pltpu.CompilerParams(dimension_semantics=semantics,
                                vmem_limit_bytes=VMEM_LIMIT_BYTES)


def _rmsnorm_f32(x, g):
    return x * lax.rsqrt(jnp.mean(x * x, axis=-1, keepdims=True) + NORM_EPS) * g


def _silu(x):
    return x * (1.0 / (1.0 + jnp.exp(-x)))


def _ret_in_kernel(x_ref, g_ref, w_ref, cos_ref, sin_ref, o_ref, xn_ref):
    n = pl.program_id(1)

    @pl.when(n == 0)
    def _():
        xn_ref[...] = _rmsnorm_f32(x_ref[...], g_ref[...]).astype(BF16)

    acc = jnp.dot(xn_ref[...], w_ref[...], preferred_element_type=F32)

    def rotary(scale):
        cos = cos_ref[...]
        sin = sin_ref[...]
        for h in range(IN_COL_TILE // RET_QK_DIM):
            lo = h * RET_QK_DIM
            t1 = acc[:, lo:lo + HALF]
            t2 = acc[:, lo + HALF:lo + RET_QK_DIM]
            o_ref[:, lo:lo + HALF] = ((t1 * cos - t2 * sin) * scale).astype(BF16)
            o_ref[:, lo + HALF:lo + RET_QK_DIM] = ((t1 * sin + t2 * cos) * scale).astype(BF16)

    @pl.when(n == 0)
    def _():
        rotary(1.0)

    @pl.when(n == 1)
    def _():
        rotary(RET_QK_DIM ** -0.5)

    @pl.when(n >= 2)
    def _():
        o_ref[...] = acc.astype(BF16)


def _ret_in(x2, g, w_bf, cos, sin, seq):
    t = x2.shape[0]
    seq_tiles = seq // TOKEN_TILE
    return pl.pallas_call(
        _ret_in_kernel,
        grid=(t // TOKEN_TILE, RET_IN_COLS // IN_COL_TILE),
        in_specs=[
            pl.BlockSpec((TOKEN_TILE, D_MODEL), lambda i, n: (i, 0)),
            pl.BlockSpec((1, D_MODEL), lambda i, n: (0, 0)),
            pl.BlockSpec((D_MODEL, IN_COL_TILE), lambda i, n: (0, n)),
            pl.BlockSpec((TOKEN_TILE, HALF), lambda i, n: (i % seq_tiles, 0)),
            pl.BlockSpec((TOKEN_TILE, HALF), lambda i, n: (i % seq_tiles, 0)),
        ],
        out_specs=pl.BlockSpec((TOKEN_TILE, IN_COL_TILE), lambda i, n: (i, n)),
        out_shape=jax.ShapeDtypeStruct((t, RET_IN_COLS), BF16),
        scratch_shapes=[pltpu.VMEM((TOKEN_TILE, D_MODEL), BF16)],
        compiler_params=_params("parallel", "arbitrary"),
        name="ret_in_proj",
    )(x2, g, w_bf, cos, sin)


def _ret_core_kernel(q_ref, k_ref, v_ref, gate_ref, x_ref, wout_ref,
                     intra_ref, qdec_ref, kdec_ref, cdec_ref,
                     o_ref, state_ref, acc_ref):
    s = pl.program_id(1)
    h = pl.program_id(2)

    @pl.when(s == 0)
    def _():
        state_ref[h] = jnp.zeros((RET_QK_DIM, RET_V_DIM), F32)

    intra = intra_ref[0]
    qdec = qdec_ref[0]
    kdec = kdec_ref[0]
    cdec = cdec_ref[0]

    state = state_ref[h]
    outs = []
    for c in range(RET_SEQ_TILE // RET_CHUNK):
        rows = slice(c * RET_CHUNK, (c + 1) * RET_CHUNK)
        q = q_ref[rows, :]
        k = k_ref[rows, :]
        v = v_ref[rows, :]
        scores = lax.dot_general(q, k, (((1,), (1,)), ((), ())),
                                 preferred_element_type=F32) * intra
        inner = jnp.dot(scores.astype(BF16), v, preferred_element_type=F32)
        qd = (q.astype(F32) * qdec).astype(BF16)
        cross = jnp.dot(qd, state.astype(BF16), preferred_element_type=F32)
        kd = (k.astype(F32) * kdec).astype(BF16)
        state = state * cdec + lax.dot_general(
            kd, v, (((0,), (0,)), ((), ())), preferred_element_type=F32)
        o = inner + cross
        o = o * lax.rsqrt(jnp.mean(o * o, axis=-1, keepdims=True) + NORM_EPS)
        gate = gate_ref[rows, :].astype(F32)
        outs.append((_silu(gate) * o).astype(BF16))
    state_ref[h] = state

    og = jnp.concatenate(outs, axis=0)
    part = jnp.dot(og, wout_ref[...], preferred_element_type=F32)

    @pl.when(h == 0)
    def _():
        acc_ref[...] = x_ref[0] + part

    @pl.when(h > 0)
    def _():
        acc_ref[...] += part

    @pl.when(h == RET_HEADS - 1)
    def _():
        o_ref[0] = acc_ref[...]


def _ret_core(proj, x, wout_bf, intra, qdec, kdec, cdec):
    b, seq, _ = x.shape
    ts = RET_SEQ_TILE
    st = seq // ts
    qk_blocks = RET_QK_TOTAL // RET_QK_DIM
    v_off = 2 * RET_QK_TOTAL // RET_V_DIM
    g_off = v_off + RET_HEADS
    return pl.pallas_call(
        _ret_core_kernel,
        grid=(b, st, RET_HEADS),
        in_specs=[
            pl.BlockSpec((ts, RET_QK_DIM), lambda bi, s, h: (bi * st + s, h)),
            pl.BlockSpec((ts, RET_QK_DIM), lambda bi, s, h: (bi * st + s, qk_blocks + h)),
            pl.BlockSpec((ts, RET_V_DIM), lambda bi, s, h: (bi * st + s, v_off + h)),
            pl.BlockSpec((ts, RET_V_DIM), lambda bi, s, h: (bi * st + s, g_off + h)),
            pl.BlockSpec((1, ts, D_MODEL), lambda bi, s, h: (bi, s, 0)),
            pl.BlockSpec((RET_V_DIM, D_MODEL), lambda bi, s, h: (h, 0)),
            pl.BlockSpec((1, RET_CHUNK, RET_CHUNK), lambda bi, s, h: (h, 0, 0)),
            pl.BlockSpec((1, RET_CHUNK, 1), lambda bi, s, h: (h, 0, 0)),
            pl.BlockSpec((1, RET_CHUNK, 1), lambda bi, s, h: (h, 0, 0)),
            pl.BlockSpec((1, 1, 1), lambda bi, s, h: (h, 0, 0)),
        ],
        out_specs=pl.BlockSpec((1, ts, D_MODEL), lambda bi, s, h: (bi, s, 0)),
        out_shape=jax.ShapeDtypeStruct(x.shape, F32),
        scratch_shapes=[
            pltpu.VMEM((RET_HEADS, RET_QK_DIM, RET_V_DIM), F32),
            pltpu.VMEM((ts, D_MODEL), F32),
        ],
        compiler_params=_params("parallel", "arbitrary", "arbitrary"),
        name="ret_core",
    )(proj, proj, proj, proj, x, wout_bf, intra, qdec, kdec, cdec)


def _ffn_kernel(x_ref, g_ref, wg_ref, wu_ref, wd_ref, o_ref, xn_ref, acc_ref):
    c = pl.program_id(1)

    @pl.when(c == 0)
    def _():
        xn_ref[...] = _rmsnorm_f32(x_ref[...], g_ref[...]).astype(BF16)

    xn = xn_ref[...]
    hg = jnp.dot(xn, wg_ref[...], preferred_element_type=F32)
    hu = jnp.dot(xn, wu_ref[...], preferred_element_type=F32)
    hid = (_silu(hg) * hu).astype(BF16)
    part = jnp.dot(hid, wd_ref[...], preferred_element_type=F32)

    @pl.when(c == 0)
    def _():
        acc_ref[...] = x_ref[...] + part

    @pl.when(c > 0)
    def _():
        acc_ref[...] += part

    @pl.when(c == pl.num_programs(1) - 1)
    def _():
        o_ref[...] = acc_ref[...]


def _ffn(x2, g, wg_bf, wu_bf, wd_bf):
    t = x2.shape[0]
    return pl.pallas_call(
        _ffn_kernel,
        grid=(t // TOKEN_TILE, D_FF // FF_TILE),
        in_specs=[
            pl.BlockSpec((TOKEN_TILE, D_MODEL), lambda i, c: (i, 0)),
            pl.BlockSpec((1, D_MODEL), lambda i, c: (0, 0)),
            pl.BlockSpec((D_MODEL, FF_TILE), lambda i, c: (0, c)),
            pl.BlockSpec((D_MODEL, FF_TILE), lambda i, c: (0, c)),
            pl.BlockSpec((FF_TILE, D_MODEL), lambda i, c: (c, 0)),
        ],
        out_specs=pl.BlockSpec((TOKEN_TILE, D_MODEL), lambda i, c: (i, 0)),
        out_shape=jax.ShapeDtypeStruct(x2.shape, F32),
        scratch_shapes=[pltpu.VMEM((TOKEN_TILE, D_MODEL), BF16),
                        pltpu.VMEM((TOKEN_TILE, D_MODEL), F32)],
        compiler_params=_params("parallel", "arbitrary"),
        name="ffn_swiglu",
    )(x2, g, wg_bf, wu_bf, wd_bf)


def _conv_kernel(x_ref, g_ref, win_ref, wc_ref, wout_ref, o_ref, carry_ref):
    s = pl.program_id(1)
    ts = CONV_SEQ_TILE

    @pl.when(s == 0)
    def _():
        carry_ref[...] = jnp.zeros_like(carry_ref)

    x = x_ref[0]
    xn = _rmsnorm_f32(x, g_ref[...]).astype(BF16)
    proj = jnp.dot(xn, win_ref[...], preferred_element_type=F32)
    gate_b = proj[:, :D_MODEL]
    gate_c = proj[:, D_MODEL:2 * D_MODEL]
    hid = proj[:, 2 * D_MODEL:]
    u = gate_c * hid

    row = lax.broadcasted_iota(jnp.int32, (ts, D_MODEL), 0)
    prev1 = carry_ref[7:8, :]
    prev2 = carry_ref[6:7, :]
    u1 = jnp.where(row == 0, prev1, pltpu.roll(u, 1, axis=0))
    u2 = jnp.where(row == 0, prev2, jnp.where(row == 1, prev1, pltpu.roll(u, 2, axis=0)))
    carry_ref[...] = u[ts - 8:, :]

    wc = wc_ref[...]
    y = wc[2:3, :] * u + wc[1:2, :] * u1 + wc[0:1, :] * u2
    y = (gate_b * y).astype(BF16)
    o_ref[0] = x + jnp.dot(y, wout_ref[...], preferred_element_type=F32)


def _conv_layer(x, g, win_bf, wc, wout_bf):
    b, seq, _ = x.shape
    ts = CONV_SEQ_TILE
    return pl.pallas_call(
        _conv_kernel,
        grid=(b, seq // ts),
        in_specs=[
            pl.BlockSpec((1, ts, D_MODEL), lambda bi, s: (bi, s, 0)),
            pl.BlockSpec((1, D_MODEL), lambda bi, s: (0, 0)),
            pl.BlockSpec((D_MODEL, 3 * D_MODEL), lambda bi, s: (0, 0)),
            pl.BlockSpec((CONV_WIDTH, D_MODEL), lambda bi, s: (0, 0)),
            pl.BlockSpec((D_MODEL, D_MODEL), lambda bi, s: (0, 0)),
        ],
        out_specs=pl.BlockSpec((1, ts, D_MODEL), lambda bi, s: (bi, s, 0)),
        out_shape=jax.ShapeDtypeStruct(x.shape, F32),
        scratch_shapes=[pltpu.VMEM((8, D_MODEL), F32)],
        compiler_params=_params("parallel", "arbitrary"),
        name="short_conv",
    )(x, g, win_bf, wc, wout_bf)


def _router_kernel(x_ref, g_ref, wr_ref, xn_ref, gates_ref):
    xn = _rmsnorm_f32(x_ref[...], g_ref[...])
    xn_ref[...] = xn.astype(BF16)
    logits = jnp.dot(xn, wr_ref[...], preferred_element_type=F32,
                     precision=lax.Precision.HIGHEST)
    lane = lax.broadcasted_iota(jnp.int32, logits.shape, 1).astype(F32)
    none = float(N_EXPERTS)
    m1 = jnp.max(logits, axis=-1, keepdims=True)
    i1 = jnp.min(jnp.where(logits == m1, lane, none), axis=-1, keepdims=True)
    rest = jnp.where(lane == i1, -jnp.inf, logits)
    m2 = jnp.max(rest, axis=-1, keepdims=True)
    i2 = jnp.min(jnp.where(rest == m2, lane, none), axis=-1, keepdims=True)
    e2 = jnp.exp(m2 - m1)
    denom = 1.0 + e2
    gates_ref[...] = jnp.where(lane == i1, 1.0 / denom,
                               jnp.where(lane == i2, e2 / denom, 0.0))


def _router(x2, g, wr):
    t = x2.shape[0]
    return pl.pallas_call(
        _router_kernel,
        grid=(t // TOKEN_TILE,),
        in_specs=[
            pl.BlockSpec((TOKEN_TILE, D_MODEL), lambda i: (i, 0)),
            pl.BlockSpec((1, D_MODEL), lambda i: (0, 0)),
            pl.BlockSpec((D_MODEL, N_EXPERTS), lambda i: (0, 0)),
        ],
        out_specs=[
            pl.BlockSpec((TOKEN_TILE, D_MODEL), lambda i: (i, 0)),
            pl.BlockSpec((TOKEN_TILE, N_EXPERTS), lambda i: (i, 0)),
        ],
        out_shape=[jax.ShapeDtypeStruct((t, D_MODEL), BF16),
                   jax.ShapeDtypeStruct((t, N_EXPERTS), F32)],
        compiler_params=_params("parallel"),
        name="router",
    )(x2, g, wr)


def _moe_kernel(xn_ref, gates_ref, x_ref, gf_ref, wg_ref, wu_ref, wd_ref, o_ref, acc_ref):
    e = pl.program_id(1)
    c = pl.program_id(2)
    first = jnp.logical_and(e == 0, c == 0)
    last = jnp.logical_and(e == pl.num_programs(1) - 1, c == pl.num_programs(2) - 1)

    xn = xn_ref[...]
    hg = jnp.dot(xn, wg_ref[0], preferred_element_type=F32)
    hu = jnp.dot(xn, wu_ref[0], preferred_element_type=F32)
    hid = (_silu(hg) * hu).astype(BF16)
    part = jnp.dot(hid, wd_ref[0], preferred_element_type=F32)

    gates = gates_ref[...]
    lane = lax.broadcasted_iota(jnp.int32, gates.shape, 1)
    gate = jnp.sum(jnp.where(lane == e, gates, 0.0), axis=-1, keepdims=True)
    part = gate * part

    @pl.when(first)
    def _():
        acc_ref[...] = x_ref[...] + part

    @pl.when(jnp.logical_not(first))
    def _():
        acc_ref[...] += part

    @pl.when(last)
    def _():
        o_ref[...] = _rmsnorm_f32(acc_ref[...], gf_ref[...])


def _moe_dense(xn_bf, gates, x2, g_final, wg_bf, wu_bf, wd_bf):
    t = x2.shape[0]
    return pl.pallas_call(
        _moe_kernel,
        grid=(t // TOKEN_TILE, N_EXPERTS, D_FF // FF_TILE),
        in_specs=[
            pl.BlockSpec((TOKEN_TILE, D_MODEL), lambda i, e, c: (i, 0)),
            pl.BlockSpec((TOKEN_TILE, N_EXPERTS), lambda i, e, c: (i, 0)),
            pl.BlockSpec((TOKEN_TILE, D_MODEL), lambda i, e, c: (i, 0)),
            pl.BlockSpec((1, D_MODEL), lambda i, e, c: (0, 0)),
            pl.BlockSpec((1, D_MODEL, FF_TILE), lambda i, e, c: (e, 0, c)),
            pl.BlockSpec((1, D_MODEL, FF_TILE), lambda i, e, c: (e, 0, c)),
            pl.BlockSpec((1, FF_TILE, D_MODEL), lambda i, e, c: (e, c, 0)),
        ],
        out_specs=pl.BlockSpec((TOKEN_TILE, D_MODEL), lambda i, e, c: (i, 0)),
        out_shape=jax.ShapeDtypeStruct(x2.shape, F32),
        scratch_shapes=[pltpu.VMEM((TOKEN_TILE, D_MODEL), F32)],
        compiler_params=_params("parallel", "arbitrary", "arbitrary"),
        name="moe_dense",
    )(xn_bf, gates, x2, g_final, wg_bf, wu_bf, wd_bf)


def _rotary_tables(seq):
    inv_freq = ROPE_BASE ** (-jnp.arange(0, RET_QK_DIM, 2, dtype=F32) / RET_QK_DIM)
    ang = jnp.arange(seq, dtype=F32)[:, None] * inv_freq[None, :]
    return jnp.cos(ang), jnp.sin(ang)


def _decay_tables():
    c = RET_CHUNK
    log_gamma = jnp.log(1.0 - 2.0 ** (-5.0 - jnp.arange(RET_HEADS, dtype=F32)))
    idx = jnp.arange(c, dtype=F32)
    diff = idx[:, None] - idx[None, :]
    causal = diff >= 0
    intra = jnp.where(causal[None],
                      jnp.exp(jnp.where(causal, diff, 0.0)[None] * log_gamma[:, None, None]), 0.0)
    qdec = jnp.exp((idx[None, :] + 1.0) * log_gamma[:, None])[..., None]
    kdec = jnp.exp((c - 1.0 - idx[None, :]) * log_gamma[:, None])[..., None]
    cdec = jnp.exp(c * log_gamma)[:, None, None]
    return intra, qdec, kdec, cdec


def kernel(x, norm_mix0, ret_w_in, ret_w_out, norm_ffn0, ffn_w_gate, ffn_w_up, ffn_w_down,
           norm_mix1, conv_w_in, conv_w, conv_w_out, norm_ffn1, moe_router, moe_w_gate,
           moe_w_up, moe_w_down, norm_final):
    b, seq, d = x.shape
    t = b * seq
    row = lambda g: g.reshape(1, d).astype(F32)
    bf = lambda w: w.astype(BF16)

    cos, sin = _rotary_tables(seq)
    intra, qdec, kdec, cdec = _decay_tables()

    proj = _ret_in(x.reshape(t, d), row(norm_mix0), bf(ret_w_in), cos, sin, seq)
    h = _ret_core(proj, x, bf(ret_w_out), intra, qdec, kdec, cdec)
    h = _ffn(h.reshape(t, d), row(norm_ffn0), bf(ffn_w_gate), bf(ffn_w_up), bf(ffn_w_down))
    h = _conv_layer(h.reshape(b, seq, d), row(norm_mix1), bf(conv_w_in),
                    conv_w.astype(F32), bf(conv_w_out))
    h2 = h.reshape(t, d)
    xn, gates = _router(h2, row(norm_ffn1), moe_router.astype(F32))
    out = _moe_dense(xn, gates, h2, row(norm_final), bf(moe_w_gate), bf(moe_w_up), bf(moe_w_down))
    return out.reshape(b, seq, d)
```

```python
import functools

import jax
import jax.numpy as jnp
from jax import lax
from jax.experimental import pallas as pl
from jax.experimental.pallas import tpu as pltpu

D_MODEL = 1024
RET_HEADS = 4
RET_QK_DIM = D_MODEL // RET_HEADS
RET_V_DIM = 2 * RET_QK_DIM
RET_QK_TOTAL = RET_HEADS * RET_QK_DIM
RET_V_TOTAL = RET_HEADS * RET_V_DIM
RET_IN_COLS = 2 * RET_QK_TOTAL + 2 * RET_V_TOTAL
ROPE_BASE = 10000.0
CONV_WIDTH = 3
D_FF = 7 * D_MODEL // 2
N_EXPERTS = 8
NORM_EPS = 1e-6

RET_CHUNK = 256
HALF = RET_QK_DIM // 2

VMEM_LIMIT_BYTES = 56 * 1024 * 1024

TOKEN_TILE = 1024
RET_SEQ_TILE = 1024
CONV_SEQ_TILE = 512
FF_TILE = 896
IN_COL_TILE = 1024
ROUTE_WINDOW = 512
EXPERT_TILE = 1024
GATHER_TILE = 256
COMBINE_BLOCK = 256
LANES = 128

BF16 = jnp.bfloat16
F32 = jnp.float32


def _params(*semantics):
    return pltpu.CompilerParams(dimension_semantics=semantics,
                                vmem_limit_bytes=VMEM_LIMIT_BYTES)


def _rmsnorm_f32(x, g):
    return x * lax.rsqrt(jnp.mean(x * x, axis=-1, keepdims=True) + NORM_EPS) * g


def _silu(x):
    return x * (1.0 / (1.0 + jnp.exp(-x)))


def _ret_in_kernel(x_ref, g_ref, w_ref, cos_ref, sin_ref, o_ref, xn_ref):
    n = pl.program_id(1)

    @pl.when(n == 0)
    def _():
        xn_ref[...] = _rmsnorm_f32(x_ref[...], g_ref[...]).astype(BF16)

    acc = jnp.dot(xn_ref[...], w_ref[...], preferred_element_type=F32)

    def rotary(scale):
        cos = cos_ref[...]
        sin = sin_ref[...]
        for h in range(IN_COL_TILE // RET_QK_DIM):
            lo = h * RET_QK_DIM
            t1 = acc[:, lo:lo + HALF]
            t2 = acc[:, lo + HALF:lo + RET_QK_DIM]
            o_ref[:, lo:lo + HALF] = ((t1 * cos - t2 * sin) * scale).astype(BF16)
            o_ref[:, lo + HALF:lo + RET_QK_DIM] = ((t1 * sin + t2 * cos) * scale).astype(BF16)

    @pl.when(n == 0)
    def _():
        rotary(1.0)

    @pl.when(n == 1)
    def _():
        rotary(RET_QK_DIM ** -0.5)

    @pl.when(n >= 2)
    def _():
        o_ref[...] = acc.astype(BF16)


def _ret_in(x2, g, w_bf, cos, sin, seq):
    t = x2.shape[0]
    seq_tiles = seq // TOKEN_TILE
    return pl.pallas_call(
        _ret_in_kernel,
        grid=(t // TOKEN_TILE, RET_IN_COLS // IN_COL_TILE),
        in_specs=[
            pl.BlockSpec((TOKEN_TILE, D_MODEL), lambda i, n: (i, 0)),
            pl.BlockSpec((1, D_MODEL), lambda i, n: (0, 0)),
            pl.BlockSpec((D_MODEL, IN_COL_TILE), lambda i, n: (0, n)),
            pl.BlockSpec((TOKEN_TILE, HALF), lambda i, n: (i % seq_tiles, 0)),
            pl.BlockSpec((TOKEN_TILE, HALF), lambda i, n: (i % seq_tiles, 0)),
        ],
        out_specs=pl.BlockSpec((TOKEN_TILE, IN_COL_TILE), lambda i, n: (i, n)),
        out_shape=jax.ShapeDtypeStruct((t, RET_IN_COLS), BF16),
        scratch_shapes=[pltpu.VMEM((TOKEN_TILE, D_MODEL), BF16)],
        compiler_params=_params("parallel", "arbitrary"),
        name="ret_in_proj",
    )(x2, g, w_bf, cos, sin)


def _ret_core_kernel(q_ref, k_ref, v_ref, gate_ref, x_ref, wout_ref,
                     intra_ref, qdec_ref, kdec_ref, cdec_ref,
                     o_ref, state_ref, acc_ref):
    s = pl.program_id(1)
    h = pl.program_id(2)

    @pl.when(s == 0)
    def _():
        state_ref[h] = jnp.zeros((RET_QK_DIM, RET_V_DIM), F32)

    intra = intra_ref[0]
    qdec = qdec_ref[0]
    kdec = kdec_ref[0]
    cdec = cdec_ref[0]

    state = state_ref[h]
    outs = []
    for c in range(RET_SEQ_TILE // RET_CHUNK):
        rows = slice(c * RET_CHUNK, (c + 1) * RET_CHUNK)
        q = q_ref[rows, :]
        k = k_ref[rows, :]
        v = v_ref[rows, :]
        scores = lax.dot_general(q, k, (((1,), (1,)), ((), ())),
                                 preferred_element_type=F32) * intra
        inner = jnp.dot(scores.astype(BF16), v, preferred_element_type=F32)
        qd = (q.astype(F32) * qdec).astype(BF16)
        cross = jnp.dot(qd, state.astype(BF16), preferred_element_type=F32)
        kd = (k.astype(F32) * kdec).astype(BF16)
        state = state * cdec + lax.dot_general(
            kd, v, (((0,), (0,)), ((), ())), preferred_element_type=F32)
        o = inner + cross
        o = o * lax.rsqrt(jnp.mean(o * o, axis=-1, keepdims=True) + NORM_EPS)
        gate = gate_ref[rows, :].astype(F32)
        outs.append((_silu(gate) * o).astype(BF16))
    state_ref[h] = state

    og = jnp.concatenate(outs, axis=0)
    part = jnp.dot(og, wout_ref[...], preferred_element_type=F32)

    @pl.when(h == 0)
    def _():
        acc_ref[...] = x_ref[0] + part

    @pl.when(h > 0)
    def _():
        acc_ref[...] += part

    @pl.when(h == RET_HEADS - 1)
    def _():
        o_ref[0] = acc_ref[...]


def _ret_core(proj, x, wout_bf, intra, qdec, kdec, cdec):
    b, seq, _ = x.shape
    ts = RET_SEQ_TILE
    st = seq // ts
    qk_blocks = RET_QK_TOTAL // RET_QK_DIM
    v_off = 2 * RET_QK_TOTAL // RET_V_DIM
    g_off = v_off + RET_HEADS
    return pl.pallas_call(
        _ret_core_kernel,
        grid=(b, st, RET_HEADS),
        in_specs=[
            pl.BlockSpec((ts, RET_QK_DIM), lambda bi, s, h: (bi * st + s, h)),
            pl.BlockSpec((ts, RET_QK_DIM), lambda bi, s, h: (bi * st + s, qk_blocks + h)),
            pl.BlockSpec((ts, RET_V_DIM), lambda bi, s, h: (bi * st + s, v_off + h)),
            pl.BlockSpec((ts, RET_V_DIM), lambda bi, s, h: (bi * st + s, g_off + h)),
            pl.BlockSpec((1, ts, D_MODEL), lambda bi, s, h: (bi, s, 0)),
            pl.BlockSpec((RET_V_DIM, D_MODEL), lambda bi, s, h: (h, 0)),
            pl.BlockSpec((1, RET_CHUNK, RET_CHUNK), lambda bi, s, h: (h, 0, 0)),
            pl.BlockSpec((1, RET_CHUNK, 1), lambda bi, s, h: (h, 0, 0)),
            pl.BlockSpec((1, RET_CHUNK, 1), lambda bi, s, h: (h, 0, 0)),
            pl.BlockSpec((1, 1, 1), lambda bi, s, h: (h, 0, 0)),
        ],
        out_specs=pl.BlockSpec((1, ts, D_MODEL), lambda bi, s, h: (bi, s, 0)),
        out_shape=jax.ShapeDtypeStruct(x.shape, F32),
        scratch_shapes=[
            pltpu.VMEM((RET_HEADS, RET_QK_DIM, RET_V_DIM), F32),
            pltpu.VMEM((ts, D_MODEL), F32),
        ],
        compiler_params=_params("parallel", "arbitrary", "arbitrary"),
        name="ret_core",
    )(proj, proj, proj, proj, x, wout_bf, intra, qdec, kdec, cdec)


def _ffn_kernel(x_ref, g_ref, wg_ref, wu_ref, wd_ref, o_ref, xn_ref, acc_ref):
    c = pl.program_id(1)

    @pl.when(c == 0)
    def _():
        xn_ref[...] = _rmsnorm_f32(x_ref[...], g_ref[...]).astype(BF16)

    xn = xn_ref[...]
    hg = jnp.dot(xn, wg_ref[...], preferred_element_type=F32)
    hu = jnp.dot(xn, wu_ref[...], preferred_element_type=F32)
    hid = (_silu(hg) * hu).astype(BF16)
    part = jnp.dot(hid, wd_ref[...], preferred_element_type=F32)

    @pl.when(c == 0)
    def _():
        acc_ref[...] = x_ref[...] + part

    @pl.when(c > 0)
    def _():
        acc_ref[...] += part

    @pl.when(c == pl.num_programs(1) - 1)
    def _():
        o_ref[...] = acc_ref[...]


def _ffn(x2, g, wg_bf, wu_bf, wd_bf):
    t = x2.shape[0]
    return pl.pallas_call(
        _ffn_kernel,
        grid=(t // TOKEN_TILE, D_FF // FF_TILE),
        in_specs=[
            pl.BlockSpec((TOKEN_TILE, D_MODEL), lambda i, c: (i, 0)),
            pl.BlockSpec((1, D_MODEL), lambda i, c: (0, 0)),
            pl.BlockSpec((D_MODEL, FF_TILE), lambda i, c: (0, c)),
            pl.BlockSpec((D_MODEL, FF_TILE), lambda i, c: (0, c)),
            pl.BlockSpec((FF_TILE, D_MODEL), lambda i, c: (c, 0)),
        ],
        out_specs=pl.BlockSpec((TOKEN_TILE, D_MODEL), lambda i, c: (i, 0)),
        out_shape=jax.ShapeDtypeStruct(x2.shape, F32),
        scratch_shapes=[pltpu.VMEM((TOKEN_TILE, D_MODEL), BF16),
                        pltpu.VMEM((TOKEN_TILE, D_MODEL), F32)],
        compiler_params=_params("parallel", "arbitrary"),
        name="ffn_swiglu",
    )(x2, g, wg_bf, wu_bf, wd_bf)


def _conv_kernel(x_ref, g_ref, win_ref, wc_ref, wout_ref, o_ref, carry_ref):
    s = pl.program_id(1)
    ts = CONV_SEQ_TILE

    @pl.when(s == 0)
    def _():
        carry_ref[...] = jnp.zeros_like(carry_ref)

    x = x_ref[0]
    xn = _rmsnorm_f32(x, g_ref[...]).astype(BF16)
    proj = jnp.dot(xn, win_ref[...], preferred_element_type=F32)
    gate_b = proj[:, :D_MODEL]
    gate_c = proj[:, D_MODEL:2 * D_MODEL]
    hid = proj[:, 2 * D_MODEL:]
    u = gate_c * hid

    row = lax.broadcasted_iota(jnp.int32, (ts, D_MODEL), 0)
    prev1 = carry_ref[7:8, :]
    prev2 = carry_ref[6:7, :]
    u1 = jnp.where(row == 0, prev1, pltpu.roll(u, 1, axis=0))
    u2 = jnp.where(row == 0, prev2, jnp.where(row == 1, prev1, pltpu.roll(u, 2, axis=0)))
    carry_ref[...] = u[ts - 8:, :]

    wc = wc_ref[...]
    y = wc[2:3, :] * u + wc[1:2, :] * u1 + wc[0:1, :] * u2
    y = (gate_b * y).astype(BF16)
    o_ref[0] = x + jnp.dot(y, wout_ref[...], preferred_element_type=F32)


def _conv_layer(x, g, win_bf, wc, wout_bf):
    b, seq, _ = x.shape
    ts = CONV_SEQ_TILE
    return pl.pallas_call(
        _conv_kernel,
        grid=(b, seq // ts),
        in_specs=[
            pl.BlockSpec((1, ts, D_MODEL), lambda bi, s: (bi, s, 0)),
            pl.BlockSpec((1, D_MODEL), lambda bi, s: (0, 0)),
            pl.BlockSpec((D_MODEL, 3 * D_MODEL), lambda bi, s: (0, 0)),
            pl.BlockSpec((CONV_WIDTH, D_MODEL), lambda bi, s: (0, 0)),
            pl.BlockSpec((D_MODEL, D_MODEL), lambda bi, s: (0, 0)),
        ],
        out_specs=pl.BlockSpec((1, ts, D_MODEL), lambda bi, s: (bi, s, 0)),
        out_shape=jax.ShapeDtypeStruct(x.shape, F32),
        scratch_shapes=[pltpu.VMEM((8, D_MODEL), F32)],
        compiler_params=_params("parallel", "arbitrary"),
        name="short_conv",
    )(x, g, win_bf, wc, wout_bf)


def _router_kernel(x_ref, g_ref, wrt_ref, xn_ref, rank_ref, gates_ref, cend_ref, carry_ref):
    w = ROUTE_WINDOW

    @pl.when(pl.program_id(0) == 0)
    def _():
        carry_ref[...] = jnp.zeros_like(carry_ref)

    xn = _rmsnorm_f32(x_ref[...], g_ref[...])
    xn_ref[...] = xn.astype(BF16)
    logits = lax.dot_general(wrt_ref[...], xn, (((1,), (1,)), ((), ())),
                             preferred_element_type=F32,
                             precision=lax.Precision.HIGHEST)
    sub = lax.broadcasted_iota(jnp.int32, logits.shape, 0).astype(F32)
    none = float(N_EXPERTS)
    m1 = jnp.max(logits, axis=0, keepdims=True)
    i1 = jnp.min(jnp.where(logits == m1, sub, none), axis=0, keepdims=True)
    rest = jnp.where(sub == i1, -jnp.inf, logits)
    m2 = jnp.max(rest, axis=0, keepdims=True)
    i2 = jnp.min(jnp.where(rest == m2, sub, none), axis=0, keepdims=True)
    e2 = jnp.exp(m2 - m1)
    denom = 1.0 + e2
    gates_ref[...] = jnp.where(sub == i1, 1.0 / denom, jnp.where(sub == i2, e2 / denom, 0.0))

    chosen = jnp.logical_or(sub == i1, sub == i2)
    before = (lax.broadcasted_iota(jnp.int32, (w, w), 0)
              < lax.broadcasted_iota(jnp.int32, (w, w), 1))
    prefix = jnp.dot(jnp.where(chosen, 1.0, 0.0).astype(BF16),
                     jnp.where(before, 1.0, 0.0).astype(BF16),
                     preferred_element_type=F32)
    carry = carry_ref[...]
    rank_ref[...] = jnp.where(chosen, prefix + carry[:, :1], -1.0)
    carry = carry + jnp.sum(jnp.where(chosen, 1.0, 0.0), axis=1, keepdims=True)
    carry_ref[...] = carry
    cend_ref[0] = carry


def _router(x2, g, wrt):
    t = x2.shape[0]
    w = ROUTE_WINDOW
    return pl.pallas_call(
        _router_kernel,
        grid=(t // w,),
        in_specs=[
            pl.BlockSpec((w, D_MODEL), lambda i: (i, 0)),
            pl.BlockSpec((1, D_MODEL), lambda i: (0, 0)),
            pl.BlockSpec((N_EXPERTS, D_MODEL), lambda i: (0, 0)),
        ],
        out_specs=[
            pl.BlockSpec((w, D_MODEL), lambda i: (i, 0)),
            pl.BlockSpec((N_EXPERTS, w), lambda i: (0, i)),
            pl.BlockSpec((N_EXPERTS, w), lambda i: (0, i)),
            pl.BlockSpec((1, N_EXPERTS, LANES), lambda i: (i, 0, 0)),
        ],
        out_shape=[jax.ShapeDtypeStruct((t, D_MODEL), BF16),
                   jax.ShapeDtypeStruct((N_EXPERTS, t), F32),
                   jax.ShapeDtypeStruct((N_EXPERTS, t), F32),
                   jax.ShapeDtypeStruct((t // w, N_EXPERTS, LANES), F32)],
        scratch_shapes=[pltpu.VMEM((N_EXPERTS, LANES), F32)],
        compiler_params=_params("arbitrary"),
        name="router",
    )(x2, g, wrt)


def _cell_lookup(table, rows, values):
    return jnp.sum(table[rows] <= values[:, None], axis=1).astype(jnp.int32) - 1


def _count_le(sorted_values, queries):
    return jnp.sum(sorted_values[None, :] <= queries[:, None], axis=1).astype(jnp.int32)


def _flatten_items(nitems, n_max):
    end = jnp.cumsum(nitems)
    start = end - nitems
    total = end[-1]
    k = jnp.minimum(jnp.arange(n_max, dtype=jnp.int32), total - 1)
    cell = jnp.minimum(_count_le(end, k), nitems.shape[0] - 1)
    valid = jnp.arange(n_max, dtype=jnp.int32) < total
    return cell, k - start[cell], start, end, valid


def _routing_plan(cend, t):
    i32 = jnp.int32
    nw = t // ROUTE_WINDOW
    cend_i = cend[:, :, 0].astype(i32).T
    cstart = jnp.concatenate([jnp.zeros((N_EXPERTS, 1), i32), cend_i[:, :-1]], axis=1)
    counts = cend_i[:, -1]
    ntiles = (counts + EXPERT_TILE - 1) // EXPERT_TILE
    tile_end = jnp.cumsum(ntiles)
    tile_start = tile_end - ntiles
    off = tile_start * EXPERT_TILE

    nt_max = 2 * t // EXPERT_TILE + N_EXPERTS
    ti = jnp.arange(nt_max, dtype=i32)
    tile_e = jnp.minimum(_count_le(tile_end, ti), N_EXPERTS - 1)
    tile_valid = (ti < tile_end[-1]).astype(i32)

    per = EXPERT_TILE // GATHER_TILE
    gj = jnp.arange(nt_max * per, dtype=i32)
    g_e = tile_e[gj // per]
    g_r0 = (gj // per - tile_start[g_e]) * EXPERT_TILE + (gj % per) * GATHER_TILE
    g_tile_valid = tile_valid[gj // per] > 0
    g_has_rows = jnp.logical_and(g_tile_valid, g_r0 < counts[g_e])
    g_last = jnp.minimum(g_r0 + GATHER_TILE, counts[g_e]) - 1
    wlo = jnp.where(g_has_rows, _cell_lookup(cstart, g_e, g_r0), 0)
    whi = jnp.where(g_has_rows, _cell_lookup(cstart, g_e, g_last), 0)
    g_n = jnp.where(g_tile_valid, whi - wlo + 1, 0)
    n_gather = N_EXPERTS * nw + nt_max * per
    cell, pos, start, end, valid = _flatten_items(g_n, n_gather)
    gather = dict(
        tile=cell, win=wlo[cell] + pos, e=g_e[cell], r0=g_r0[cell],
        first=jnp.logical_and(valid, pos == 0).astype(i32),
        valid=valid.astype(i32))

    c_w = jnp.arange(nw * N_EXPERTS, dtype=i32) // N_EXPERTS
    c_e = jnp.arange(nw * N_EXPERTS, dtype=i32) % N_EXPERTS
    lo = cstart[c_e, c_w]
    hi = cend_i[c_e, c_w]
    blo = (off[c_e] + lo) // COMBINE_BLOCK
    bhi = (off[c_e] + hi - 1) // COMBINE_BLOCK
    c_n = jnp.where(hi > lo, bhi - blo + 1, 0)
    n_combine = N_EXPERTS * nw + nt_max * (EXPERT_TILE // COMBINE_BLOCK)
    cell, pos, start, end, valid = _flatten_items(c_n, n_combine)
    k = jnp.minimum(jnp.arange(n_combine, dtype=i32), end[-1] - 1)
    win = c_w[cell]
    blk = blo[cell] + pos
    combine = dict(
        win=win, blk=blk, e=c_e[cell], r0=blk * COMBINE_BLOCK - off[c_e[cell]],
        first=jnp.logical_and(valid, k == start[win * N_EXPERTS]).astype(i32),
        last=jnp.logical_and(valid, k == end[win * N_EXPERTS + N_EXPERTS - 1] - 1).astype(i32),
        valid=valid.astype(i32))
    return tile_e, tile_valid, gather, combine


def _match(rank_row, r0, rows):
    want = (lax.broadcasted_iota(jnp.int32, (rows, rank_row.shape[1]), 0) + r0).astype(F32)
    return jnp.where(rank_row == want, 1.0, 0.0).astype(BF16)


def _gather_kernel(tile_s, win_s, e_s, r0_s, first_s, valid_s, xn_ref, rank_ref, o_ref):
    k = pl.program_id(0)
    e = e_s[k]

    @pl.when(valid_s[k] > 0)
    def _():
        onehot = _match(rank_ref[pl.ds(e, 1), :], r0_s[k], GATHER_TILE)
        rows = jnp.dot(onehot, xn_ref[...], preferred_element_type=F32).astype(BF16)

        @pl.when(first_s[k] > 0)
        def _():
            o_ref[...] = rows

        @pl.when(first_s[k] == 0)
        def _():
            o_ref[...] += rows


def _gather(plan, xn, rank, n_rows):
    w = ROUTE_WINDOW
    n_items = plan["tile"].shape[0]
    grid_spec = pltpu.PrefetchScalarGridSpec(
        num_scalar_prefetch=6,
        grid=(n_items,),
        in_specs=[
            pl.BlockSpec((w, D_MODEL), lambda k, tile, win, e, r0, first, valid: (win[k], 0)),
            pl.BlockSpec((N_EXPERTS, w), lambda k, tile, win, e, r0, first, valid: (0, win[k])),
        ],
        out_specs=pl.BlockSpec((GATHER_TILE, D_MODEL),
                               lambda k, tile, win, e, r0, first, valid: (tile[k], 0)),
    )
    return pl.pallas_call(
        _gather_kernel,
        grid_spec=grid_spec,
        out_shape=jax.ShapeDtypeStruct((n_rows, D_MODEL), BF16),
        compiler_params=_params("arbitrary"),
        name="moe_gather",
    )(plan["tile"], plan["win"], plan["e"], plan["r0"], plan["first"], plan["valid"], xn, rank)


def _expert_kernel(tile_e_s, tile_valid_s, xs_ref, wg_ref, wu_ref, wd_ref, y_ref, acc_ref):
    i = pl.program_id(0)
    c = pl.program_id(1)
    valid = tile_valid_s[i] > 0

    @pl.when(valid)
    def _():
        xs = xs_ref[...]
        hg = jnp.dot(xs, wg_ref[0], preferred_element_type=F32)
        hu = jnp.dot(xs, wu_ref[0], preferred_element_type=F32)
        hid = (_silu(hg) * hu).astype(BF16)
        part = jnp.dot(hid, wd_ref[0], preferred_element_type=F32)

        @pl.when(c == 0)
        def _():
            acc_ref[...] = part

        @pl.when(c > 0)
        def _():
            acc_ref[...] += part

    @pl.when(c == pl.num_programs(1) - 1)
    def _():
        y_ref[...] = jnp.where(valid, acc_ref[...], 0.0).astype(BF16)


def _experts(tile_e, tile_valid, xs, wg_bf, wu_bf, wd_bf):
    n_rows = xs.shape[0]
    grid_spec = pltpu.PrefetchScalarGridSpec(
        num_scalar_prefetch=2,
        grid=(n_rows // EXPERT_TILE, D_FF // FF_TILE),
        in_specs=[
            pl.BlockSpec((EXPERT_TILE, D_MODEL), lambda i, c, te, tv: (i * tv[i], 0)),
            pl.BlockSpec((1, D_MODEL, FF_TILE), lambda i, c, te, tv: (te[i], 0, c)),
            pl.BlockSpec((1, D_MODEL, FF_TILE), lambda i, c, te, tv: (te[i], 0, c)),
            pl.BlockSpec((1, FF_TILE, D_MODEL), lambda i, c, te, tv: (te[i], c, 0)),
        ],
        out_specs=pl.BlockSpec((EXPERT_TILE, D_MODEL), lambda i, c, te, tv: (i, 0)),
        scratch_shapes=[pltpu.VMEM((EXPERT_TILE, D_MODEL), F32)],
    )
    return pl.pallas_call(
        _expert_kernel,
        grid_spec=grid_spec,
        out_shape=jax.ShapeDtypeStruct((n_rows, D_MODEL), BF16),
        compiler_params=_params("arbitrary", "arbitrary"),
        name="moe_experts",
    )(tile_e, tile_valid, xs, wg_bf, wu_bf, wd_bf)


def _combine_kernel(win_s, blk_s, e_s, r0_s, first_s, last_s, valid_s,
                    y_ref, rank_ref, gates_ref, x_ref, gf_ref, o_ref, acc_ref):
    k = pl.program_id(0)
    e = e_s[k]
    w = ROUTE_WINDOW

    @pl.when(first_s[k] > 0)
    def _():
        acc_ref[...] = x_ref[...]

    @pl.when(valid_s[k] > 0)
    def _():
        onehot = _match(rank_ref[pl.ds(e, 1), :], r0_s[k], COMBINE_BLOCK)
        rows = lax.dot_general(onehot, y_ref[...], (((0,), (0,)), ((), ())),
                               preferred_element_type=F32)
        diag = (lax.broadcasted_iota(jnp.int32, (w, w), 0)
                == lax.broadcasted_iota(jnp.int32, (w, w), 1))
        gate = jnp.sum(jnp.where(diag, gates_ref[pl.ds(e, 1), :], 0.0), axis=1, keepdims=True)
        acc_ref[...] += gate * rows

    @pl.when(last_s[k] > 0)
    def _():
        o_ref[...] = _rmsnorm_f32(acc_ref[...], gf_ref[...])


def _combine(plan, y, rank, gates, x2, g_final):
    t = x2.shape[0]
    w = ROUTE_WINDOW
    n_items = plan["win"].shape[0]
    grid_spec = pltpu.PrefetchScalarGridSpec(
        num_scalar_prefetch=7,
        grid=(n_items,),
        in_specs=[
            pl.BlockSpec((COMBINE_BLOCK, D_MODEL), lambda k, win, blk, *_: (blk[k], 0)),
            pl.BlockSpec((N_EXPERTS, w), lambda k, win, blk, *_: (0, win[k])),
            pl.BlockSpec((N_EXPERTS, w), lambda k, win, blk, *_: (0, win[k])),
            pl.BlockSpec((w, D_MODEL), lambda k, win, blk, *_: (win[k], 0)),
            pl.BlockSpec((1, D_MODEL), lambda k, win, blk, *_: (0, 0)),
        ],
        out_specs=pl.BlockSpec((w, D_MODEL), lambda k, win, blk, *_: (win[k], 0)),
        scratch_shapes=[pltpu.VMEM((w, D_MODEL), F32)],
    )
    return pl.pallas_call(
        _combine_kernel,
        grid_spec=grid_spec,
        out_shape=jax.ShapeDtypeStruct((t, D_MODEL), F32),
        compiler_params=_params("arbitrary"),
        name="moe_combine",
    )(plan["win"], plan["blk"], plan["e"], plan["r0"], plan["first"], plan["last"],
      plan["valid"], y, rank, gates, x2, g_final)


def _rotary_tables(seq):
    inv_freq = ROPE_BASE ** (-jnp.arange(0, RET_QK_DIM, 2, dtype=F32) / RET_QK_DIM)
    ang = jnp.arange(seq, dtype=F32)[:, None] * inv_freq[None, :]
    return jnp.cos(ang), jnp.sin(ang)


def _decay_tables():
    c = RET_CHUNK
    log_gamma = jnp.log(1.0 - 2.0 ** (-5.0 - jnp.arange(RET_HEADS, dtype=F32)))
    idx = jnp.arange(c, dtype=F32)
    diff = idx[:, None] - idx[None, :]
    causal = diff >= 0
    intra = jnp.where(causal[None],
                      jnp.exp(jnp.where(causal, diff, 0.0)[None] * log_gamma[:, None, None]), 0.0)
    qdec = jnp.exp((idx[None, :] + 1.0) * log_gamma[:, None])[..., None]
    kdec = jnp.exp((c - 1.0 - idx[None, :]) * log_gamma[:, None])[..., None]
    cdec = jnp.exp(c * log_gamma)[:, None, None]
    return intra, qdec, kdec, cdec


def kernel(x, norm_mix0, ret_w_in, ret_w_out, norm_ffn0, ffn_w_gate, ffn_w_up, ffn_w_down,
           norm_mix1, conv_w_in, conv_w, conv_w_out, norm_ffn1, moe_router, moe_w_gate,
           moe_w_up, moe_w_down, norm_final):
    b, seq, d = x.shape
    t = b * seq
    row = lambda g: g.reshape(1, d).astype(F32)
    bf = lambda w: w.astype(BF16)

    cos, sin = _rotary_tables(seq)
    intra, qdec, kdec, cdec = _decay_tables()

    proj = _ret_in(x.reshape(t, d), row(norm_mix0), bf(ret_w_in), cos, sin, seq)
    h = _ret_core(proj, x, bf(ret_w_out), intra, qdec, kdec, cdec)
    h = _ffn(h.reshape(t, d), row(norm_ffn0), bf(ffn_w_gate), bf(ffn_w_up), bf(ffn_w_down))
    h = _conv_layer(h.reshape(b, seq, d), row(norm_mix1), bf(conv_w_in),
                    conv_w.astype(F32), bf(conv_w_out))
    h2 = h.reshape(t, d)
    xn, rank, gates, cend = _router(h2, row(norm_ffn1), moe_router.astype(F32).T)
    tile_e, tile_valid, gather_plan, combine_plan = _routing_plan(cend, t)
    n_rows = (2 * t // EXPERT_TILE + N_EXPERTS) * EXPERT_TILE
    xs = _gather(gather_plan, xn, rank, n_rows)
    y = _experts(tile_e, tile_valid, xs, bf(moe_w_gate), bf(moe_w_up), bf(moe_w_down))
    out = _combine(combine_plan, y, rank, gates, h2, row(norm_final))
    return out.reshape(b, seq, d)
```

```python
import functools

import jax
import jax.numpy as jnp
from jax import lax
from jax.experimental import pallas as pl
from jax.experimental.pallas import tpu as pltpu

D_MODEL = 1024
RET_HEADS = 4
RET_QK_DIM = D_MODEL // RET_HEADS
RET_V_DIM = 2 * RET_QK_DIM
RET_QK_TOTAL = RET_HEADS * RET_QK_DIM
RET_V_TOTAL = RET_HEADS * RET_V_DIM
RET_IN_COLS = 2 * RET_QK_TOTAL + 2 * RET_V_TOTAL
ROPE_BASE = 10000.0
CONV_WIDTH = 3
D_FF = 7 * D_MODEL // 2
N_EXPERTS = 8
NORM_EPS = 1e-6

RET_CHUNK = 256
HALF = RET_QK_DIM // 2

VMEM_LIMIT_BYTES = 56 * 1024 * 1024

TOKEN_TILE = 1024
RET_SEQ_TILE = 1024
CONV_SEQ_TILE = 512
FF_TILE = 1792
ROW_SUB = 512
IN_TOKEN_TILE = 512
IN_COL_TILE = 1024
ROUTE_WINDOW = 512
EXPERT_TILE = 1024
GATHER_TILE = 256
COMBINE_BLOCK = 256
LANES = 128

BF16 = jnp.bfloat16
F32 = jnp.float32


def _params(*semantics):
    return pltpu.CompilerParams(dimension_semantics=semantics,
                                vmem_limit_bytes=VMEM_LIMIT_BYTES)


def _rmsnorm_f32(x, g):
    return x * lax.rsqrt(jnp.mean(x * x, axis=-1, keepdims=True) + NORM_EPS) * g


def _silu(x):
    return x * (1.0 / (1.0 + jnp.exp(-x)))


def _ret_in_kernel(x_ref, g_ref, w_ref, cos_ref, sin_ref, o_ref):
    xn = _rmsnorm_f32(x_ref[...], g_ref[...]).astype(BF16)
    cos = cos_ref[...]
    sin = sin_ref[...]
    for n in range(RET_IN_COLS // IN_COL_TILE):
        cols = slice(n * IN_COL_TILE, (n + 1) * IN_COL_TILE)
        acc = jnp.dot(xn, w_ref[:, cols], preferred_element_type=F32)
        if n * IN_COL_TILE >= 2 * RET_QK_TOTAL:
            o_ref[:, cols] = acc.astype(BF16)
            continue
        scale = 1.0 if n * IN_COL_TILE < RET_QK_TOTAL else RET_QK_DIM ** -0.5
        for h in range(IN_COL_TILE // RET_QK_DIM):
            lo = h * RET_QK_DIM
            t1 = acc[:, lo:lo + HALF]
            t2 = acc[:, lo + HALF:lo + RET_QK_DIM]
            base = n * IN_COL_TILE + lo
            o_ref[:, base:base + HALF] = ((t1 * cos - t2 * sin) * scale).astype(BF16)
            o_ref[:, base + HALF:base + RET_QK_DIM] = ((t1 * sin + t2 * cos) * scale).astype(BF16)


def _ret_in(x2, g, w_bf, cos, sin, seq):
    t = x2.shape[0]
    tm = IN_TOKEN_TILE
    seq_tiles = seq // tm
    return pl.pallas_call(
        _ret_in_kernel,
        grid=(t // tm,),
        in_specs=[
            pl.BlockSpec((tm, D_MODEL), lambda i: (i, 0)),
            pl.BlockSpec((1, D_MODEL), lambda i: (0, 0)),
            pl.BlockSpec((D_MODEL, RET_IN_COLS), lambda i: (0, 0), pipeline_mode=pl.Buffered(1)),
            pl.BlockSpec((tm, HALF), lambda i: (i % seq_tiles, 0)),
            pl.BlockSpec((tm, HALF), lambda i: (i % seq_tiles, 0)),
        ],
        out_specs=pl.BlockSpec((tm, RET_IN_COLS), lambda i: (i, 0)),
        out_shape=jax.ShapeDtypeStruct((t, RET_IN_COLS), BF16),
        compiler_params=_params("parallel"),
        name="ret_in_proj",
    )(x2, g, w_bf, cos, sin)


def _ret_core_kernel(q_ref, k_ref, v_ref, gate_ref, x_ref, wout_ref,
                     intra_ref, qdec_ref, kdec_ref, cdec_ref,
                     o_ref, state_ref):
    s = pl.program_id(1)
    h = pl.program_id(2)

    @pl.when(s == 0)
    def _():
        state_ref[h] = jnp.zeros((RET_QK_DIM, RET_V_DIM), F32)

    @pl.when(h == 0)
    def _():
        o_ref[0] = x_ref[0]

    intra = intra_ref[0]
    qdec = qdec_ref[0]
    kdec = kdec_ref[0]
    cdec = cdec_ref[0]

    state = state_ref[h]
    for c in range(RET_SEQ_TILE // RET_CHUNK):
        rows = pl.ds(c * RET_CHUNK, RET_CHUNK)
        q = q_ref[rows, :]
        k = k_ref[rows, :]
        v = v_ref[rows, :]
        scores = lax.dot_general(q, k, (((1,), (1,)), ((), ())),
                                 preferred_element_type=F32) * intra
        inner = jnp.dot(scores.astype(BF16), v, preferred_element_type=F32)
        qd = (q.astype(F32) * qdec).astype(BF16)
        cross = jnp.dot(qd, state.astype(BF16), preferred_element_type=F32)
        kd = (k.astype(F32) * kdec).astype(BF16)
        state = state * cdec + lax.dot_general(
            kd, v, (((0,), (0,)), ((), ())), preferred_element_type=F32)
        o = inner + cross
        o = o * lax.rsqrt(jnp.mean(o * o, axis=-1, keepdims=True) + NORM_EPS)
        gate = gate_ref[rows, :].astype(F32)
        og = (_silu(gate) * o).astype(BF16)
        o_ref[0, rows, :] += jnp.dot(og, wout_ref[...], preferred_element_type=F32)
    state_ref[h] = state


def _ret_core(proj, x, wout_bf, intra, qdec, kdec, cdec):
    b, seq, _ = x.shape
    ts = RET_SEQ_TILE
    st = seq // ts
    qk_blocks = RET_QK_TOTAL // RET_QK_DIM
    v_off = 2 * RET_QK_TOTAL // RET_V_DIM
    g_off = v_off + RET_HEADS
    return pl.pallas_call(
        _ret_core_kernel,
        grid=(b, st, RET_HEADS),
        in_specs=[
            pl.BlockSpec((ts, RET_QK_DIM), lambda bi, s, h: (bi * st + s, h)),
            pl.BlockSpec((ts, RET_QK_DIM), lambda bi, s, h: (bi * st + s, qk_blocks + h)),
            pl.BlockSpec((ts, RET_V_DIM), lambda bi, s, h: (bi * st + s, v_off + h)),
            pl.BlockSpec((ts, RET_V_DIM), lambda bi, s, h: (bi * st + s, g_off + h)),
            pl.BlockSpec((1, ts, D_MODEL), lambda bi, s, h: (bi, s, 0)),
            pl.BlockSpec((RET_V_DIM, D_MODEL), lambda bi, s, h: (h, 0)),
            pl.BlockSpec((1, RET_CHUNK, RET_CHUNK), lambda bi, s, h: (h, 0, 0)),
            pl.BlockSpec((1, RET_CHUNK, 1), lambda bi, s, h: (h, 0, 0)),
            pl.BlockSpec((1, RET_CHUNK, 1), lambda bi, s, h: (h, 0, 0)),
            pl.BlockSpec((1, 1, 1), lambda bi, s, h: (h, 0, 0)),
        ],
        out_specs=pl.BlockSpec((1, ts, D_MODEL), lambda bi, s, h: (bi, s, 0)),
        out_shape=jax.ShapeDtypeStruct(x.shape, F32),
        scratch_shapes=[pltpu.VMEM((RET_HEADS, RET_QK_DIM, RET_V_DIM), F32)],
        compiler_params=_params("parallel", "arbitrary", "arbitrary"),
        name="ret_core",
    )(proj, proj, proj, proj, x, wout_bf, intra, qdec, kdec, cdec)


def _swiglu_part(x_bf, wg, wu, wd):
    hg = jnp.dot(x_bf, wg, preferred_element_type=F32)
    hu = jnp.dot(x_bf, wu, preferred_element_type=F32)
    hid = (_silu(hg) * hu).astype(BF16)
    return jnp.dot(hid, wd, preferred_element_type=F32)


def _ffn_kernel(x_ref, g_ref, wg_ref, wu_ref, wd_ref, o_ref):
    @pl.when(pl.program_id(1) == 0)
    def _():
        o_ref[...] = x_ref[...]

    for r in range(TOKEN_TILE // ROW_SUB):
        rows = pl.ds(r * ROW_SUB, ROW_SUB)
        xn = _rmsnorm_f32(x_ref[rows, :], g_ref[...]).astype(BF16)
        o_ref[rows, :] += _swiglu_part(xn, wg_ref[...], wu_ref[...], wd_ref[...])


def _ffn(x2, g, wg_bf, wu_bf, wd_bf):
    t = x2.shape[0]
    return pl.pallas_call(
        _ffn_kernel,
        grid=(t // TOKEN_TILE, D_FF // FF_TILE),
        in_specs=[
            pl.BlockSpec((TOKEN_TILE, D_MODEL), lambda i, c: (i, 0)),
            pl.BlockSpec((1, D_MODEL), lambda i, c: (0, 0)),
            pl.BlockSpec((D_MODEL, FF_TILE), lambda i, c: (0, c)),
            pl.BlockSpec((D_MODEL, FF_TILE), lambda i, c: (0, c)),
            pl.BlockSpec((FF_TILE, D_MODEL), lambda i, c: (c, 0)),
        ],
        out_specs=pl.BlockSpec((TOKEN_TILE, D_MODEL), lambda i, c: (i, 0)),
        out_shape=jax.ShapeDtypeStruct(x2.shape, F32),
        compiler_params=_params("parallel", "arbitrary"),
        name="ffn_swiglu",
    )(x2, g, wg_bf, wu_bf, wd_bf)


def _conv_kernel(x_ref, g_ref, win_ref, wc_ref, wout_ref, o_ref, carry_ref):
    s = pl.program_id(1)
    ts = CONV_SEQ_TILE

    @pl.when(s == 0)
    def _():
        carry_ref[...] = jnp.zeros_like(carry_ref)

    x = x_ref[0]
    xn = _rmsnorm_f32(x, g_ref[...]).astype(BF16)
    proj = jnp.dot(xn, win_ref[...], preferred_element_type=F32)
    gate_b = proj[:, :D_MODEL]
    gate_c = proj[:, D_MODEL:2 * D_MODEL]
    hid = proj[:, 2 * D_MODEL:]
    u = gate_c * hid

    row = lax.broadcasted_iota(jnp.int32, (ts, D_MODEL), 0)
    prev1 = carry_ref[7:8, :]
    prev2 = carry_ref[6:7, :]
    u1 = jnp.where(row == 0, prev1, pltpu.roll(u, 1, axis=0))
    u2 = jnp.where(row == 0, prev2, jnp.where(row == 1, prev1, pltpu.roll(u, 2, axis=0)))
    carry_ref[...] = u[ts - 8:, :]

    wc = wc_ref[...]
    y = wc[2:3, :] * u + wc[1:2, :] * u1 + wc[0:1, :] * u2
    y = (gate_b * y).astype(BF16)
    o_ref[0] = x + jnp.dot(y, wout_ref[...], preferred_element_type=F32)


def _conv_layer(x, g, win_bf, wc, wout_bf):
    b, seq, _ = x.shape
    ts = CONV_SEQ_TILE
    return pl.pallas_call(
        _conv_kernel,
        grid=(b, seq // ts),
        in_specs=[
            pl.BlockSpec((1, ts, D_MODEL), lambda bi, s: (bi, s, 0)),
            pl.BlockSpec((1, D_MODEL), lambda bi, s: (0, 0)),
            pl.BlockSpec((D_MODEL, 3 * D_MODEL), lambda bi, s: (0, 0)),
            pl.BlockSpec((CONV_WIDTH, D_MODEL), lambda bi, s: (0, 0)),
            pl.BlockSpec((D_MODEL, D_MODEL), lambda bi, s: (0, 0)),
        ],
        out_specs=pl.BlockSpec((1, ts, D_MODEL), lambda bi, s: (bi, s, 0)),
        out_shape=jax.ShapeDtypeStruct(x.shape, F32),
        scratch_shapes=[pltpu.VMEM((8, D_MODEL), F32)],
        compiler_params=_params("parallel", "arbitrary"),
        name="short_conv",
    )(x, g, win_bf, wc, wout_bf)


def _router_kernel(x_ref, g_ref, wrt_ref, xn_ref, rank_ref, gates_ref, cend_ref, carry_ref):
    w = ROUTE_WINDOW

    @pl.when(pl.program_id(0) == 0)
    def _():
        carry_ref[...] = jnp.zeros_like(carry_ref)

    xn = _rmsnorm_f32(x_ref[...], g_ref[...])
    xn_ref[...] = xn.astype(BF16)
    logits = lax.dot_general(wrt_ref[...], xn, (((1,), (1,)), ((), ())),
                             preferred_element_type=F32,
                             precision=lax.Precision.HIGHEST)
    sub = lax.broadcasted_iota(jnp.int32, logits.shape, 0).astype(F32)
    none = float(N_EXPERTS)
    m1 = jnp.max(logits, axis=0, keepdims=True)
    i1 = jnp.min(jnp.where(logits == m1, sub, none), axis=0, keepdims=True)
    rest = jnp.where(sub == i1, -jnp.inf, logits)
    m2 = jnp.max(rest, axis=0, keepdims=True)
    i2 = jnp.min(jnp.where(rest == m2, sub, none), axis=0, keepdims=True)
    e2 = jnp.exp(m2 - m1)
    denom = 1.0 + e2
    gates_ref[...] = jnp.where(sub == i1, 1.0 / denom, jnp.where(sub == i2, e2 / denom, 0.0))

    chosen = jnp.logical_or(sub == i1, sub == i2)
    before = (lax.broadcasted_iota(jnp.int32, (w, w), 0)
              < lax.broadcasted_iota(jnp.int32, (w, w), 1))
    prefix = jnp.dot(jnp.where(chosen, 1.0, 0.0).astype(BF16),
                     jnp.where(before, 1.0, 0.0).astype(BF16),
                     preferred_element_type=F32)
    carry = carry_ref[...]
    rank_ref[...] = jnp.where(chosen, prefix + carry[:, :1], -1.0)
    carry = carry + jnp.sum(jnp.where(chosen, 1.0, 0.0), axis=1, keepdims=True)
    carry_ref[...] = carry
    cend_ref[0] = carry


def _router(x2, g, wrt):
    t = x2.shape[0]
    w = ROUTE_WINDOW
    return pl.pallas_call(
        _router_kernel,
        grid=(t // w,),
        in_specs=[
            pl.BlockSpec((w, D_MODEL), lambda i: (i, 0)),
            pl.BlockSpec((1, D_MODEL), lambda i: (0, 0)),
            pl.BlockSpec((N_EXPERTS, D_MODEL), lambda i: (0, 0)),
        ],
        out_specs=[
            pl.BlockSpec((w, D_MODEL), lambda i: (i, 0)),
            pl.BlockSpec((N_EXPERTS, w), lambda i: (0, i)),
            pl.BlockSpec((N_EXPERTS, w), lambda i: (0, i)),
            pl.BlockSpec((1, N_EXPERTS, LANES), lambda i: (i, 0, 0)),
        ],
        out_shape=[jax.ShapeDtypeStruct((t, D_MODEL), BF16),
                   jax.ShapeDtypeStruct((N_EXPERTS, t), F32),
                   jax.ShapeDtypeStruct((N_EXPERTS, t), F32),
                   jax.ShapeDtypeStruct((t // w, N_EXPERTS, LANES), F32)],
        scratch_shapes=[pltpu.VMEM((N_EXPERTS, LANES), F32)],
        compiler_params=_params("arbitrary"),
        name="router",
    )(x2, g, wrt)


def _cell_lookup(table, rows, values):
    return jnp.sum(table[rows] <= values[:, None], axis=1).astype(jnp.int32) - 1


def _count_le(sorted_values, queries):
    return jnp.sum(sorted_values[None, :] <= queries[:, None], axis=1).astype(jnp.int32)


def _flatten_items(nitems, n_max):
    end = jnp.cumsum(nitems)
    start = end - nitems
    total = end[-1]
    k = jnp.clip(jnp.arange(n_max, dtype=jnp.int32), 0, jnp.maximum(total - 1, 0))
    cell = jnp.minimum(_count_le(end, k), nitems.shape[0] - 1)
    valid = jnp.arange(n_max, dtype=jnp.int32) < total
    return cell, k - start[cell], start, end, valid


def _routing_plan(cend, t):
    i32 = jnp.int32
    nw = t // ROUTE_WINDOW
    cend_i = cend[:, :, 0].astype(i32).T
    cstart = jnp.concatenate([jnp.zeros((N_EXPERTS, 1), i32), cend_i[:, :-1]], axis=1)
    counts = cend_i[:, -1]
    ntiles = (counts + EXPERT_TILE - 1) // EXPERT_TILE
    tile_end = jnp.cumsum(ntiles)
    tile_start = tile_end - ntiles
    off = tile_start * EXPERT_TILE

    nt_max = 2 * t // EXPERT_TILE + N_EXPERTS
    ti = jnp.arange(nt_max, dtype=i32)
    tile_e = jnp.minimum(_count_le(tile_end, ti), N_EXPERTS - 1)
    tile_valid = (ti < tile_end[-1]).astype(i32)

    per = EXPERT_TILE // GATHER_TILE
    gj = jnp.arange(nt_max * per, dtype=i32)
    g_e = tile_e[gj // per]
    g_r0 = (gj // per - tile_start[g_e]) * EXPERT_TILE + (gj % per) * GATHER_TILE
    g_tile_valid = tile_valid[gj // per] > 0
    g_has_rows = jnp.logical_and(g_tile_valid, g_r0 < counts[g_e])
    g_last = jnp.minimum(g_r0 + GATHER_TILE, counts[g_e]) - 1
    wlo = jnp.where(g_has_rows, _cell_lookup(cstart, g_e, g_r0), 0)
    whi = jnp.where(g_has_rows, _cell_lookup(cstart, g_e, g_last), 0)
    g_n = jnp.where(g_tile_valid, whi - wlo + 1, 0)
    n_gather = N_EXPERTS * nw + nt_max * per
    cell, pos, start, end, valid = _flatten_items(g_n, n_gather)
    gather = dict(
        tile=cell, win=wlo[cell] + pos, e=g_e[cell], r0=g_r0[cell],
        first=jnp.logical_and(valid, pos == 0).astype(i32),
        valid=valid.astype(i32))

    c_w = jnp.arange(nw * N_EXPERTS, dtype=i32) // N_EXPERTS
    c_e = jnp.arange(nw * N_EXPERTS, dtype=i32) % N_EXPERTS
    lo = cstart[c_e, c_w]
    hi = cend_i[c_e, c_w]
    blo = (off[c_e] + lo) // COMBINE_BLOCK
    bhi = (off[c_e] + hi - 1) // COMBINE_BLOCK
    c_n = jnp.where(hi > lo, bhi - blo + 1, 0)
    n_combine = N_EXPERTS * nw + nt_max * (EXPERT_TILE // COMBINE_BLOCK)
    cell, pos, start, end, valid = _flatten_items(c_n, n_combine)
    k = jnp.clip(jnp.arange(n_combine, dtype=i32), 0, jnp.maximum(end[-1] - 1, 0))
    win = c_w[cell]
    blk = blo[cell] + pos
    combine = dict(
        win=win, blk=blk, e=c_e[cell], r0=blk * COMBINE_BLOCK - off[c_e[cell]],
        first=jnp.logical_and(valid, k == start[win * N_EXPERTS]).astype(i32),
        last=jnp.logical_and(valid, k == end[win * N_EXPERTS + N_EXPERTS - 1] - 1).astype(i32),
        valid=valid.astype(i32))
    return tile_e, tile_valid, gather, combine


def _match(rank_row, r0, rows):
    want = (lax.broadcasted_iota(jnp.int32, (rows, rank_row.shape[1]), 0) + r0).astype(F32)
    return jnp.where(rank_row == want, 1.0, 0.0).astype(BF16)


def _gather_kernel(tile_s, win_s, e_s, r0_s, first_s, valid_s, xn_ref, rank_ref, o_ref):
    k = pl.program_id(0)
    e = e_s[k]

    @pl.when(valid_s[k] > 0)
    def _():
        onehot = _match(rank_ref[pl.ds(e, 1), :], r0_s[k], GATHER_TILE)
        rows = jnp.dot(onehot, xn_ref[...], preferred_element_type=F32).astype(BF16)

        @pl.when(first_s[k] > 0)
        def _():
            o_ref[...] = rows

        @pl.when(first_s[k] == 0)
        def _():
            o_ref[...] += rows


def _gather(plan, xn, rank, n_rows):
    w = ROUTE_WINDOW
    n_items = plan["tile"].shape[0]
    grid_spec = pltpu.PrefetchScalarGridSpec(
        num_scalar_prefetch=6,
        grid=(n_items,),
        in_specs=[
            pl.BlockSpec((w, D_MODEL), lambda k, tile, win, e, r0, first, valid: (win[k], 0)),
            pl.BlockSpec((N_EXPERTS, w), lambda k, tile, win, e, r0, first, valid: (0, win[k])),
        ],
        out_specs=pl.BlockSpec((GATHER_TILE, D_MODEL),
                               lambda k, tile, win, e, r0, first, valid: (tile[k], 0)),
    )
    return pl.pallas_call(
        _gather_kernel,
        grid_spec=grid_spec,
        out_shape=jax.ShapeDtypeStruct((n_rows, D_MODEL), BF16),
        compiler_params=_params("arbitrary"),
        name="moe_gather",
    )(plan["tile"], plan["win"], plan["e"], plan["r0"], plan["first"], plan["valid"], xn, rank)


def _expert_kernel(tile_e_s, tile_valid_s, xs_ref, wg_ref, wu_ref, wd_ref, y_ref, acc_ref):
    i = pl.program_id(0)
    c = pl.program_id(1)
    valid = tile_valid_s[i] > 0

    @pl.when(c == 0)
    def _():
        acc_ref[...] = jnp.zeros_like(acc_ref)

    @pl.when(valid)
    def _():
        for r in range(EXPERT_TILE // ROW_SUB):
            rows = pl.ds(r * ROW_SUB, ROW_SUB)
            acc_ref[rows, :] += _swiglu_part(xs_ref[rows, :], wg_ref[0], wu_ref[0], wd_ref[0])

    @pl.when(c == pl.num_programs(1) - 1)
    def _():
        y_ref[...] = acc_ref[...].astype(BF16)


def _experts(tile_e, tile_valid, xs, wg_bf, wu_bf, wd_bf):
    n_rows = xs.shape[0]
    grid_spec = pltpu.PrefetchScalarGridSpec(
        num_scalar_prefetch=2,
        grid=(n_rows // EXPERT_TILE, D_FF // FF_TILE),
        in_specs=[
            pl.BlockSpec((EXPERT_TILE, D_MODEL), lambda i, c, te, tv: (i * tv[i], 0)),
            pl.BlockSpec((1, D_MODEL, FF_TILE), lambda i, c, te, tv: (te[i], 0, c)),
            pl.BlockSpec((1, D_MODEL, FF_TILE), lambda i, c, te, tv: (te[i], 0, c)),
            pl.BlockSpec((1, FF_TILE, D_MODEL), lambda i, c, te, tv: (te[i], c, 0)),
        ],
        out_specs=pl.BlockSpec((EXPERT_TILE, D_MODEL), lambda i, c, te, tv: (i, 0)),
        scratch_shapes=[pltpu.VMEM((EXPERT_TILE, D_MODEL), F32)],
    )
    return pl.pallas_call(
        _expert_kernel,
        grid_spec=grid_spec,
        out_shape=jax.ShapeDtypeStruct((n_rows, D_MODEL), BF16),
        compiler_params=_params("arbitrary", "arbitrary"),
        name="moe_experts",
    )(tile_e, tile_valid, xs, wg_bf, wu_bf, wd_bf)


def _combine_kernel(win_s, blk_s, e_s, r0_s, first_s, last_s, valid_s,
                    y_ref, rank_ref, gates_ref, x_ref, gf_ref, o_ref, acc_ref):
    k = pl.program_id(0)
    e = e_s[k]
    w = ROUTE_WINDOW

    @pl.when(first_s[k] > 0)
    def _():
        acc_ref[...] = x_ref[...]

    @pl.when(valid_s[k] > 0)
    def _():
        onehot = _match(rank_ref[pl.ds(e, 1), :], r0_s[k], COMBINE_BLOCK)
        rows = lax.dot_general(onehot, y_ref[...], (((0,), (0,)), ((), ())),
                               preferred_element_type=F32)
        diag = (lax.broadcasted_iota(jnp.int32, (w, w), 0)
                == lax.broadcasted_iota(jnp.int32, (w, w), 1))
        gate = jnp.sum(jnp.where(diag, gates_ref[pl.ds(e, 1), :], 0.0), axis=1, keepdims=True)
        acc_ref[...] += gate * rows

    @pl.when(last_s[k] > 0)
    def _():
        o_ref[...] = _rmsnorm_f32(acc_ref[...], gf_ref[...])


def _combine(plan, y, rank, gates, x2, g_final):
    t = x2.shape[0]
    w = ROUTE_WINDOW
    n_items = plan["win"].shape[0]
    grid_spec = pltpu.PrefetchScalarGridSpec(
        num_scalar_prefetch=7,
        grid=(n_items,),
        in_specs=[
            pl.BlockSpec((COMBINE_BLOCK, D_MODEL), lambda k, win, blk, *_: (blk[k], 0)),
            pl.BlockSpec((N_EXPERTS, w), lambda k, win, blk, *_: (0, win[k])),
            pl.BlockSpec((N_EXPERTS, w), lambda k, win, blk, *_: (0, win[k])),
            pl.BlockSpec((w, D_MODEL), lambda k, win, blk, *_: (win[k], 0)),
            pl.BlockSpec((1, D_MODEL), lambda k, win, blk, *_: (0, 0)),
        ],
        out_specs=pl.BlockSpec((w, D_MODEL), lambda k, win, blk, *_: (win[k], 0)),
        scratch_shapes=[pltpu.VMEM((w, D_MODEL), F32)],
    )
    return pl.pallas_call(
        _combine_kernel,
        grid_spec=grid_spec,
        out_shape=jax.ShapeDtypeStruct((t, D_MODEL), F32),
        compiler_params=_params("arbitrary"),
        name="moe_combine",
    )(plan["win"], plan["blk"], plan["e"], plan["r0"], plan["first"], plan["last"],
      plan["valid"], y, rank, gates, x2, g_final)


def _rotary_tables(seq):
    inv_freq = ROPE_BASE ** (-jnp.arange(0, RET_QK_DIM, 2, dtype=F32) / RET_QK_DIM)
    ang = jnp.arange(seq, dtype=F32)[:, None] * inv_freq[None, :]
    return jnp.cos(ang), jnp.sin(ang)


def _decay_tables():
    c = RET_CHUNK
    log_gamma = jnp.log(1.0 - 2.0 ** (-5.0 - jnp.arange(RET_HEADS, dtype=F32)))
    idx = jnp.arange(c, dtype=F32)
    diff = idx[:, None] - idx[None, :]
    causal = diff >= 0
    intra = jnp.where(causal[None],
                      jnp.exp(jnp.where(causal, diff, 0.0)[None] * log_gamma[:, None, None]), 0.0)
    qdec = jnp.exp((idx[None, :] + 1.0) * log_gamma[:, None])[..., None]
    kdec = jnp.exp((c - 1.0 - idx[None, :]) * log_gamma[:, None])[..., None]
    cdec = jnp.exp(c * log_gamma)[:, None, None]
    return intra, qdec, kdec, cdec


def kernel(x, norm_mix0, ret_w_in, ret_w_out, norm_ffn0, ffn_w_gate, ffn_w_up, ffn_w_down,
           norm_mix1, conv_w_in, conv_w, conv_w_out, norm_ffn1, moe_router, moe_w_gate,
           moe_w_up, moe_w_down, norm_final):
    b, seq, d = x.shape
    t = b * seq
    row = lambda g: g.reshape(1, d).astype(F32)
    bf = lambda w: w.astype(BF16)

    cos, sin = _rotary_tables(seq)
    intra, qdec, kdec, cdec = _decay_tables()

    proj = _ret_in(x.reshape(t, d), row(norm_mix0), bf(ret_w_in), cos, sin, seq)
    h = _ret_core(proj, x, bf(ret_w_out), intra, qdec, kdec, cdec)
    h = _ffn(h.reshape(t, d), row(norm_ffn0), bf(ffn_w_gate), bf(ffn_w_up), bf(ffn_w_down))
    h = _conv_layer(h.reshape(b, seq, d), row(norm_mix1), bf(conv_w_in),
                    conv_w.astype(F32), bf(conv_w_out))
    h2 = h.reshape(t, d)
    xn, rank, gates, cend = _router(h2, row(norm_ffn1), moe_router.astype(F32).T)
    tile_e, tile_valid, gather_plan, combine_plan = _routing_plan(cend, t)
    n_rows = (2 * t // EXPERT_TILE + N_EXPERTS) * EXPERT_TILE
    xs = _gather(gather_plan, xn, rank, n_rows)
    y = _experts(tile_e, tile_valid, xs, bf(moe_w_gate), bf(moe_w_up), bf(moe_w_down))
    out = _combine(combine_plan, y, rank, gates, h2, row(norm_final))
    return out.reshape(b, seq, d)
```

```python
import functools

import jax
import jax.numpy as jnp
from jax import lax
from jax.experimental import pallas as pl
from jax.experimental.pallas import tpu as pltpu
from jax.experimental.pallas import tpu_sc as plsc

D_MODEL = 1024
RET_HEADS = 4
RET_QK_DIM = D_MODEL // RET_HEADS
RET_V_DIM = 2 * RET_QK_DIM
RET_QK_TOTAL = RET_HEADS * RET_QK_DIM
RET_V_TOTAL = RET_HEADS * RET_V_DIM
RET_IN_COLS = 2 * RET_QK_TOTAL + 2 * RET_V_TOTAL
ROPE_BASE = 10000.0
CONV_WIDTH = 3
D_FF = 7 * D_MODEL // 2
N_EXPERTS = 8
TOP_K = 2
NORM_EPS = 1e-6

RET_CHUNK = 256
HALF = RET_QK_DIM // 2

VMEM_LIMIT_BYTES = 56 * 1024 * 1024
SC_CORES = 2
SC_SUBCORES = 16
SC_WORKERS = SC_CORES * SC_SUBCORES
SC_ROWS = 64

TOKEN_TILE = 1024
RET_SEQ_TILE = 512
CONV_SEQ_TILE = 512
FF_TILE = 1792
ROW_SUB = 512
IN_TOKEN_TILE = 512
IN_COL_TILE = 1024
ROUTE_WINDOW = 512
EXPERT_TILE = 1024
LANES = 128
PACKED = D_MODEL // 2

BF16 = jnp.bfloat16
F32 = jnp.float32
I32 = jnp.int32


def _params(*semantics):
    return pltpu.CompilerParams(dimension_semantics=semantics,
                                vmem_limit_bytes=VMEM_LIMIT_BYTES)


def _rmsnorm_f32(x, g):
    return x * lax.rsqrt(jnp.mean(x * x, axis=-1, keepdims=True) + NORM_EPS) * g


def _silu(x):
    return x * (1.0 / (1.0 + jnp.exp(-x)))


def _pack_bf16_pair(x):
    half = x.shape[1] // 2
    hi = lax.bitcast_convert_type(x[:, :half].astype(BF16).astype(F32), I32)
    lo = lax.bitcast_convert_type(x[:, half:].astype(BF16).astype(F32), I32)
    return hi | lax.shift_right_logical(lo, 16)


def _unpack_bf16_pair(w):
    hi = lax.bitcast_convert_type(w & jnp.int32(-65536), F32)
    lo = lax.bitcast_convert_type(lax.shift_left(w, 16), F32)
    return jnp.concatenate([hi, lo], axis=1)


def _ret_in_kernel(x_ref, g_ref, w_ref, cos_ref, sin_ref, o_ref):
    xn = _rmsnorm_f32(x_ref[...], g_ref[...]).astype(BF16)
    cos = cos_ref[...]
    sin = sin_ref[...]
    for n in range(RET_IN_COLS // IN_COL_TILE):
        cols = slice(n * IN_COL_TILE, (n + 1) * IN_COL_TILE)
        acc = jnp.dot(xn, w_ref[:, cols], preferred_element_type=F32)
        if n * IN_COL_TILE >= 2 * RET_QK_TOTAL:
            o_ref[:, cols] = acc.astype(BF16)
            continue
        scale = 1.0 if n * IN_COL_TILE < RET_QK_TOTAL else RET_QK_DIM ** -0.5
        for h in range(IN_COL_TILE // RET_QK_DIM):
            lo = h * RET_QK_DIM
            t1 = acc[:, lo:lo + HALF]
            t2 = acc[:, lo + HALF:lo + RET_QK_DIM]
            base = n * IN_COL_TILE + lo
            o_ref[:, base:base + HALF] = ((t1 * cos - t2 * sin) * scale).astype(BF16)
            o_ref[:, base + HALF:base + RET_QK_DIM] = ((t1 * sin + t2 * cos) * scale).astype(BF16)


def _ret_in(x2, g, w_bf, cos, sin, seq):
    t = x2.shape[0]
    tm = IN_TOKEN_TILE
    seq_tiles = seq // tm
    return pl.pallas_call(
        _ret_in_kernel,
        grid=(t // tm,),
        in_specs=[
            pl.BlockSpec((tm, D_MODEL), lambda i: (i, 0)),
            pl.BlockSpec((1, D_MODEL), lambda i: (0, 0)),
            pl.BlockSpec((D_MODEL, RET_IN_COLS), lambda i: (0, 0), pipeline_mode=pl.Buffered(1)),
            pl.BlockSpec((tm, HALF), lambda i: (i % seq_tiles, 0)),
            pl.BlockSpec((tm, HALF), lambda i: (i % seq_tiles, 0)),
        ],
        out_specs=pl.BlockSpec((tm, RET_IN_COLS), lambda i: (i, 0)),
        out_shape=jax.ShapeDtypeStruct((t, RET_IN_COLS), BF16),
        compiler_params=_params("parallel"),
        name="ret_in_proj",
    )(x2, g, w_bf, cos, sin)


def _ret_core_kernel(q_ref, k_ref, v_ref, gate_ref, x_ref, wout_ref,
                     intra_ref, qdec_ref, kdec_ref, cdec_ref, o_ref, state_ref):
    @pl.when(pl.program_id(1) == 0)
    def _():
        state_ref[...] = jnp.zeros_like(state_ref)

    gated = []
    for c in range(RET_SEQ_TILE // RET_CHUNK):
        rows = pl.ds(c * RET_CHUNK, RET_CHUNK)
        heads = []
        for h in range(RET_HEADS):
            qk_cols = slice(h * RET_QK_DIM, (h + 1) * RET_QK_DIM)
            v_cols = slice(h * RET_V_DIM, (h + 1) * RET_V_DIM)
            q = q_ref[rows, qk_cols]
            k = k_ref[rows, qk_cols]
            v = v_ref[rows, v_cols]
            state = state_ref[h]
            scores = lax.dot_general(q, k, (((1,), (1,)), ((), ())),
                                     preferred_element_type=F32) * intra_ref[h]
            inner = jnp.dot(scores.astype(BF16), v, preferred_element_type=F32)
            qd = (q.astype(F32) * qdec_ref[h]).astype(BF16)
            cross = jnp.dot(qd, state.astype(BF16), preferred_element_type=F32)
            kd = (k.astype(F32) * kdec_ref[h]).astype(BF16)
            state_ref[h] = state * cdec_ref[h] + lax.dot_general(
                kd, v, (((0,), (0,)), ((), ())), preferred_element_type=F32)
            o = inner + cross
            o = o * lax.rsqrt(jnp.mean(o * o, axis=-1, keepdims=True) + NORM_EPS)
            gate = gate_ref[rows, v_cols].astype(F32)
            heads.append((_silu(gate) * o).astype(BF16))
        gated.append(jnp.concatenate(heads, axis=1))
    og = jnp.concatenate(gated, axis=0)
    o_ref[0] = x_ref[0] + jnp.dot(og, wout_ref[...], preferred_element_type=F32)


def _ret_core(proj, x, wout_bf, intra, qdec, kdec, cdec):
    b, seq, _ = x.shape
    ts = RET_SEQ_TILE
    st = seq // ts
    whole = lambda a: pl.BlockSpec(a.shape, lambda bi, s: (0,) * a.ndim)
    return pl.pallas_call(
        _ret_core_kernel,
        grid=(b, st),
        in_specs=[
            pl.BlockSpec((ts, RET_QK_TOTAL), lambda bi, s: (bi * st + s, 0)),
            pl.BlockSpec((ts, RET_QK_TOTAL), lambda bi, s: (bi * st + s, 1)),
            pl.BlockSpec((ts, RET_V_TOTAL), lambda bi, s: (bi * st + s, 1)),
            pl.BlockSpec((ts, RET_V_TOTAL), lambda bi, s: (bi * st + s, 2)),
            pl.BlockSpec((1, ts, D_MODEL), lambda bi, s: (bi, s, 0)),
            pl.BlockSpec((RET_V_TOTAL, D_MODEL), lambda bi, s: (0, 0), pipeline_mode=pl.Buffered(1)),
            whole(intra), whole(qdec), whole(kdec), whole(cdec),
        ],
        out_specs=pl.BlockSpec((1, ts, D_MODEL), lambda bi, s: (bi, s, 0)),
        out_shape=jax.ShapeDtypeStruct(x.shape, F32),
        scratch_shapes=[pltpu.VMEM((RET_HEADS, RET_QK_DIM, RET_V_DIM), F32)],
        compiler_params=_params("parallel", "arbitrary"),
        name="ret_core",
    )(proj, proj, proj, proj, x, wout_bf, intra, qdec, kdec, cdec)


def _swiglu_part(x_bf, wg, wu, wd):
    hg = jnp.dot(x_bf, wg, preferred_element_type=F32)
    hu = jnp.dot(x_bf, wu, preferred_element_type=F32)
    hid = (_silu(hg) * hu).astype(BF16)
    return jnp.dot(hid, wd, preferred_element_type=F32)


def _ffn_kernel(x_ref, g_ref, wg_ref, wu_ref, wd_ref, o_ref):
    @pl.when(pl.program_id(1) == 0)
    def _():
        o_ref[...] = x_ref[...]

    for r in range(TOKEN_TILE // ROW_SUB):
        rows = pl.ds(r * ROW_SUB, ROW_SUB)
        xn = _rmsnorm_f32(x_ref[rows, :], g_ref[...]).astype(BF16)
        o_ref[rows, :] += _swiglu_part(xn, wg_ref[...], wu_ref[...], wd_ref[...])


def _ffn(x2, g, wg_bf, wu_bf, wd_bf):
    t = x2.shape[0]
    return pl.pallas_call(
        _ffn_kernel,
        grid=(t // TOKEN_TILE, D_FF // FF_TILE),
        in_specs=[
            pl.BlockSpec((TOKEN_TILE, D_MODEL), lambda i, c: (i, 0)),
            pl.BlockSpec((1, D_MODEL), lambda i, c: (0, 0)),
            pl.BlockSpec((D_MODEL, FF_TILE), lambda i, c: (0, c)),
            pl.BlockSpec((D_MODEL, FF_TILE), lambda i, c: (0, c)),
            pl.BlockSpec((FF_TILE, D_MODEL), lambda i, c: (c, 0)),
        ],
        out_specs=pl.BlockSpec((TOKEN_TILE, D_MODEL), lambda i, c: (i, 0)),
        out_shape=jax.ShapeDtypeStruct(x2.shape, F32),
        compiler_params=_params("parallel", "arbitrary"),
        name="ffn_swiglu",
    )(x2, g, wg_bf, wu_bf, wd_bf)


def _conv_kernel(x_ref, g_ref, win_ref, wc_ref, wout_ref, o_ref, carry_ref):
    s = pl.program_id(1)
    ts = CONV_SEQ_TILE

    @pl.when(s == 0)
    def _():
        carry_ref[...] = jnp.zeros_like(carry_ref)

    x = x_ref[0]
    xn = _rmsnorm_f32(x, g_ref[...]).astype(BF16)
    proj = jnp.dot(xn, win_ref[...], preferred_element_type=F32)
    gate_b = proj[:, :D_MODEL]
    gate_c = proj[:, D_MODEL:2 * D_MODEL]
    hid = proj[:, 2 * D_MODEL:]
    u = gate_c * hid

    row = lax.broadcasted_iota(jnp.int32, (ts, D_MODEL), 0)
    prev1 = carry_ref[7:8, :]
    prev2 = carry_ref[6:7, :]
    u1 = jnp.where(row == 0, prev1, pltpu.roll(u, 1, axis=0))
    u2 = jnp.where(row == 0, prev2, jnp.where(row == 1, prev1, pltpu.roll(u, 2, axis=0)))
    carry_ref[...] = u[ts - 8:, :]

    wc = wc_ref[...]
    y = wc[2:3, :] * u + wc[1:2, :] * u1 + wc[0:1, :] * u2
    y = (gate_b * y).astype(BF16)
    o_ref[0] = x + jnp.dot(y, wout_ref[...], preferred_element_type=F32)


def _conv_layer(x, g, win_bf, wc, wout_bf):
    b, seq, _ = x.shape
    ts = CONV_SEQ_TILE
    return pl.pallas_call(
        _conv_kernel,
        grid=(b, seq // ts),
        in_specs=[
            pl.BlockSpec((1, ts, D_MODEL), lambda bi, s: (bi, s, 0)),
            pl.BlockSpec((1, D_MODEL), lambda bi, s: (0, 0)),
            pl.BlockSpec((D_MODEL, 3 * D_MODEL), lambda bi, s: (0, 0)),
            pl.BlockSpec((CONV_WIDTH, D_MODEL), lambda bi, s: (0, 0)),
            pl.BlockSpec((D_MODEL, D_MODEL), lambda bi, s: (0, 0)),
        ],
        out_specs=pl.BlockSpec((1, ts, D_MODEL), lambda bi, s: (bi, s, 0)),
        out_shape=jax.ShapeDtypeStruct(x.shape, F32),
        scratch_shapes=[pltpu.VMEM((8, D_MODEL), F32)],
        compiler_params=_params("parallel", "arbitrary"),
        name="short_conv",
    )(x, g, win_bf, wc, wout_bf)


def _router_kernel(x_ref, g_ref, wrt_ref, xn_ref, sel_e_ref, sel_r_ref, sel_g_ref, cend_ref,
                   carry_ref):
    w = ROUTE_WINDOW

    @pl.when(pl.program_id(0) == 0)
    def _():
        carry_ref[...] = jnp.zeros_like(carry_ref)

    xn = _rmsnorm_f32(x_ref[...], g_ref[...])
    xn_ref[...] = _pack_bf16_pair(xn)
    logits = lax.dot_general(wrt_ref[...], xn, (((1,), (1,)), ((), ())),
                             preferred_element_type=F32,
                             precision=lax.Precision.HIGHEST)
    sub = lax.broadcasted_iota(jnp.int32, logits.shape, 0).astype(F32)
    none = float(N_EXPERTS)
    m1 = jnp.max(logits, axis=0, keepdims=True)
    i1 = jnp.min(jnp.where(logits == m1, sub, none), axis=0, keepdims=True)
    rest = jnp.where(sub == i1, -jnp.inf, logits)
    m2 = jnp.max(rest, axis=0, keepdims=True)
    i2 = jnp.min(jnp.where(rest == m2, sub, none), axis=0, keepdims=True)
    e2 = jnp.exp(m2 - m1)
    denom = 1.0 + e2

    chosen = jnp.logical_or(sub == i1, sub == i2)
    before = (lax.broadcasted_iota(jnp.int32, (w, w), 0)
              < lax.broadcasted_iota(jnp.int32, (w, w), 1))
    prefix = jnp.dot(jnp.where(chosen, 1.0, 0.0).astype(BF16),
                     jnp.where(before, 1.0, 0.0).astype(BF16),
                     preferred_element_type=F32)
    carry = carry_ref[...]
    rank = prefix + carry[:, :1]
    r1 = jnp.sum(jnp.where(sub == i1, rank, 0.0), axis=0, keepdims=True)
    r2 = jnp.sum(jnp.where(sub == i2, rank, 0.0), axis=0, keepdims=True)
    sel_e_ref[0:1, :] = i1.astype(I32)
    sel_e_ref[1:2, :] = i2.astype(I32)
    sel_r_ref[0:1, :] = r1.astype(I32)
    sel_r_ref[1:2, :] = r2.astype(I32)
    sel_g_ref[0:1, :] = 1.0 / denom
    sel_g_ref[1:2, :] = e2 / denom
    carry = carry + jnp.sum(jnp.where(chosen, 1.0, 0.0), axis=1, keepdims=True)
    carry_ref[...] = carry
    cend_ref[0] = carry


def _router(x2, g, wrt):
    t = x2.shape[0]
    w = ROUTE_WINDOW
    per_token = lambda dtype: jax.ShapeDtypeStruct((TOP_K, t), dtype)
    return pl.pallas_call(
        _router_kernel,
        grid=(t // w,),
        in_specs=[
            pl.BlockSpec((w, D_MODEL), lambda i: (i, 0)),
            pl.BlockSpec((1, D_MODEL), lambda i: (0, 0)),
            pl.BlockSpec((N_EXPERTS, D_MODEL), lambda i: (0, 0)),
        ],
        out_specs=[
            pl.BlockSpec((w, PACKED), lambda i: (i, 0)),
            pl.BlockSpec((TOP_K, w), lambda i: (0, i)),
            pl.BlockSpec((TOP_K, w), lambda i: (0, i)),
            pl.BlockSpec((TOP_K, w), lambda i: (0, i)),
            pl.BlockSpec((1, N_EXPERTS, LANES), lambda i: (i, 0, 0)),
        ],
        out_shape=[jax.ShapeDtypeStruct((t, PACKED), I32),
                   per_token(I32), per_token(I32), per_token(F32),
                   jax.ShapeDtypeStruct((t // w, N_EXPERTS, LANES), F32)],
        scratch_shapes=[pltpu.VMEM((N_EXPERTS, LANES), F32)],
        compiler_params=_params("arbitrary"),
        name="router",
    )(x2, g, wrt)


def _routing_plan(cend, sel_e, sel_r, t):
    counts = cend[-1, :, 0].astype(I32)
    ntiles = (counts + EXPERT_TILE - 1) // EXPERT_TILE
    tile_end = jnp.cumsum(ntiles)
    tile_start = tile_end - ntiles
    off = tile_start * EXPERT_TILE

    nt_max = TOP_K * t // EXPERT_TILE + N_EXPERTS
    ti = jnp.arange(nt_max, dtype=I32)
    tile_e = jnp.minimum(jnp.sum(tile_end[None, :] <= ti[:, None], axis=1).astype(I32),
                         N_EXPERTS - 1)
    tile_rows = jnp.clip(counts[tile_e] - (ti - tile_start[tile_e]) * EXPERT_TILE, 0, EXPERT_TILE)
    tile_rows = jnp.where(ti < tile_end[-1], tile_rows, 0)

    first_row = jnp.zeros_like(sel_e)
    for e in range(N_EXPERTS):
        first_row = first_row + jnp.where(sel_e == e, off[e], 0)
    dest = first_row + sel_r
    return tile_e, tile_rows, dest


def _sc_mesh():
    return plsc.VectorSubcoreMesh(core_axis_name="c", subcore_axis_name="s",
                                  num_cores=SC_CORES, num_subcores=SC_SUBCORES)


def _dispatch_rows(xn, dest, n_rows):
    t, d = xn.shape
    per_worker = t // SC_WORKERS
    chunks = per_worker // SC_ROWS
    dest4 = dest.reshape(TOP_K, SC_WORKERS, chunks, SC_ROWS)

    @functools.partial(
        pl.kernel, mesh=_sc_mesh(),
        out_type=jax.ShapeDtypeStruct((n_rows, d), xn.dtype),
        scratch_types=[pltpu.VMEM((TOP_K, chunks, SC_ROWS), I32),
                       pltpu.VMEM((SC_ROWS, d), xn.dtype)],
        name="moe_dispatch",
    )
    def run(x_hbm, dest_hbm, o_hbm, dest_v, rows_v):
        worker = lax.axis_index("s") * SC_CORES + lax.axis_index("c")
        for k in range(TOP_K):
            pltpu.sync_copy(dest_hbm.at[k, worker], dest_v.at[k])

        @pl.loop(0, chunks)
        def _(j):
            pltpu.sync_copy(x_hbm.at[pl.ds(worker * per_worker + j * SC_ROWS, SC_ROWS)], rows_v)
            for k in range(TOP_K):
                pltpu.sync_copy(rows_v, o_hbm.at[dest_v.at[k, j]])

    return run(xn, dest4)


def _collect_rows(y, dest):
    t = dest.shape[1]
    d = y.shape[1]
    per_worker = t // SC_WORKERS
    chunks = per_worker // SC_ROWS
    dest4 = dest.reshape(TOP_K, SC_WORKERS, chunks, SC_ROWS)

    @functools.partial(
        pl.kernel, mesh=_sc_mesh(),
        out_type=jax.ShapeDtypeStruct((TOP_K, t, d), y.dtype),
        scratch_types=[pltpu.VMEM((TOP_K, chunks, SC_ROWS), I32),
                       pltpu.VMEM((SC_ROWS, d), y.dtype)],
        name="moe_collect",
    )
    def run(y_hbm, dest_hbm, o_hbm, dest_v, rows_v):
        worker = lax.axis_index("s") * SC_CORES + lax.axis_index("c")
        for k in range(TOP_K):
            pltpu.sync_copy(dest_hbm.at[k, worker], dest_v.at[k])

        @pl.loop(0, chunks)
        def _(j):
            for k in range(TOP_K):
                pltpu.sync_copy(y_hbm.at[dest_v.at[k, j]], rows_v)
                pltpu.sync_copy(rows_v, o_hbm.at[k, pl.ds(worker * per_worker + j * SC_ROWS, SC_ROWS)])

    return run(y, dest4)


def _expert_kernel(tile_e_s, tile_rows_s, xs_ref, wg_ref, wu_ref, wd_ref, y_ref, acc_ref):
    i = pl.program_id(0)
    c = pl.program_id(1)
    n_rows = tile_rows_s[i]

    @pl.when(c == 0)
    def _():
        acc_ref[...] = jnp.zeros_like(acc_ref)

    @pl.when(n_rows > 0)
    def _():
        for r in range(EXPERT_TILE // ROW_SUB):
            rows = pl.ds(r * ROW_SUB, ROW_SUB)
            real = lax.broadcasted_iota(jnp.int32, (ROW_SUB, PACKED), 0) < n_rows - r * ROW_SUB
            xs = _unpack_bf16_pair(jnp.where(real, xs_ref[rows, :], 0)).astype(BF16)
            acc_ref[rows, :] += _swiglu_part(xs, wg_ref[0], wu_ref[0], wd_ref[0])

    @pl.when(c == pl.num_programs(1) - 1)
    def _():
        y_ref[...] = _pack_bf16_pair(acc_ref[...])


def _experts(tile_e, tile_rows, xs, wg_bf, wu_bf, wd_bf):
    n_rows = xs.shape[0]
    grid_spec = pltpu.PrefetchScalarGridSpec(
        num_scalar_prefetch=2,
        grid=(n_rows // EXPERT_TILE, D_FF // FF_TILE),
        in_specs=[
            pl.BlockSpec((EXPERT_TILE, PACKED), lambda i, c, te, tr: (i, 0)),
            pl.BlockSpec((1, D_MODEL, FF_TILE), lambda i, c, te, tr: (te[i], 0, c)),
            pl.BlockSpec((1, D_MODEL, FF_TILE), lambda i, c, te, tr: (te[i], 0, c)),
            pl.BlockSpec((1, FF_TILE, D_MODEL), lambda i, c, te, tr: (te[i], c, 0)),
        ],
        out_specs=pl.BlockSpec((EXPERT_TILE, PACKED), lambda i, c, te, tr: (i, 0)),
        scratch_shapes=[pltpu.VMEM((EXPERT_TILE, D_MODEL), F32)],
    )
    return pl.pallas_call(
        _expert_kernel,
        grid_spec=grid_spec,
        out_shape=jax.ShapeDtypeStruct((n_rows, PACKED), I32),
        compiler_params=_params("arbitrary", "arbitrary"),
        name="moe_experts",
    )(tile_e, tile_rows, xs, wg_bf, wu_bf, wd_bf)


def _final_kernel(x_ref, y_ref, sel_g_ref, gf_ref, o_ref):
    w = ROUTE_WINDOW
    diag = (lax.broadcasted_iota(jnp.int32, (w, w), 0)
            == lax.broadcasted_iota(jnp.int32, (w, w), 1))
    acc = x_ref[...]
    for k in range(TOP_K):
        gate = jnp.sum(jnp.where(diag, sel_g_ref[k:k + 1, :], 0.0), axis=1, keepdims=True)
        acc = acc + gate * _unpack_bf16_pair(y_ref[k])
    o_ref[...] = _rmsnorm_f32(acc, gf_ref[...])


def _final(x2, y_tok, sel_g, g_final):
    t = x2.shape[0]
    w = ROUTE_WINDOW
    return pl.pallas_call(
        _final_kernel,
        grid=(t // w,),
        in_specs=[
            pl.BlockSpec((w, D_MODEL), lambda i: (i, 0)),
            pl.BlockSpec((TOP_K, w, PACKED), lambda i: (0, i, 0)),
            pl.BlockSpec((TOP_K, w), lambda i: (0, i)),
            pl.BlockSpec((1, D_MODEL), lambda i: (0, 0)),
        ],
        out_specs=pl.BlockSpec((w, D_MODEL), lambda i: (i, 0)),
        out_shape=jax.ShapeDtypeStruct((t, D_MODEL), F32),
        compiler_params=_params("parallel"),
        name="moe_final",
    )(x2, y_tok, sel_g, g_final)


def _rotary_tables(seq):
    inv_freq = ROPE_BASE ** (-jnp.arange(0, RET_QK_DIM, 2, dtype=F32) / RET_QK_DIM)
    ang = jnp.arange(seq, dtype=F32)[:, None] * inv_freq[None, :]
    return jnp.cos(ang), jnp.sin(ang)


def _decay_tables():
    c = RET_CHUNK
    log_gamma = jnp.log(1.0 - 2.0 ** (-5.0 - jnp.arange(RET_HEADS, dtype=F32)))
    idx = jnp.arange(c, dtype=F32)
    diff = idx[:, None] - idx[None, :]
    causal = diff >= 0
    intra = jnp.where(causal[None],
                      jnp.exp(jnp.where(causal, diff, 0.0)[None] * log_gamma[:, None, None]), 0.0)
    qdec = jnp.exp((idx[None, :] + 1.0) * log_gamma[:, None])[..., None]
    kdec = jnp.exp((c - 1.0 - idx[None, :]) * log_gamma[:, None])[..., None]
    cdec = jnp.exp(c * log_gamma)[:, None, None]
    return intra, qdec, kdec, cdec


def kernel(x, norm_mix0, ret_w_in, ret_w_out, norm_ffn0, ffn_w_gate, ffn_w_up, ffn_w_down,
           norm_mix1, conv_w_in, conv_w, conv_w_out, norm_ffn1, moe_router, moe_w_gate,
           moe_w_up, moe_w_down, norm_final):
    b, seq, d = x.shape
    t = b * seq
    row = lambda g: g.reshape(1, d).astype(F32)
    bf = lambda w: w.astype(BF16)

    cos, sin = _rotary_tables(seq)
    intra, qdec, kdec, cdec = _decay_tables()

    proj = _ret_in(x.reshape(t, d), row(norm_mix0), bf(ret_w_in), cos, sin, seq)
    h = _ret_core(proj, x, bf(ret_w_out), intra, qdec, kdec, cdec)
    h = _ffn(h.reshape(t, d), row(norm_ffn0), bf(ffn_w_gate), bf(ffn_w_up), bf(ffn_w_down))
    h = _conv_layer(h.reshape(b, seq, d), row(norm_mix1), bf(conv_w_in),
                    conv_w.astype(F32), bf(conv_w_out))
    h2 = h.reshape(t, d)
    xn, sel_e, sel_r, sel_g, cend = _router(h2, row(norm_ffn1), moe_router.astype(F32).T)
    tile_e, tile_rows, dest = _routing_plan(cend, sel_e, sel_r, t)
    n_rows = (TOP_K * t // EXPERT_TILE + N_EXPERTS) * EXPERT_TILE
    xs = _dispatch_rows(xn, dest, n_rows)
    y = _experts(tile_e, tile_rows, xs, bf(moe_w_gate), bf(moe_w_up), bf(moe_w_down))
    y_tok = _collect_rows(y, dest)
    out = _final(h2, y_tok, sel_g, row(norm_final))
    return out.reshape(b, seq, d)
```

```python
import functools

import jax
import jax.numpy as jnp
from jax import lax
from jax.experimental import pallas as pl
from jax.experimental.pallas import tpu as pltpu
from jax.experimental.pallas import tpu_sc as plsc

D_MODEL = 1024
RET_HEADS = 4
RET_QK_DIM = D_MODEL // RET_HEADS
RET_V_DIM = 2 * RET_QK_DIM
RET_QK_TOTAL = RET_HEADS * RET_QK_DIM
RET_V_TOTAL = RET_HEADS * RET_V_DIM
RET_IN_COLS = 2 * RET_QK_TOTAL + 2 * RET_V_TOTAL
ROPE_BASE = 10000.0
CONV_WIDTH = 3
D_FF = 7 * D_MODEL // 2
N_EXPERTS = 8
TOP_K = 2
NORM_EPS = 1e-6

RET_CHUNK = 256
HALF = RET_QK_DIM // 2

VMEM_LIMIT_BYTES = 56 * 1024 * 1024
SC_CORES = 2
SC_SUBCORES = 16
SC_WORKERS = SC_CORES * SC_SUBCORES
SC_ROWS = 64

TOKEN_TILE = 1024
RET_SEQ_TILE = 512
CONV_SEQ_TILE = 512
FF_TILE = 1792
ROW_SUB = 512
IN_TOKEN_TILE = 512
IN_COL_TILE = 1024
ROUTE_WINDOW = 512
EXPERT_TILE = 1024
LANES = 128
PACKED = D_MODEL // 2

BF16 = jnp.bfloat16
F32 = jnp.float32
I32 = jnp.int32


def _params(*semantics):
    return pltpu.CompilerParams(dimension_semantics=semantics,
                                vmem_limit_bytes=VMEM_LIMIT_BYTES)


def _rmsnorm_f32(x, g):
    return x * lax.rsqrt(jnp.mean(x * x, axis=-1, keepdims=True) + NORM_EPS) * g


def _silu(x):
    return x * (1.0 / (1.0 + jnp.exp(-x)))


def _pack_bf16_pair(x):
    half = x.shape[1] // 2
    hi = lax.bitcast_convert_type(x[:, :half].astype(BF16).astype(F32), I32)
    lo = lax.bitcast_convert_type(x[:, half:].astype(BF16).astype(F32), I32)
    return hi | lax.shift_right_logical(lo, 16)


def _unpack_bf16_pair(w):
    hi = lax.bitcast_convert_type(w & jnp.int32(-65536), F32)
    lo = lax.bitcast_convert_type(lax.shift_left(w, 16), F32)
    return jnp.concatenate([hi, lo], axis=1)


def _ret_in_kernel(x_ref, g_ref, w_ref, cos_ref, sin_ref, o_ref):
    xn = _rmsnorm_f32(x_ref[...], g_ref[...]).astype(BF16)
    cos = cos_ref[...]
    sin = sin_ref[...]
    for n in range(RET_IN_COLS // IN_COL_TILE):
        cols = slice(n * IN_COL_TILE, (n + 1) * IN_COL_TILE)
        acc = jnp.dot(xn, w_ref[:, cols], preferred_element_type=F32)
        if n * IN_COL_TILE >= 2 * RET_QK_TOTAL:
            o_ref[:, cols] = acc.astype(BF16)
            continue
        scale = 1.0 if n * IN_COL_TILE < RET_QK_TOTAL else RET_QK_DIM ** -0.5
        for h in range(IN_COL_TILE // RET_QK_DIM):
            lo = h * RET_QK_DIM
            t1 = acc[:, lo:lo + HALF]
            t2 = acc[:, lo + HALF:lo + RET_QK_DIM]
            base = n * IN_COL_TILE + lo
            o_ref[:, base:base + HALF] = ((t1 * cos - t2 * sin) * scale).astype(BF16)
            o_ref[:, base + HALF:base + RET_QK_DIM] = ((t1 * sin + t2 * cos) * scale).astype(BF16)


def _ret_in(x2, g, w_bf, cos, sin, seq):
    t = x2.shape[0]
    tm = IN_TOKEN_TILE
    seq_tiles = seq // tm
    return pl.pallas_call(
        _ret_in_kernel,
        grid=(t // tm,),
        in_specs=[
            pl.BlockSpec((tm, D_MODEL), lambda i: (i, 0)),
            pl.BlockSpec((1, D_MODEL), lambda i: (0, 0)),
            pl.BlockSpec((D_MODEL, RET_IN_COLS), lambda i: (0, 0), pipeline_mode=pl.Buffered(1)),
            pl.BlockSpec((tm, HALF), lambda i: (i % seq_tiles, 0)),
            pl.BlockSpec((tm, HALF), lambda i: (i % seq_tiles, 0)),
        ],
        out_specs=pl.BlockSpec((tm, RET_IN_COLS), lambda i: (i, 0)),
        out_shape=jax.ShapeDtypeStruct((t, RET_IN_COLS), BF16),
        compiler_params=_params("parallel"),
        name="ret_in_proj",
    )(x2, g, w_bf, cos, sin)


def _ret_core_kernel(q_ref, k_ref, v_ref, gate_ref, x_ref, wout_ref,
                     intra_ref, qdec_ref, kdec_ref, cdec_ref, o_ref, state_ref):
    @pl.when(pl.program_id(1) == 0)
    def _():
        state_ref[...] = jnp.zeros_like(state_ref)

    gated = []
    for c in range(RET_SEQ_TILE // RET_CHUNK):
        rows = pl.ds(c * RET_CHUNK, RET_CHUNK)
        heads = []
        for h in range(RET_HEADS):
            qk_cols = slice(h * RET_QK_DIM, (h + 1) * RET_QK_DIM)
            v_cols = slice(h * RET_V_DIM, (h + 1) * RET_V_DIM)
            q = q_ref[rows, qk_cols]
            k = k_ref[rows, qk_cols]
            v = v_ref[rows, v_cols]
            state = state_ref[h]
            scores = lax.dot_general(q, k, (((1,), (1,)), ((), ())),
                                     preferred_element_type=F32) * intra_ref[h]
            inner = jnp.dot(scores.astype(BF16), v, preferred_element_type=F32)
            qd = (q.astype(F32) * qdec_ref[h]).astype(BF16)
            cross = jnp.dot(qd, state.astype(BF16), preferred_element_type=F32)
            kd = (k.astype(F32) * kdec_ref[h]).astype(BF16)
            state_ref[h] = state * cdec_ref[h] + lax.dot_general(
                kd, v, (((0,), (0,)), ((), ())), preferred_element_type=F32)
            o = inner + cross
            o = o * lax.rsqrt(jnp.mean(o * o, axis=-1, keepdims=True) + NORM_EPS)
            gate = gate_ref[rows, v_cols].astype(F32)
            heads.append((_silu(gate) * o).astype(BF16))
        gated.append(jnp.concatenate(heads, axis=1))
    og = jnp.concatenate(gated, axis=0)
    o_ref[0] = x_ref[0] + jnp.dot(og, wout_ref[...], preferred_element_type=F32)


def _ret_core(proj, x, wout_bf, intra, qdec, kdec, cdec):
    b, seq, _ = x.shape
    ts = RET_SEQ_TILE
    st = seq // ts
    whole = lambda a: pl.BlockSpec(a.shape, lambda bi, s: (0,) * a.ndim)
    return pl.pallas_call(
        _ret_core_kernel,
        grid=(b, st),
        in_specs=[
            pl.BlockSpec((ts, RET_QK_TOTAL), lambda bi, s: (bi * st + s, 0)),
            pl.BlockSpec((ts, RET_QK_TOTAL), lambda bi, s: (bi * st + s, 1)),
            pl.BlockSpec((ts, RET_V_TOTAL), lambda bi, s: (bi * st + s, 1)),
            pl.BlockSpec((ts, RET_V_TOTAL), lambda bi, s: (bi * st + s, 2)),
            pl.BlockSpec((1, ts, D_MODEL), lambda bi, s: (bi, s, 0)),
            pl.BlockSpec((RET_V_TOTAL, D_MODEL), lambda bi, s: (0, 0), pipeline_mode=pl.Buffered(1)),
            whole(intra), whole(qdec), whole(kdec), whole(cdec),
        ],
        out_specs=pl.BlockSpec((1, ts, D_MODEL), lambda bi, s: (bi, s, 0)),
        out_shape=jax.ShapeDtypeStruct(x.shape, F32),
        scratch_shapes=[pltpu.VMEM((RET_HEADS, RET_QK_DIM, RET_V_DIM), F32)],
        compiler_params=_params("parallel", "arbitrary"),
        name="ret_core",
    )(proj, proj, proj, proj, x, wout_bf, intra, qdec, kdec, cdec)


def _swiglu(x_bf, wg_ref, wu_ref, wd_ref):
    out = None
    for c in range(D_FF // FF_TILE):
        cols = slice(c * FF_TILE, (c + 1) * FF_TILE)
        hg = jnp.dot(x_bf, wg_ref[:, cols], preferred_element_type=F32)
        hu = jnp.dot(x_bf, wu_ref[:, cols], preferred_element_type=F32)
        hid = (_silu(hg) * hu).astype(BF16)
        part = jnp.dot(hid, wd_ref[cols, :], preferred_element_type=F32)
        out = part if out is None else out + part
    return out


def _ffn_kernel(x_ref, g_ref, wg_ref, wu_ref, wd_ref, o_ref):
    for r in range(TOKEN_TILE // ROW_SUB):
        rows = pl.ds(r * ROW_SUB, ROW_SUB)
        x = x_ref[rows, :]
        xn = _rmsnorm_f32(x, g_ref[...]).astype(BF16)
        o_ref[rows, :] = x + _swiglu(xn, wg_ref, wu_ref, wd_ref)


def _ffn(x2, g, wg_bf, wu_bf, wd_bf):
    t = x2.shape[0]
    resident = lambda shape: pl.BlockSpec(shape, lambda i: (0, 0), pipeline_mode=pl.Buffered(1))
    return pl.pallas_call(
        _ffn_kernel,
        grid=(t // TOKEN_TILE,),
        in_specs=[
            pl.BlockSpec((TOKEN_TILE, D_MODEL), lambda i: (i, 0)),
            pl.BlockSpec((1, D_MODEL), lambda i: (0, 0)),
            resident((D_MODEL, D_FF)), resident((D_MODEL, D_FF)), resident((D_FF, D_MODEL)),
        ],
        out_specs=pl.BlockSpec((TOKEN_TILE, D_MODEL), lambda i: (i, 0)),
        out_shape=jax.ShapeDtypeStruct(x2.shape, F32),
        compiler_params=_params("parallel"),
        name="ffn_swiglu",
    )(x2, g, wg_bf, wu_bf, wd_bf)


def _conv_kernel(x_ref, g_ref, win_ref, wc_ref, wout_ref, o_ref, carry_ref):
    s = pl.program_id(1)
    ts = CONV_SEQ_TILE

    @pl.when(s == 0)
    def _():
        carry_ref[...] = jnp.zeros_like(carry_ref)

    x = x_ref[0]
    xn = _rmsnorm_f32(x, g_ref[...]).astype(BF16)
    proj = jnp.dot(xn, win_ref[...], preferred_element_type=F32)
    gate_b = proj[:, :D_MODEL]
    gate_c = proj[:, D_MODEL:2 * D_MODEL]
    hid = proj[:, 2 * D_MODEL:]
    u = gate_c * hid

    row = lax.broadcasted_iota(jnp.int32, (ts, D_MODEL), 0)
    prev1 = carry_ref[7:8, :]
    prev2 = carry_ref[6:7, :]
    u1 = jnp.where(row == 0, prev1, pltpu.roll(u, 1, axis=0))
    u2 = jnp.where(row == 0, prev2, jnp.where(row == 1, prev1, pltpu.roll(u, 2, axis=0)))
    carry_ref[...] = u[ts - 8:, :]

    wc = wc_ref[...]
    y = wc[2:3, :] * u + wc[1:2, :] * u1 + wc[0:1, :] * u2
    y = (gate_b * y).astype(BF16)
    o_ref[0] = x + jnp.dot(y, wout_ref[...], preferred_element_type=F32)


def _conv_layer(x, g, win_bf, wc, wout_bf):
    b, seq, _ = x.shape
    ts = CONV_SEQ_TILE
    return pl.pallas_call(
        _conv_kernel,
        grid=(b, seq // ts),
        in_specs=[
            pl.BlockSpec((1, ts, D_MODEL), lambda bi, s: (bi, s, 0)),
            pl.BlockSpec((1, D_MODEL), lambda bi, s: (0, 0)),
            pl.BlockSpec((D_MODEL, 3 * D_MODEL), lambda bi, s: (0, 0)),
            pl.BlockSpec((CONV_WIDTH, D_MODEL), lambda bi, s: (0, 0)),
            pl.BlockSpec((D_MODEL, D_MODEL), lambda bi, s: (0, 0)),
        ],
        out_specs=pl.BlockSpec((1, ts, D_MODEL), lambda bi, s: (bi, s, 0)),
        out_shape=jax.ShapeDtypeStruct(x.shape, F32),
        scratch_shapes=[pltpu.VMEM((8, D_MODEL), F32)],
        compiler_params=_params("parallel", "arbitrary"),
        name="short_conv",
    )(x, g, win_bf, wc, wout_bf)


def _split_bf16(x):
    hi = x.astype(BF16)
    return hi, (x - hi.astype(F32)).astype(BF16)


def _router_kernel(x_ref, g_ref, wrt_ref, before_ref, xn_ref, sel_e_ref, sel_r_ref, sel_g_ref,
                   cend_ref, carry_ref):
    @pl.when(pl.program_id(0) == 0)
    def _():
        carry_ref[...] = jnp.zeros_like(carry_ref)

    xn = _rmsnorm_f32(x_ref[...], g_ref[...])
    xn_ref[...] = _pack_bf16_pair(xn)
    x_hi, x_lo = _split_bf16(xn)
    w_hi, w_lo = _split_bf16(wrt_ref[...])
    nt = lambda a, b: lax.dot_general(a, b, (((1,), (1,)), ((), ())), preferred_element_type=F32)
    both = nt(jnp.concatenate([w_hi, w_lo], axis=0), x_hi)
    logits = both[:N_EXPERTS] + (both[N_EXPERTS:] + nt(w_hi, x_lo))
    sub = lax.broadcasted_iota(jnp.int32, logits.shape, 0).astype(F32)
    none = float(N_EXPERTS)
    m1 = jnp.max(logits, axis=0, keepdims=True)
    i1 = jnp.min(jnp.where(logits == m1, sub, none), axis=0, keepdims=True)
    rest = jnp.where(sub == i1, -jnp.inf, logits)
    m2 = jnp.max(rest, axis=0, keepdims=True)
    i2 = jnp.min(jnp.where(rest == m2, sub, none), axis=0, keepdims=True)
    e2 = jnp.exp(m2 - m1)
    denom = 1.0 + e2

    chosen = jnp.logical_or(sub == i1, sub == i2)
    prefix = jnp.dot(jnp.where(chosen, 1.0, 0.0).astype(BF16), before_ref[...],
                     preferred_element_type=F32)
    carry = carry_ref[...]
    rank = prefix + carry[:, :1]
    r1 = jnp.sum(jnp.where(sub == i1, rank, 0.0), axis=0, keepdims=True)
    r2 = jnp.sum(jnp.where(sub == i2, rank, 0.0), axis=0, keepdims=True)
    sel_e_ref[0:1, :] = i1.astype(I32)
    sel_e_ref[1:2, :] = i2.astype(I32)
    sel_r_ref[0:1, :] = r1.astype(I32)
    sel_r_ref[1:2, :] = r2.astype(I32)
    sel_g_ref[0:1, :] = 1.0 / denom
    sel_g_ref[1:2, :] = e2 / denom
    carry = carry + jnp.sum(jnp.where(chosen, 1.0, 0.0), axis=1, keepdims=True)
    carry_ref[...] = carry
    cend_ref[0] = carry


def _router(x2, g, wrt):
    t = x2.shape[0]
    w = ROUTE_WINDOW
    per_token = lambda dtype: jax.ShapeDtypeStruct((TOP_K, t), dtype)
    before = (jnp.arange(w)[:, None] < jnp.arange(w)[None, :]).astype(BF16)
    return pl.pallas_call(
        _router_kernel,
        grid=(t // w,),
        in_specs=[
            pl.BlockSpec((w, D_MODEL), lambda i: (i, 0)),
            pl.BlockSpec((1, D_MODEL), lambda i: (0, 0)),
            pl.BlockSpec((N_EXPERTS, D_MODEL), lambda i: (0, 0)),
            pl.BlockSpec((w, w), lambda i: (0, 0)),
        ],
        out_specs=[
            pl.BlockSpec((w, PACKED), lambda i: (i, 0)),
            pl.BlockSpec((TOP_K, w), lambda i: (0, i)),
            pl.BlockSpec((TOP_K, w), lambda i: (0, i)),
            pl.BlockSpec((TOP_K, w), lambda i: (0, i)),
            pl.BlockSpec((1, N_EXPERTS, LANES), lambda i: (i, 0, 0)),
        ],
        out_shape=[jax.ShapeDtypeStruct((t, PACKED), I32),
                   per_token(I32), per_token(I32), per_token(F32),
                   jax.ShapeDtypeStruct((t // w, N_EXPERTS, LANES), F32)],
        scratch_shapes=[pltpu.VMEM((N_EXPERTS, LANES), F32)],
        compiler_params=_params("arbitrary"),
        name="router",
    )(x2, g, wrt, before)


def _routing_plan(cend, sel_e, sel_r, t):
    counts = cend[-1, :, 0].astype(I32)
    ntiles = (counts + EXPERT_TILE - 1) // EXPERT_TILE
    tile_end = jnp.cumsum(ntiles)
    tile_start = tile_end - ntiles
    off = tile_start * EXPERT_TILE

    nt_max = TOP_K * t // EXPERT_TILE + N_EXPERTS
    ti = jnp.arange(nt_max, dtype=I32)
    tile_e = jnp.minimum(jnp.sum(tile_end[None, :] <= ti[:, None], axis=1).astype(I32),
                         N_EXPERTS - 1)
    tile_rows = jnp.clip(counts[tile_e] - (ti - tile_start[tile_e]) * EXPERT_TILE, 0, EXPERT_TILE)
    tile_rows = jnp.where(ti < tile_end[-1], tile_rows, 0)

    first_row = jnp.zeros_like(sel_e)
    for e in range(N_EXPERTS):
        first_row = first_row + jnp.where(sel_e == e, off[e], 0)
    dest = first_row + sel_r
    return tile_e, tile_rows, dest


def _sc_mesh():
    return plsc.VectorSubcoreMesh(core_axis_name="c", subcore_axis_name="s",
                                  num_cores=SC_CORES, num_subcores=SC_SUBCORES)


def _dispatch_rows(xn, dest, n_rows):
    t, d = xn.shape
    per_worker = t // SC_WORKERS
    chunks = per_worker // SC_ROWS
    dest4 = dest.reshape(TOP_K, SC_WORKERS, chunks, SC_ROWS)

    @functools.partial(
        pl.kernel, mesh=_sc_mesh(),
        out_type=jax.ShapeDtypeStruct((n_rows, d), xn.dtype),
        scratch_types=[pltpu.VMEM((TOP_K, chunks, SC_ROWS), I32),
                       pltpu.VMEM((SC_ROWS, d), xn.dtype)],
        name="moe_dispatch",
    )
    def run(x_hbm, dest_hbm, o_hbm, dest_v, rows_v):
        worker = lax.axis_index("s") * SC_CORES + lax.axis_index("c")
        for k in range(TOP_K):
            pltpu.sync_copy(dest_hbm.at[k, worker], dest_v.at[k])

        @pl.loop(0, chunks)
        def _(j):
            pltpu.sync_copy(x_hbm.at[pl.ds(worker * per_worker + j * SC_ROWS, SC_ROWS)], rows_v)
            for k in range(TOP_K):
                pltpu.sync_copy(rows_v, o_hbm.at[dest_v.at[k, j]])

    return run(xn, dest4)


def _collect_rows(y, dest):
    t = dest.shape[1]
    d = y.shape[1]
    per_worker = t // SC_WORKERS
    chunks = per_worker // SC_ROWS
    dest4 = dest.reshape(TOP_K, SC_WORKERS, chunks, SC_ROWS)

    @functools.partial(
        pl.kernel, mesh=_sc_mesh(),
        out_type=jax.ShapeDtypeStruct((TOP_K, t, d), y.dtype),
        scratch_types=[pltpu.VMEM((TOP_K, chunks, SC_ROWS), I32),
                       pltpu.VMEM((SC_ROWS, d), y.dtype)],
        name="moe_collect",
    )
    def run(y_hbm, dest_hbm, o_hbm, dest_v, rows_v):
        worker = lax.axis_index("s") * SC_CORES + lax.axis_index("c")
        for k in range(TOP_K):
            pltpu.sync_copy(dest_hbm.at[k, worker], dest_v.at[k])

        @pl.loop(0, chunks)
        def _(j):
            for k in range(TOP_K):
                pltpu.sync_copy(y_hbm.at[dest_v.at[k, j]], rows_v)
                pltpu.sync_copy(rows_v, o_hbm.at[k, pl.ds(worker * per_worker + j * SC_ROWS, SC_ROWS)])

    return run(y, dest4)


def _expert_kernel(tile_e_s, tile_rows_s, xs_ref, wg_ref, wu_ref, wd_ref, y_ref):
    n_rows = tile_rows_s[pl.program_id(0)]

    @pl.when(n_rows > 0)
    def _():
        for r in range(EXPERT_TILE // ROW_SUB):
            rows = pl.ds(r * ROW_SUB, ROW_SUB)
            real = lax.broadcasted_iota(jnp.int32, (ROW_SUB, PACKED), 0) < n_rows - r * ROW_SUB
            xs = _unpack_bf16_pair(jnp.where(real, xs_ref[rows, :], 0)).astype(BF16)
            y_ref[rows, :] = _pack_bf16_pair(_swiglu(xs, wg_ref.at[0], wu_ref.at[0], wd_ref.at[0]))

    @pl.when(n_rows == 0)
    def _():
        y_ref[...] = jnp.zeros_like(y_ref)


def _experts(tile_e, tile_rows, xs, wg_bf, wu_bf, wd_bf):
    n_rows = xs.shape[0]
    weights = lambda shape: pl.BlockSpec((1,) + shape, lambda i, te, tr: (te[i], 0, 0),
                                         pipeline_mode=pl.Buffered(1))
    grid_spec = pltpu.PrefetchScalarGridSpec(
        num_scalar_prefetch=2,
        grid=(n_rows // EXPERT_TILE,),
        in_specs=[
            pl.BlockSpec((EXPERT_TILE, PACKED), lambda i, te, tr: (i, 0)),
            weights((D_MODEL, D_FF)), weights((D_MODEL, D_FF)), weights((D_FF, D_MODEL)),
        ],
        out_specs=pl.BlockSpec((EXPERT_TILE, PACKED), lambda i, te, tr: (i, 0)),
    )
    return pl.pallas_call(
        _expert_kernel,
        grid_spec=grid_spec,
        out_shape=jax.ShapeDtypeStruct((n_rows, PACKED), I32),
        compiler_params=_params("arbitrary"),
        name="moe_experts",
    )(tile_e, tile_rows, xs, wg_bf, wu_bf, wd_bf)


def _final_kernel(x_ref, y_ref, sel_g_ref, gf_ref, o_ref):
    w = ROUTE_WINDOW
    diag = (lax.broadcasted_iota(jnp.int32, (w, w), 0)
            == lax.broadcasted_iota(jnp.int32, (w, w), 1))
    acc = x_ref[...]
    for k in range(TOP_K):
        gate = jnp.sum(jnp.where(diag, sel_g_ref[k:k + 1, :], 0.0), axis=1, keepdims=True)
        acc = acc + gate * _unpack_bf16_pair(y_ref[k])
    o_ref[...] = _rmsnorm_f32(acc, gf_ref[...])


def _final(x2, y_tok, sel_g, g_final):
    t = x2.shape[0]
    w = ROUTE_WINDOW
    return pl.pallas_call(
        _final_kernel,
        grid=(t // w,),
        in_specs=[
            pl.BlockSpec((w, D_MODEL), lambda i: (i, 0)),
            pl.BlockSpec((TOP_K, w, PACKED), lambda i: (0, i, 0)),
            pl.BlockSpec((TOP_K, w), lambda i: (0, i)),
            pl.BlockSpec((1, D_MODEL), lambda i: (0, 0)),
        ],
        out_specs=pl.BlockSpec((w, D_MODEL), lambda i: (i, 0)),
        out_shape=jax.ShapeDtypeStruct((t, D_MODEL), F32),
        compiler_params=_params("parallel"),
        name="moe_final",
    )(x2, y_tok, sel_g, g_final)


def _rotary_tables(seq):
    inv_freq = ROPE_BASE ** (-jnp.arange(0, RET_QK_DIM, 2, dtype=F32) / RET_QK_DIM)
    ang = jnp.arange(seq, dtype=F32)[:, None] * inv_freq[None, :]
    return jnp.cos(ang), jnp.sin(ang)


def _decay_tables():
    c = RET_CHUNK
    log_gamma = jnp.log(1.0 - 2.0 ** (-5.0 - jnp.arange(RET_HEADS, dtype=F32)))
    idx = jnp.arange(c, dtype=F32)
    diff = idx[:, None] - idx[None, :]
    causal = diff >= 0
    intra = jnp.where(causal[None],
                      jnp.exp(jnp.where(causal, diff, 0.0)[None] * log_gamma[:, None, None]), 0.0)
    qdec = jnp.exp((idx[None, :] + 1.0) * log_gamma[:, None])[..., None]
    kdec = jnp.exp((c - 1.0 - idx[None, :]) * log_gamma[:, None])[..., None]
    cdec = jnp.exp(c * log_gamma)[:, None, None]
    return intra, qdec, kdec, cdec


def kernel(x, norm_mix0, ret_w_in, ret_w_out, norm_ffn0, ffn_w_gate, ffn_w_up, ffn_w_down,
           norm_mix1, conv_w_in, conv_w, conv_w_out, norm_ffn1, moe_router, moe_w_gate,
           moe_w_up, moe_w_down, norm_final):
    b, seq, d = x.shape
    t = b * seq
    row = lambda g: g.reshape(1, d).astype(F32)
    bf = lambda w: w.astype(BF16)

    cos, sin = _rotary_tables(seq)
    intra, qdec, kdec, cdec = _decay_tables()

    proj = _ret_in(x.reshape(t, d), row(norm_mix0), bf(ret_w_in), cos, sin, seq)
    h = _ret_core(proj, x, bf(ret_w_out), intra, qdec, kdec, cdec)
    h = _ffn(h.reshape(t, d), row(norm_ffn0), bf(ffn_w_gate), bf(ffn_w_up), bf(ffn_w_down))
    h = _conv_layer(h.reshape(b, seq, d), row(norm_mix1), bf(conv_w_in),
                    conv_w.astype(F32), bf(conv_w_out))
    h2 = h.reshape(t, d)
    xn, sel_e, sel_r, sel_g, cend = _router(h2, row(norm_ffn1), moe_router.astype(F32).T)
    tile_e, tile_rows, dest = _routing_plan(cend, sel_e, sel_r, t)
    n_rows = (TOP_K * t // EXPERT_TILE + N_EXPERTS) * EXPERT_TILE
    xs = _dispatch_rows(xn, dest, n_rows)
    y = _experts(tile_e, tile_rows, xs, bf(moe_w_gate), bf(moe_w_up), bf(moe_w_down))
    y_tok = _collect_rows(y, dest)
    out = _final(h2, y_tok, sel_g, row(norm_final))
    return out.reshape(b, seq, d)
```

```python
import functools

import jax
import jax.numpy as jnp
from jax import lax
from jax.experimental import pallas as pl
from jax.experimental.pallas import tpu as pltpu
from jax.experimental.pallas import tpu_sc as plsc

D_MODEL = 1024
RET_HEADS = 4
RET_QK_DIM = D_MODEL // RET_HEADS
RET_V_DIM = 2 * RET_QK_DIM
RET_QK_TOTAL = RET_HEADS * RET_QK_DIM
RET_V_TOTAL = RET_HEADS * RET_V_DIM
RET_IN_COLS = 2 * RET_QK_TOTAL + 2 * RET_V_TOTAL
ROPE_BASE = 10000.0
CONV_WIDTH = 3
D_FF = 7 * D_MODEL // 2
N_EXPERTS = 8
TOP_K = 2
NORM_EPS = 1e-6

RET_CHUNK = 256
HALF = RET_QK_DIM // 2

VMEM_LIMIT_BYTES = 56 * 1024 * 1024
SC_CORES = 2
SC_SUBCORES = 16
SC_WORKERS = SC_CORES * SC_SUBCORES
SC_ROWS = 64

TOKEN_TILE = 1024
RET_SEQ_TILE = 512
CONV_SEQ_TILE = 512
FF_TILE = 1792
ROW_SUB = 512
IN_TOKEN_TILE = 512
IN_COL_TILE = 1024
ROUTE_WINDOW = 512
EXPERT_TILE = 1024
MOE_PARTS = 2
LANES = 128
BF16_SUBLANES = 16
SIDE_SLAB_BYTES = 2 * 1024 * 1024
PACKED = D_MODEL // 2

BF16 = jnp.bfloat16
F32 = jnp.float32
I32 = jnp.int32


def _params(*semantics):
    return pltpu.CompilerParams(dimension_semantics=semantics,
                                vmem_limit_bytes=VMEM_LIMIT_BYTES)


def _rmsnorm_f32(x, g):
    return x * lax.rsqrt(jnp.mean(x * x, axis=-1, keepdims=True) + NORM_EPS) * g


def _silu(x):
    return x * (1.0 / (1.0 + jnp.exp(-x)))


def _pack_bf16_pair(x):
    half = x.shape[1] // 2
    hi = lax.bitcast_convert_type(x[:, :half].astype(BF16).astype(F32), I32)
    lo = lax.bitcast_convert_type(x[:, half:].astype(BF16).astype(F32), I32)
    return hi | lax.shift_right_logical(lo, 16)


def _unpack_bf16_pair(w):
    hi = lax.bitcast_convert_type(w & jnp.int32(-65536), F32)
    lo = lax.bitcast_convert_type(lax.shift_left(w, 16), F32)
    return jnp.concatenate([hi, lo], axis=1)


def _ret_in_kernel(x_ref, g_ref, w_ref, cos_ref, sin_ref, o_ref):
    xn = _rmsnorm_f32(x_ref[...], g_ref[...]).astype(BF16)
    cos = cos_ref[...]
    sin = sin_ref[...]
    for n in range(RET_IN_COLS // IN_COL_TILE):
        cols = slice(n * IN_COL_TILE, (n + 1) * IN_COL_TILE)
        acc = jnp.dot(xn, w_ref[:, cols], preferred_element_type=F32)
        if n * IN_COL_TILE >= 2 * RET_QK_TOTAL:
            o_ref[:, cols] = acc.astype(BF16)
            continue
        scale = 1.0 if n * IN_COL_TILE < RET_QK_TOTAL else RET_QK_DIM ** -0.5
        for h in range(IN_COL_TILE // RET_QK_DIM):
            lo = h * RET_QK_DIM
            t1 = acc[:, lo:lo + HALF]
            t2 = acc[:, lo + HALF:lo + RET_QK_DIM]
            base = n * IN_COL_TILE + lo
            o_ref[:, base:base + HALF] = ((t1 * cos - t2 * sin) * scale).astype(BF16)
            o_ref[:, base + HALF:base + RET_QK_DIM] = ((t1 * sin + t2 * cos) * scale).astype(BF16)


def _ret_in(x2, g, w_bf, cos, sin, seq):
    t = x2.shape[0]
    tm = IN_TOKEN_TILE
    seq_tiles = seq // tm
    return pl.pallas_call(
        _ret_in_kernel,
        grid=(t // tm,),
        in_specs=[
            pl.BlockSpec((tm, D_MODEL), lambda i: (i, 0)),
            pl.BlockSpec((1, D_MODEL), lambda i: (0, 0)),
            pl.BlockSpec((D_MODEL, RET_IN_COLS), lambda i: (0, 0), pipeline_mode=pl.Buffered(1)),
            pl.BlockSpec((tm, HALF), lambda i: (i % seq_tiles, 0)),
            pl.BlockSpec((tm, HALF), lambda i: (i % seq_tiles, 0)),
        ],
        out_specs=pl.BlockSpec((tm, RET_IN_COLS), lambda i: (i, 0)),
        out_shape=jax.ShapeDtypeStruct((t, RET_IN_COLS), BF16),
        compiler_params=_params("parallel"),
        name="ret_in_proj",
    )(x2, g, w_bf, cos, sin)


def _ret_core_kernel(q_ref, k_ref, v_ref, gate_ref, x_ref, wout_ref,
                     intra_ref, qdec_ref, kdec_ref, cdec_ref, o_ref, state_ref):
    @pl.when(pl.program_id(1) == 0)
    def _():
        state_ref[...] = jnp.zeros_like(state_ref)

    gated = []
    for c in range(RET_SEQ_TILE // RET_CHUNK):
        rows = pl.ds(c * RET_CHUNK, RET_CHUNK)
        heads = []
        for h in range(RET_HEADS):
            qk_cols = slice(h * RET_QK_DIM, (h + 1) * RET_QK_DIM)
            v_cols = slice(h * RET_V_DIM, (h + 1) * RET_V_DIM)
            q = q_ref[rows, qk_cols]
            k = k_ref[rows, qk_cols]
            v = v_ref[rows, v_cols]
            state = state_ref[h]
            scores = lax.dot_general(q, k, (((1,), (1,)), ((), ())),
                                     preferred_element_type=F32) * intra_ref[h]
            inner = jnp.dot(scores.astype(BF16), v, preferred_element_type=F32)
            qd = (q.astype(F32) * qdec_ref[h]).astype(BF16)
            cross = jnp.dot(qd, state.astype(BF16), preferred_element_type=F32)
            kd = (k.astype(F32) * kdec_ref[h]).astype(BF16)
            state_ref[h] = state * cdec_ref[h] + lax.dot_general(
                kd, v, (((0,), (0,)), ((), ())), preferred_element_type=F32)
            o = inner + cross
            o = o * lax.rsqrt(jnp.mean(o * o, axis=-1, keepdims=True) + NORM_EPS)
            gate = gate_ref[rows, v_cols].astype(F32)
            heads.append((_silu(gate) * o).astype(BF16))
        gated.append(jnp.concatenate(heads, axis=1))
    og = jnp.concatenate(gated, axis=0)
    o_ref[0] = x_ref[0] + jnp.dot(og, wout_ref[...], preferred_element_type=F32)


def _ret_core(proj, x, wout_bf, intra, qdec, kdec, cdec):
    b, seq, _ = x.shape
    ts = RET_SEQ_TILE
    st = seq // ts
    whole = lambda a: pl.BlockSpec(a.shape, lambda bi, s: (0,) * a.ndim)
    return pl.pallas_call(
        _ret_core_kernel,
        grid=(b, st),
        in_specs=[
            pl.BlockSpec((ts, RET_QK_TOTAL), lambda bi, s: (bi * st + s, 0)),
            pl.BlockSpec((ts, RET_QK_TOTAL), lambda bi, s: (bi * st + s, 1)),
            pl.BlockSpec((ts, RET_V_TOTAL), lambda bi, s: (bi * st + s, 1)),
            pl.BlockSpec((ts, RET_V_TOTAL), lambda bi, s: (bi * st + s, 2)),
            pl.BlockSpec((1, ts, D_MODEL), lambda bi, s: (bi, s, 0)),
            pl.BlockSpec((RET_V_TOTAL, D_MODEL), lambda bi, s: (0, 0), pipeline_mode=pl.Buffered(1)),
            whole(intra), whole(qdec), whole(kdec), whole(cdec),
        ],
        out_specs=pl.BlockSpec((1, ts, D_MODEL), lambda bi, s: (bi, s, 0)),
        out_shape=jax.ShapeDtypeStruct(x.shape, F32),
        scratch_shapes=[pltpu.VMEM((RET_HEADS, RET_QK_DIM, RET_V_DIM), F32)],
        compiler_params=_params("parallel", "arbitrary"),
        name="ret_core",
    )(proj, proj, proj, proj, x, wout_bf, intra, qdec, kdec, cdec)


def _swiglu(x_bf, wg_ref, wu_ref, wd_ref):
    out = None
    for c in range(D_FF // FF_TILE):
        cols = slice(c * FF_TILE, (c + 1) * FF_TILE)
        hg = jnp.dot(x_bf, wg_ref[:, cols], preferred_element_type=F32)
        hu = jnp.dot(x_bf, wu_ref[:, cols], preferred_element_type=F32)
        hid = (_silu(hg) * hu).astype(BF16)
        part = jnp.dot(hid, wd_ref[cols, :], preferred_element_type=F32)
        out = part if out is None else out + part
    return out


def _ffn_kernel(x_ref, g_ref, wg_ref, wu_ref, wd_ref, o_ref):
    for r in range(TOKEN_TILE // ROW_SUB):
        rows = pl.ds(r * ROW_SUB, ROW_SUB)
        x = x_ref[rows, :]
        xn = _rmsnorm_f32(x, g_ref[...]).astype(BF16)
        o_ref[rows, :] = x + _swiglu(xn, wg_ref, wu_ref, wd_ref)


def _ffn(x2, g, wg_bf, wu_bf, wd_bf):
    t = x2.shape[0]
    resident = lambda shape: pl.BlockSpec(shape, lambda i: (0, 0), pipeline_mode=pl.Buffered(1))
    return pl.pallas_call(
        _ffn_kernel,
        grid=(t // TOKEN_TILE,),
        in_specs=[
            pl.BlockSpec((TOKEN_TILE, D_MODEL), lambda i: (i, 0)),
            pl.BlockSpec((1, D_MODEL), lambda i: (0, 0)),
            resident((D_MODEL, D_FF)), resident((D_MODEL, D_FF)), resident((D_FF, D_MODEL)),
        ],
        out_specs=pl.BlockSpec((TOKEN_TILE, D_MODEL), lambda i: (i, 0)),
        out_shape=jax.ShapeDtypeStruct(x2.shape, F32),
        compiler_params=_params("parallel"),
        name="ffn_swiglu",
    )(x2, g, wg_bf, wu_bf, wd_bf)


def _conv_kernel(x_ref, g_ref, win_ref, wc_ref, wout_ref, *rest):
    n_side = (len(rest) - 2) // 2
    side_in, o_ref, side_out, carry_ref = (rest[:n_side], rest[n_side],
                                           rest[n_side + 1:2 * n_side + 1], rest[-1])
    s = pl.program_id(1)
    ts = CONV_SEQ_TILE

    for src, dst in zip(side_in, side_out):
        dst[...] = src[...].astype(BF16)

    @pl.when(s == 0)
    def _():
        carry_ref[...] = jnp.zeros_like(carry_ref)

    x = x_ref[0]
    xn = _rmsnorm_f32(x, g_ref[...]).astype(BF16)
    proj = jnp.dot(xn, win_ref[...], preferred_element_type=F32)
    gate_b = proj[:, :D_MODEL]
    gate_c = proj[:, D_MODEL:2 * D_MODEL]
    hid = proj[:, 2 * D_MODEL:]
    u = gate_c * hid

    row = lax.broadcasted_iota(jnp.int32, (ts, D_MODEL), 0)
    prev1 = carry_ref[7:8, :]
    prev2 = carry_ref[6:7, :]
    u1 = jnp.where(row == 0, prev1, pltpu.roll(u, 1, axis=0))
    u2 = jnp.where(row == 0, prev2, jnp.where(row == 1, prev1, pltpu.roll(u, 2, axis=0)))
    carry_ref[...] = u[ts - 8:, :]

    wc = wc_ref[...]
    y = wc[2:3, :] * u + wc[1:2, :] * u1 + wc[0:1, :] * u2
    y = (gate_b * y).astype(BF16)
    o_ref[0] = x + jnp.dot(y, wout_ref[...], preferred_element_type=F32)


def _conv_layer(x, g, win_bf, wc, wout_bf, side_weights):
    b, seq, _ = x.shape
    ts = CONV_SEQ_TILE
    st = seq // ts
    steps = b * st
    resident = lambda shape: pl.BlockSpec(shape, lambda bi, s: (0, 0), pipeline_mode=pl.Buffered(1))

    def slab_rows(w):
        rows = w.shape[0] * w.shape[1]
        per_step = rows // steps
        ok = (rows % steps == 0 and per_step % BF16_SUBLANES == 0 and w.shape[1] % per_step == 0
              and per_step * w.shape[2] * 4 <= SIDE_SLAB_BYTES)
        return per_step if ok else None

    hosted = all(slab_rows(w) is not None for w in side_weights)
    side = side_weights if hosted else ()
    side_specs = []
    for w in side:
        per_step = slab_rows(w)
        per_mat = w.shape[1] // per_step
        side_specs.append(pl.BlockSpec(
            (1, per_step, w.shape[2]),
            lambda bi, s, per_mat=per_mat: ((bi * st + s) // per_mat, (bi * st + s) % per_mat, 0)))

    outs = pl.pallas_call(
        _conv_kernel,
        grid=(b, st),
        in_specs=[
            pl.BlockSpec((1, ts, D_MODEL), lambda bi, s: (bi, s, 0)),
            pl.BlockSpec((1, D_MODEL), lambda bi, s: (0, 0)),
            resident((D_MODEL, 3 * D_MODEL)),
            pl.BlockSpec((CONV_WIDTH, D_MODEL), lambda bi, s: (0, 0)),
            resident((D_MODEL, D_MODEL)),
        ] + side_specs,
        out_specs=[pl.BlockSpec((1, ts, D_MODEL), lambda bi, s: (bi, s, 0))] + side_specs,
        out_shape=[jax.ShapeDtypeStruct(x.shape, F32)]
                  + [jax.ShapeDtypeStruct(w.shape, BF16) for w in side],
        scratch_shapes=[pltpu.VMEM((8, D_MODEL), F32)],
        compiler_params=_params("arbitrary", "arbitrary"),
        name="short_conv",
    )(x, g, win_bf, wc, wout_bf, *side)
    if hosted:
        return outs[0], tuple(outs[1:])
    return outs[0], tuple(w.astype(BF16) for w in side_weights)


def _split_bf16(x):
    hi = x.astype(BF16)
    return hi, (x - hi.astype(F32)).astype(BF16)


def _router_kernel(x_ref, g_ref, wrt_ref, before_ref, xn_ref, sel_e_ref, sel_r_ref, sel_g_ref,
                   cend_ref, carry_ref):
    @pl.when(pl.program_id(0) == 0)
    def _():
        carry_ref[...] = jnp.zeros_like(carry_ref)

    xn = _rmsnorm_f32(x_ref[...], g_ref[...])
    xn_ref[...] = _pack_bf16_pair(xn)
    x_hi, x_lo = _split_bf16(xn)
    w_hi, w_lo = _split_bf16(wrt_ref[...])
    nt = lambda a, b: lax.dot_general(a, b, (((1,), (1,)), ((), ())), preferred_element_type=F32)
    both = nt(jnp.concatenate([w_hi, w_lo], axis=0), x_hi)
    logits = both[:N_EXPERTS] + (both[N_EXPERTS:] + nt(w_hi, x_lo))
    sub = lax.broadcasted_iota(jnp.int32, logits.shape, 0).astype(F32)
    none = float(N_EXPERTS)
    m1 = jnp.max(logits, axis=0, keepdims=True)
    i1 = jnp.min(jnp.where(logits == m1, sub, none), axis=0, keepdims=True)
    rest = jnp.where(sub == i1, -jnp.inf, logits)
    m2 = jnp.max(rest, axis=0, keepdims=True)
    i2 = jnp.min(jnp.where(rest == m2, sub, none), axis=0, keepdims=True)
    e2 = jnp.exp(m2 - m1)
    denom = 1.0 + e2

    chosen = jnp.logical_or(sub == i1, sub == i2)
    prefix = jnp.dot(jnp.where(chosen, 1.0, 0.0).astype(BF16), before_ref[...],
                     preferred_element_type=F32)
    carry = carry_ref[...]
    rank = prefix + carry[:, :1]
    r1 = jnp.sum(jnp.where(sub == i1, rank, 0.0), axis=0, keepdims=True)
    r2 = jnp.sum(jnp.where(sub == i2, rank, 0.0), axis=0, keepdims=True)
    sel_e_ref[0:1, :] = i1.astype(I32)
    sel_e_ref[1:2, :] = i2.astype(I32)
    sel_r_ref[0:1, :] = r1.astype(I32)
    sel_r_ref[1:2, :] = r2.astype(I32)
    sel_g_ref[0:1, :] = 1.0 / denom
    sel_g_ref[1:2, :] = e2 / denom
    carry = carry + jnp.sum(jnp.where(chosen, 1.0, 0.0), axis=1, keepdims=True)
    carry_ref[...] = carry
    cend_ref[0] = carry


def _router(x2, g, wrt, part):
    t = x2.shape[0] // MOE_PARTS
    w = ROUTE_WINDOW
    first = part * (t // w)
    per_token = lambda dtype: jax.ShapeDtypeStruct((TOP_K, t), dtype)
    before = (jnp.arange(w)[:, None] < jnp.arange(w)[None, :]).astype(BF16)
    return pl.pallas_call(
        _router_kernel,
        grid=(t // w,),
        in_specs=[
            pl.BlockSpec((w, D_MODEL), lambda i: (first + i, 0)),
            pl.BlockSpec((1, D_MODEL), lambda i: (0, 0)),
            pl.BlockSpec((N_EXPERTS, D_MODEL), lambda i: (0, 0)),
            pl.BlockSpec((w, w), lambda i: (0, 0)),
        ],
        out_specs=[
            pl.BlockSpec((w, PACKED), lambda i: (i, 0)),
            pl.BlockSpec((TOP_K, w), lambda i: (0, i)),
            pl.BlockSpec((TOP_K, w), lambda i: (0, i)),
            pl.BlockSpec((TOP_K, w), lambda i: (0, i)),
            pl.BlockSpec((1, N_EXPERTS, LANES), lambda i: (i, 0, 0)),
        ],
        out_shape=[jax.ShapeDtypeStruct((t, PACKED), I32),
                   per_token(I32), per_token(I32), per_token(F32),
                   jax.ShapeDtypeStruct((t // w, N_EXPERTS, LANES), F32)],
        scratch_shapes=[pltpu.VMEM((N_EXPERTS, LANES), F32)],
        compiler_params=_params("arbitrary"),
        name="router",
    )(x2, g, wrt, before)


def _routing_plan(cend, sel_e, sel_r, t):
    counts = cend[-1, :, 0].astype(I32)
    ntiles = (counts + EXPERT_TILE - 1) // EXPERT_TILE
    tile_end = jnp.cumsum(ntiles)
    tile_start = tile_end - ntiles
    off = tile_start * EXPERT_TILE

    nt_max = TOP_K * t // EXPERT_TILE + N_EXPERTS
    ti = jnp.arange(nt_max, dtype=I32)
    tile_e = jnp.minimum(jnp.sum(tile_end[None, :] <= ti[:, None], axis=1).astype(I32),
                         N_EXPERTS - 1)
    tile_rows = jnp.clip(counts[tile_e] - (ti - tile_start[tile_e]) * EXPERT_TILE, 0, EXPERT_TILE)
    tile_rows = jnp.where(ti < tile_end[-1], tile_rows, 0)

    first_row = jnp.zeros_like(sel_e)
    for e in range(N_EXPERTS):
        first_row = first_row + jnp.where(sel_e == e, off[e], 0)
    dest = first_row + sel_r
    return tile_e, tile_rows, dest


def _sc_mesh():
    return plsc.VectorSubcoreMesh(core_axis_name="c", subcore_axis_name="s",
                                  num_cores=SC_CORES, num_subcores=SC_SUBCORES)


def _dispatch_rows(xn, dest, n_rows):
    t, d = xn.shape
    per_worker = t // SC_WORKERS
    chunks = per_worker // SC_ROWS
    dest4 = dest.reshape(TOP_K, SC_WORKERS, chunks, SC_ROWS)

    @functools.partial(
        pl.kernel, mesh=_sc_mesh(),
        out_type=jax.ShapeDtypeStruct((n_rows, d), xn.dtype),
        scratch_types=[pltpu.VMEM((TOP_K, chunks, SC_ROWS), I32),
                       pltpu.VMEM((SC_ROWS, d), xn.dtype)],
        name="moe_dispatch",
    )
    def run(x_hbm, dest_hbm, o_hbm, dest_v, rows_v):
        worker = lax.axis_index("s") * SC_CORES + lax.axis_index("c")
        for k in range(TOP_K):
            pltpu.sync_copy(dest_hbm.at[k, worker], dest_v.at[k])

        @pl.loop(0, chunks)
        def _(j):
            pltpu.sync_copy(x_hbm.at[pl.ds(worker * per_worker + j * SC_ROWS, SC_ROWS)], rows_v)
            for k in range(TOP_K):
                pltpu.sync_copy(rows_v, o_hbm.at[dest_v.at[k, j]])

    return run(xn, dest4)


def _collect_rows(y, dest):
    t = dest.shape[1]
    d = y.shape[1]
    per_worker = t // SC_WORKERS
    chunks = per_worker // SC_ROWS
    dest4 = dest.reshape(TOP_K, SC_WORKERS, chunks, SC_ROWS)

    @functools.partial(
        pl.kernel, mesh=_sc_mesh(),
        out_type=jax.ShapeDtypeStruct((TOP_K, t, d), y.dtype),
        scratch_types=[pltpu.VMEM((TOP_K, chunks, SC_ROWS), I32),
                       pltpu.VMEM((SC_ROWS, d), y.dtype)],
        name="moe_collect",
    )
    def run(y_hbm, dest_hbm, o_hbm, dest_v, rows_v):
        worker = lax.axis_index("s") * SC_CORES + lax.axis_index("c")
        for k in range(TOP_K):
            pltpu.sync_copy(dest_hbm.at[k, worker], dest_v.at[k])

        @pl.loop(0, chunks)
        def _(j):
            for k in range(TOP_K):
                pltpu.sync_copy(y_hbm.at[dest_v.at[k, j]], rows_v)
                pltpu.sync_copy(rows_v, o_hbm.at[k, pl.ds(worker * per_worker + j * SC_ROWS, SC_ROWS)])

    return run(y, dest4)


def _expert_kernel(tile_e_s, tile_rows_s, xs_ref, wg_ref, wu_ref, wd_ref, y_ref):
    n_rows = tile_rows_s[pl.program_id(0)]

    @pl.when(n_rows > 0)
    def _():
        for r in range(EXPERT_TILE // ROW_SUB):
            rows = pl.ds(r * ROW_SUB, ROW_SUB)
            real = lax.broadcasted_iota(jnp.int32, (ROW_SUB, PACKED), 0) < n_rows - r * ROW_SUB
            xs = _unpack_bf16_pair(jnp.where(real, xs_ref[rows, :], 0)).astype(BF16)
            y_ref[rows, :] = _pack_bf16_pair(_swiglu(xs, wg_ref.at[0], wu_ref.at[0], wd_ref.at[0]))

    @pl.when(n_rows == 0)
    def _():
        y_ref[...] = jnp.zeros_like(y_ref)


def _experts(tile_e, tile_rows, xs, wg_bf, wu_bf, wd_bf):
    n_rows = xs.shape[0]
    weights = lambda shape: pl.BlockSpec((1,) + shape, lambda i, te, tr: (te[i], 0, 0),
                                         pipeline_mode=pl.Buffered(1))
    grid_spec = pltpu.PrefetchScalarGridSpec(
        num_scalar_prefetch=2,
        grid=(n_rows // EXPERT_TILE,),
        in_specs=[
            pl.BlockSpec((EXPERT_TILE, PACKED), lambda i, te, tr: (i, 0)),
            weights((D_MODEL, D_FF)), weights((D_MODEL, D_FF)), weights((D_FF, D_MODEL)),
        ],
        out_specs=pl.BlockSpec((EXPERT_TILE, PACKED), lambda i, te, tr: (i, 0)),
    )
    return pl.pallas_call(
        _expert_kernel,
        grid_spec=grid_spec,
        out_shape=jax.ShapeDtypeStruct((n_rows, PACKED), I32),
        compiler_params=_params("arbitrary"),
        name="moe_experts",
    )(tile_e, tile_rows, xs, wg_bf, wu_bf, wd_bf)


def _final_kernel(x_ref, gf_ref, *rest):
    y_refs, sel_g_refs, o_ref = rest[:MOE_PARTS], rest[MOE_PARTS:2 * MOE_PARTS], rest[-1]
    w = ROUTE_WINDOW
    part = pl.program_id(0) // (pl.num_programs(0) // MOE_PARTS)
    diag = (lax.broadcasted_iota(jnp.int32, (w, w), 0)
            == lax.broadcasted_iota(jnp.int32, (w, w), 1))
    acc = x_ref[...]
    for k in range(TOP_K):
        sel_g = sel_g_refs[0][k:k + 1, :]
        y = y_refs[0][k]
        for p in range(1, MOE_PARTS):
            sel_g = jnp.where(part == p, sel_g_refs[p][k:k + 1, :], sel_g)
            y = jnp.where(part == p, y_refs[p][k], y)
        gate = jnp.sum(jnp.where(diag, sel_g, 0.0), axis=1, keepdims=True)
        acc = acc + gate * _unpack_bf16_pair(y)
    o_ref[...] = _rmsnorm_f32(acc, gf_ref[...])


def _final(x2, y_parts, sel_g_parts, g_final):
    t = x2.shape[0]
    w = ROUTE_WINDOW
    per_part = t // MOE_PARTS // w

    def window(p):
        return lambda i: jnp.clip(i - p * per_part, 0, per_part - 1)

    y_specs = [pl.BlockSpec((TOP_K, w, PACKED), lambda i, f=window(p): (0, f(i), 0))
               for p in range(MOE_PARTS)]
    g_specs = [pl.BlockSpec((TOP_K, w), lambda i, f=window(p): (0, f(i)))
               for p in range(MOE_PARTS)]
    return pl.pallas_call(
        _final_kernel,
        grid=(t // w,),
        in_specs=[
            pl.BlockSpec((w, D_MODEL), lambda i: (i, 0)),
            pl.BlockSpec((1, D_MODEL), lambda i: (0, 0)),
        ] + y_specs + g_specs,
        out_specs=pl.BlockSpec((w, D_MODEL), lambda i: (i, 0)),
        out_shape=jax.ShapeDtypeStruct((t, D_MODEL), F32),
        compiler_params=_params("parallel"),
        name="moe_final",
    )(x2, g_final, *y_parts, *sel_g_parts)


def _rotary_tables(seq):
    inv_freq = ROPE_BASE ** (-jnp.arange(0, RET_QK_DIM, 2, dtype=F32) / RET_QK_DIM)
    ang = jnp.arange(seq, dtype=F32)[:, None] * inv_freq[None, :]
    return jnp.cos(ang), jnp.sin(ang)


def _decay_tables():
    c = RET_CHUNK
    log_gamma = jnp.log(1.0 - 2.0 ** (-5.0 - jnp.arange(RET_HEADS, dtype=F32)))
    idx = jnp.arange(c, dtype=F32)
    diff = idx[:, None] - idx[None, :]
    causal = diff >= 0
    intra = jnp.where(causal[None],
                      jnp.exp(jnp.where(causal, diff, 0.0)[None] * log_gamma[:, None, None]), 0.0)
    qdec = jnp.exp((idx[None, :] + 1.0) * log_gamma[:, None])[..., None]
    kdec = jnp.exp((c - 1.0 - idx[None, :]) * log_gamma[:, None])[..., None]
    cdec = jnp.exp(c * log_gamma)[:, None, None]
    return intra, qdec, kdec, cdec


def kernel(x, norm_mix0, ret_w_in, ret_w_out, norm_ffn0, ffn_w_gate, ffn_w_up, ffn_w_down,
           norm_mix1, conv_w_in, conv_w, conv_w_out, norm_ffn1, moe_router, moe_w_gate,
           moe_w_up, moe_w_down, norm_final):
    b, seq, d = x.shape
    t = b * seq
    row = lambda g: g.reshape(1, d).astype(F32)
    bf = lambda w: w.astype(BF16)

    cos, sin = _rotary_tables(seq)
    intra, qdec, kdec, cdec = _decay_tables()

    proj = _ret_in(x.reshape(t, d), row(norm_mix0), bf(ret_w_in), cos, sin, seq)
    h = _ret_core(proj, x, bf(ret_w_out), intra, qdec, kdec, cdec)
    h = _ffn(h.reshape(t, d), row(norm_ffn0), bf(ffn_w_gate), bf(ffn_w_up), bf(ffn_w_down))
    h, expert_w = _conv_layer(h.reshape(b, seq, d), row(norm_mix1), bf(conv_w_in),
                              conv_w.astype(F32), bf(conv_w_out),
                              (moe_w_gate, moe_w_up, moe_w_down))
    h2 = h.reshape(t, d)

    tp = t // MOE_PARTS
    n_rows = (TOP_K * tp // EXPERT_TILE + N_EXPERTS) * EXPERT_TILE
    y_parts, gate_parts = [], []
    for part in range(MOE_PARTS):
        xn, sel_e, sel_r, sel_g, cend = _router(h2, row(norm_ffn1), moe_router.astype(F32).T, part)
        tile_e, tile_rows, dest = _routing_plan(cend, sel_e, sel_r, tp)
        xs = _dispatch_rows(xn, dest, n_rows)
        y = _experts(tile_e, tile_rows, xs, *expert_w)
        y_parts.append(_collect_rows(y, dest))
        gate_parts.append(sel_g)
    out = _final(h2, y_parts, gate_parts, row(norm_final))
    return out.reshape(b, seq, d)
```

```python
import functools

import jax
import jax.numpy as jnp
from jax import lax
from jax.experimental import pallas as pl
from jax.experimental.pallas import tpu as pltpu
from jax.experimental.pallas import tpu_sc as plsc

D_MODEL = 1024
RET_HEADS = 4
RET_QK_DIM = D_MODEL // RET_HEADS
RET_V_DIM = 2 * RET_QK_DIM
RET_QK_TOTAL = RET_HEADS * RET_QK_DIM
RET_V_TOTAL = RET_HEADS * RET_V_DIM
RET_IN_COLS = 2 * RET_QK_TOTAL + 2 * RET_V_TOTAL
ROPE_BASE = 10000.0
CONV_WIDTH = 3
D_FF = 7 * D_MODEL // 2
N_EXPERTS = 8
TOP_K = 2
NORM_EPS = 1e-6

RET_CHUNK = 256
HALF = RET_QK_DIM // 2

VMEM_LIMIT_BYTES = 56 * 1024 * 1024
SC_CORES = 2
SC_SUBCORES = 16
SC_WORKERS = SC_CORES * SC_SUBCORES
SC_ROWS = 64

TOKEN_TILE = 1024
RET_SEQ_TILE = 512
CONV_SEQ_TILE = 512
FF_TILE = 1792
ROW_SUB = 512
IN_TOKEN_TILE = 512
IN_COL_TILE = 1024
ROUTE_WINDOW = 512
EXPERT_TILE = 1024
MOE_PARTS = 2
LANES = 128
BF16_SUBLANES = 16
SIDE_SLAB_BYTES = 2 * 1024 * 1024
PACKED = D_MODEL // 2

BF16 = jnp.bfloat16
F32 = jnp.float32
I32 = jnp.int32


def _params(*semantics):
    return pltpu.CompilerParams(dimension_semantics=semantics,
                                vmem_limit_bytes=VMEM_LIMIT_BYTES)


def _rmsnorm_f32(x, g):
    return x * lax.rsqrt(jnp.mean(x * x, axis=-1, keepdims=True) + NORM_EPS) * g


def _silu(x):
    return x * (1.0 / (1.0 + jnp.exp(-x)))


def _pack_bf16_pair(x):
    half = x.shape[1] // 2
    hi = lax.bitcast_convert_type(x[:, :half].astype(BF16).astype(F32), I32)
    lo = lax.bitcast_convert_type(x[:, half:].astype(BF16).astype(F32), I32)
    return hi | lax.shift_right_logical(lo, 16)


def _unpack_bf16_pair(w):
    hi = lax.bitcast_convert_type(w & jnp.int32(-65536), F32)
    lo = lax.bitcast_convert_type(lax.shift_left(w, 16), F32)
    return jnp.concatenate([hi, lo], axis=1)


def _ret_in_kernel(x_ref, g_ref, w_ref, cos_ref, sin_ref, o_ref):
    xn = _rmsnorm_f32(x_ref[...], g_ref[...]).astype(BF16)
    cos = cos_ref[...]
    sin = sin_ref[...]
    for n in range(RET_IN_COLS // IN_COL_TILE):
        cols = slice(n * IN_COL_TILE, (n + 1) * IN_COL_TILE)
        acc = jnp.dot(xn, w_ref[:, cols], preferred_element_type=F32)
        if n * IN_COL_TILE >= 2 * RET_QK_TOTAL:
            o_ref[:, cols] = acc.astype(BF16)
            continue
        scale = 1.0 if n * IN_COL_TILE < RET_QK_TOTAL else RET_QK_DIM ** -0.5
        for h in range(IN_COL_TILE // RET_QK_DIM):
            lo = h * RET_QK_DIM
            t1 = acc[:, lo:lo + HALF]
            t2 = acc[:, lo + HALF:lo + RET_QK_DIM]
            base = n * IN_COL_TILE + lo
            o_ref[:, base:base + HALF] = ((t1 * cos - t2 * sin) * scale).astype(BF16)
            o_ref[:, base + HALF:base + RET_QK_DIM] = ((t1 * sin + t2 * cos) * scale).astype(BF16)


def _ret_in(x2, g, w_bf, cos, sin, seq):
    t = x2.shape[0]
    tm = IN_TOKEN_TILE
    seq_tiles = seq // tm
    return pl.pallas_call(
        _ret_in_kernel,
        grid=(t // tm,),
        in_specs=[
            pl.BlockSpec((tm, D_MODEL), lambda i: (i, 0)),
            pl.BlockSpec((1, D_MODEL), lambda i: (0, 0)),
            pl.BlockSpec((D_MODEL, RET_IN_COLS), lambda i: (0, 0), pipeline_mode=pl.Buffered(1)),
            pl.BlockSpec((tm, HALF), lambda i: (i % seq_tiles, 0)),
            pl.BlockSpec((tm, HALF), lambda i: (i % seq_tiles, 0)),
        ],
        out_specs=pl.BlockSpec((tm, RET_IN_COLS), lambda i: (i, 0)),
        out_shape=jax.ShapeDtypeStruct((t, RET_IN_COLS), BF16),
        compiler_params=_params("parallel"),
        name="ret_in_proj",
    )(x2, g, w_bf, cos, sin)


def _ret_core_kernel(q_ref, k_ref, v_ref, gate_ref, x_ref, wout_ref,
                     intra_ref, qdec_ref, kdec_ref, cdec_ref, *rest):
    n_side = (len(rest) - 2) // 2
    side_in, o_ref, side_out, state_ref = (rest[:n_side], rest[n_side],
                                           rest[n_side + 1:2 * n_side + 1], rest[-1])

    for src, dst in zip(side_in, side_out):
        dst[...] = src[...].astype(BF16)

    @pl.when(pl.program_id(1) == 0)
    def _():
        state_ref[...] = jnp.zeros_like(state_ref)

    gated = []
    for c in range(RET_SEQ_TILE // RET_CHUNK):
        rows = pl.ds(c * RET_CHUNK, RET_CHUNK)
        heads = []
        for h in range(RET_HEADS):
            qk_cols = slice(h * RET_QK_DIM, (h + 1) * RET_QK_DIM)
            v_cols = slice(h * RET_V_DIM, (h + 1) * RET_V_DIM)
            q = q_ref[rows, qk_cols]
            k = k_ref[rows, qk_cols]
            v = v_ref[rows, v_cols]
            state = state_ref[h]
            scores = lax.dot_general(q, k, (((1,), (1,)), ((), ())),
                                     preferred_element_type=F32) * intra_ref[h]
            inner = jnp.dot(scores.astype(BF16), v, preferred_element_type=F32)
            qd = (q.astype(F32) * qdec_ref[h]).astype(BF16)
            cross = jnp.dot(qd, state.astype(BF16), preferred_element_type=F32)
            kd = (k.astype(F32) * kdec_ref[h]).astype(BF16)
            state_ref[h] = state * cdec_ref[h] + lax.dot_general(
                kd, v, (((0,), (0,)), ((), ())), preferred_element_type=F32)
            o = inner + cross
            o = o * lax.rsqrt(jnp.mean(o * o, axis=-1, keepdims=True) + NORM_EPS)
            gate = gate_ref[rows, v_cols].astype(F32)
            heads.append((_silu(gate) * o).astype(BF16))
        gated.append(jnp.concatenate(heads, axis=1))
    og = jnp.concatenate(gated, axis=0)
    o_ref[0] = x_ref[0] + jnp.dot(og, wout_ref[...], preferred_element_type=F32)


def _ret_core(proj, x, wout_bf, intra, qdec, kdec, cdec, side_weights):
    b, seq, _ = x.shape
    ts = RET_SEQ_TILE
    st = seq // ts
    steps = b * st
    whole = lambda a: pl.BlockSpec(a.shape, lambda bi, s: (0,) * a.ndim)

    def slab_rows(wt):
        rows = wt.shape[0] * wt.shape[1]
        per_step = rows // steps
        ok = (rows % steps == 0 and per_step % BF16_SUBLANES == 0 and wt.shape[1] % per_step == 0
              and per_step * wt.shape[2] * 4 <= SIDE_SLAB_BYTES)
        return per_step if ok else None

    hosted = all(slab_rows(wt) is not None for wt in side_weights)
    side = tuple(side_weights) if hosted else ()
    side_specs = []
    for wt in side:
        per_step = slab_rows(wt)
        per_mat = wt.shape[1] // per_step
        side_specs.append(pl.BlockSpec(
            (1, per_step, wt.shape[2]),
            lambda bi, s, per_mat=per_mat: ((bi * st + s) // per_mat, (bi * st + s) % per_mat, 0)))

    outs = pl.pallas_call(
        _ret_core_kernel,
        grid=(b, st),
        in_specs=[
            pl.BlockSpec((ts, RET_QK_TOTAL), lambda bi, s: (bi * st + s, 0)),
            pl.BlockSpec((ts, RET_QK_TOTAL), lambda bi, s: (bi * st + s, 1)),
            pl.BlockSpec((ts, RET_V_TOTAL), lambda bi, s: (bi * st + s, 1)),
            pl.BlockSpec((ts, RET_V_TOTAL), lambda bi, s: (bi * st + s, 2)),
            pl.BlockSpec((1, ts, D_MODEL), lambda bi, s: (bi, s, 0)),
            pl.BlockSpec((RET_V_TOTAL, D_MODEL), lambda bi, s: (0, 0), pipeline_mode=pl.Buffered(1)),
            whole(intra), whole(qdec), whole(kdec), whole(cdec),
        ] + side_specs,
        out_specs=[pl.BlockSpec((1, ts, D_MODEL), lambda bi, s: (bi, s, 0))] + side_specs,
        out_shape=[jax.ShapeDtypeStruct(x.shape, F32)]
                  + [jax.ShapeDtypeStruct(wt.shape, BF16) for wt in side],
        scratch_shapes=[pltpu.VMEM((RET_HEADS, RET_QK_DIM, RET_V_DIM), F32)],
        compiler_params=_params("arbitrary", "arbitrary"),
        name="ret_core",
    )(proj, proj, proj, proj, x, wout_bf, intra, qdec, kdec, cdec, *side)
    if hosted:
        return outs[0], tuple(outs[1:])
    return outs[0], tuple(wt.astype(BF16) for wt in side_weights)


def _swiglu(x_bf, wg_ref, wu_ref, wd_ref):
    out = None
    for c in range(D_FF // FF_TILE):
        cols = slice(c * FF_TILE, (c + 1) * FF_TILE)
        hg = jnp.dot(x_bf, wg_ref[:, cols], preferred_element_type=F32)
        hu = jnp.dot(x_bf, wu_ref[:, cols], preferred_element_type=F32)
        hid = (_silu(hg) * hu).astype(BF16)
        part = jnp.dot(hid, wd_ref[cols, :], preferred_element_type=F32)
        out = part if out is None else out + part
    return out


def _ffn_kernel(x_ref, g_ref, wg_ref, wu_ref, wd_ref, o_ref):
    for r in range(TOKEN_TILE // ROW_SUB):
        rows = pl.ds(r * ROW_SUB, ROW_SUB)
        x = x_ref[rows, :]
        xn = _rmsnorm_f32(x, g_ref[...]).astype(BF16)
        o_ref[rows, :] = x + _swiglu(xn, wg_ref, wu_ref, wd_ref)


def _ffn(x2, g, wg_bf, wu_bf, wd_bf):
    t = x2.shape[0]
    resident = lambda shape: pl.BlockSpec(shape, lambda i: (0, 0), pipeline_mode=pl.Buffered(1))
    return pl.pallas_call(
        _ffn_kernel,
        grid=(t // TOKEN_TILE,),
        in_specs=[
            pl.BlockSpec((TOKEN_TILE, D_MODEL), lambda i: (i, 0)),
            pl.BlockSpec((1, D_MODEL), lambda i: (0, 0)),
            resident((D_MODEL, D_FF)), resident((D_MODEL, D_FF)), resident((D_FF, D_MODEL)),
        ],
        out_specs=pl.BlockSpec((TOKEN_TILE, D_MODEL), lambda i: (i, 0)),
        out_shape=jax.ShapeDtypeStruct(x2.shape, F32),
        compiler_params=_params("parallel"),
        name="ffn_swiglu",
    )(x2, g, wg_bf, wu_bf, wd_bf)


def _split_bf16(x):
    hi = x.astype(BF16)
    return hi, (x - hi.astype(F32)).astype(BF16)


def _route_normalise(h, g_ref, xn_ref):
    xn = _rmsnorm_f32(h, g_ref[...])
    xn_ref[...] = _pack_bf16_pair(xn)
    return _split_bf16(xn)


def _route_select(x_hi, x_lo, wrt_ref, before_ref, count_ref,
                  sel_e_ref, sel_r_ref, sel_g_ref, cend_ref):
    w_hi, w_lo = _split_bf16(wrt_ref[...])
    nt = lambda a, b: lax.dot_general(a, b, (((1,), (1,)), ((), ())), preferred_element_type=F32)
    both = nt(jnp.concatenate([w_hi, w_lo], axis=0), x_hi)
    logits = both[:N_EXPERTS] + (both[N_EXPERTS:] + nt(w_hi, x_lo))
    sub = lax.broadcasted_iota(jnp.int32, logits.shape, 0).astype(F32)
    none = float(N_EXPERTS)
    m1 = jnp.max(logits, axis=0, keepdims=True)
    i1 = jnp.min(jnp.where(logits == m1, sub, none), axis=0, keepdims=True)
    rest = jnp.where(sub == i1, -jnp.inf, logits)
    m2 = jnp.max(rest, axis=0, keepdims=True)
    i2 = jnp.min(jnp.where(rest == m2, sub, none), axis=0, keepdims=True)
    e2 = jnp.exp(m2 - m1)
    denom = 1.0 + e2

    chosen = jnp.logical_or(sub == i1, sub == i2)
    prefix = jnp.dot(jnp.where(chosen, 1.0, 0.0).astype(BF16), before_ref[...],
                     preferred_element_type=F32)
    count = count_ref[...]
    rank = prefix + count[:, :1]
    r1 = jnp.sum(jnp.where(sub == i1, rank, 0.0), axis=0, keepdims=True)
    r2 = jnp.sum(jnp.where(sub == i2, rank, 0.0), axis=0, keepdims=True)
    sel_e_ref[0:1, :] = i1.astype(I32)
    sel_e_ref[1:2, :] = i2.astype(I32)
    sel_r_ref[0:1, :] = r1.astype(I32)
    sel_r_ref[1:2, :] = r2.astype(I32)
    sel_g_ref[0:1, :] = 1.0 / denom
    sel_g_ref[1:2, :] = e2 / denom
    count = count + jnp.sum(jnp.where(chosen, 1.0, 0.0), axis=1, keepdims=True)
    count_ref[...] = count
    cend_ref[0] = count


def _conv_route_kernel(x_ref, g_ref, win_ref, wc_ref, wout_ref, g2_ref, wrt_ref, before_ref,
                       o_ref, xn_ref, sel_e_ref, sel_r_ref, sel_g_ref, cend_ref,
                       carry_ref, count_ref):
    s = pl.program_id(1)
    ts = CONV_SEQ_TILE

    @pl.when(s == 0)
    def _():
        carry_ref[...] = jnp.zeros_like(carry_ref)

    @pl.when(jnp.logical_and(pl.program_id(0) == 0, s == 0))
    def _():
        count_ref[...] = jnp.zeros_like(count_ref)

    x = x_ref[0]
    xn = _rmsnorm_f32(x, g_ref[...]).astype(BF16)
    proj = jnp.dot(xn, win_ref[...], preferred_element_type=F32)
    gate_b = proj[:, :D_MODEL]
    gate_c = proj[:, D_MODEL:2 * D_MODEL]
    hid = proj[:, 2 * D_MODEL:]
    u = gate_c * hid

    row = lax.broadcasted_iota(jnp.int32, (ts, D_MODEL), 0)
    prev1 = carry_ref[7:8, :]
    prev2 = carry_ref[6:7, :]
    u1 = jnp.where(row == 0, prev1, pltpu.roll(u, 1, axis=0))
    u2 = jnp.where(row == 0, prev2, jnp.where(row == 1, prev1, pltpu.roll(u, 2, axis=0)))
    carry_ref[...] = u[ts - 8:, :]

    wc = wc_ref[...]
    y = wc[2:3, :] * u + wc[1:2, :] * u1 + wc[0:1, :] * u2
    y = (gate_b * y).astype(BF16)
    h = x + jnp.dot(y, wout_ref[...], preferred_element_type=F32)
    o_ref[0] = h
    x_hi, x_lo = _route_normalise(h, g2_ref, xn_ref)
    _route_select(x_hi, x_lo, wrt_ref, before_ref, count_ref,
                  sel_e_ref, sel_r_ref, sel_g_ref, cend_ref)


def _conv_route(x, part, g, win_bf, wc, wout_bf, g2, wrt):
    b, seq, _ = x.shape
    ts = CONV_SEQ_TILE
    w = ROUTE_WINDOW
    assert ts == w and b % MOE_PARTS == 0
    bp = b // MOE_PARTS
    b0 = part * bp
    st = seq // ts
    steps = bp * st
    tp = bp * seq
    flat = lambda bi, s: bi * st + s
    resident = lambda shape: pl.BlockSpec(shape, lambda bi, s: (0, 0), pipeline_mode=pl.Buffered(1))
    before = (jnp.arange(w)[:, None] < jnp.arange(w)[None, :]).astype(BF16)

    per_token = lambda dtype: jax.ShapeDtypeStruct((TOP_K, tp), dtype)
    return pl.pallas_call(
        _conv_route_kernel,
        grid=(bp, st),
        in_specs=[
            pl.BlockSpec((1, ts, D_MODEL), lambda bi, s: (b0 + bi, s, 0)),
            pl.BlockSpec((1, D_MODEL), lambda bi, s: (0, 0)),
            resident((D_MODEL, 3 * D_MODEL)),
            pl.BlockSpec((CONV_WIDTH, D_MODEL), lambda bi, s: (0, 0)),
            resident((D_MODEL, D_MODEL)),
            pl.BlockSpec((1, D_MODEL), lambda bi, s: (0, 0)),
            pl.BlockSpec((N_EXPERTS, D_MODEL), lambda bi, s: (0, 0)),
            resident((w, w)),
        ],
        out_specs=[
            pl.BlockSpec((1, ts, D_MODEL), lambda bi, s: (bi, s, 0)),
            pl.BlockSpec((w, PACKED), lambda bi, s: (flat(bi, s), 0)),
            pl.BlockSpec((TOP_K, w), lambda bi, s: (0, flat(bi, s))),
            pl.BlockSpec((TOP_K, w), lambda bi, s: (0, flat(bi, s))),
            pl.BlockSpec((TOP_K, w), lambda bi, s: (0, flat(bi, s))),
            pl.BlockSpec((1, N_EXPERTS, LANES), lambda bi, s: (flat(bi, s), 0, 0)),
        ],
        out_shape=[jax.ShapeDtypeStruct((bp, seq, D_MODEL), F32),
                   jax.ShapeDtypeStruct((tp, PACKED), I32),
                   per_token(I32), per_token(I32), per_token(F32),
                   jax.ShapeDtypeStruct((steps, N_EXPERTS, LANES), F32)],
        scratch_shapes=[pltpu.VMEM((8, D_MODEL), F32), pltpu.VMEM((N_EXPERTS, LANES), F32)],
        compiler_params=_params("arbitrary", "arbitrary"),
        name="conv_route",
    )(x, g, win_bf, wc, wout_bf, g2, wrt, before)


def _routing_plan(cend, sel_e, sel_r, t):
    counts = cend[-1, :, 0].astype(I32)
    ntiles = (counts + EXPERT_TILE - 1) // EXPERT_TILE
    tile_end = jnp.cumsum(ntiles)
    tile_start = tile_end - ntiles
    off = tile_start * EXPERT_TILE

    nt_max = TOP_K * t // EXPERT_TILE + N_EXPERTS
    ti = jnp.arange(nt_max, dtype=I32)
    tile_e = jnp.minimum(jnp.sum(tile_end[None, :] <= ti[:, None], axis=1).astype(I32),
                         N_EXPERTS - 1)
    tile_rows = jnp.clip(counts[tile_e] - (ti - tile_start[tile_e]) * EXPERT_TILE, 0, EXPERT_TILE)
    tile_rows = jnp.where(ti < tile_end[-1], tile_rows, 0)

    first_row = jnp.zeros_like(sel_e)
    for e in range(N_EXPERTS):
        first_row = first_row + jnp.where(sel_e == e, off[e], 0)
    dest = first_row + sel_r
    return tile_e, tile_rows, dest


def _sc_mesh():
    return plsc.VectorSubcoreMesh(core_axis_name="c", subcore_axis_name="s",
                                  num_cores=SC_CORES, num_subcores=SC_SUBCORES)


def _dispatch_rows(xn, dest, n_rows):
    t, d = xn.shape
    per_worker = t // SC_WORKERS
    chunks = per_worker // SC_ROWS
    dest4 = dest.reshape(TOP_K, SC_WORKERS, chunks, SC_ROWS)

    @functools.partial(
        pl.kernel, mesh=_sc_mesh(),
        out_type=jax.ShapeDtypeStruct((n_rows, d), xn.dtype),
        scratch_types=[pltpu.VMEM((TOP_K, chunks, SC_ROWS), I32),
                       pltpu.VMEM((SC_ROWS, d), xn.dtype)],
        name="moe_dispatch",
    )
    def run(x_hbm, dest_hbm, o_hbm, dest_v, rows_v):
        worker = lax.axis_index("s") * SC_CORES + lax.axis_index("c")
        for k in range(TOP_K):
            pltpu.sync_copy(dest_hbm.at[k, worker], dest_v.at[k])

        @pl.loop(0, chunks)
        def _(j):
            pltpu.sync_copy(x_hbm.at[pl.ds(worker * per_worker + j * SC_ROWS, SC_ROWS)], rows_v)
            for k in range(TOP_K):
                pltpu.sync_copy(rows_v, o_hbm.at[dest_v.at[k, j]])

    return run(xn, dest4)


def _collect_rows(y, dest):
    t = dest.shape[1]
    d = y.shape[1]
    per_worker = t // SC_WORKERS
    chunks = per_worker // SC_ROWS
    dest4 = dest.reshape(TOP_K, SC_WORKERS, chunks, SC_ROWS)

    @functools.partial(
        pl.kernel, mesh=_sc_mesh(),
        out_type=jax.ShapeDtypeStruct((TOP_K, t, d), y.dtype),
        scratch_types=[pltpu.VMEM((TOP_K, chunks, SC_ROWS), I32),
                       pltpu.VMEM((SC_ROWS, d), y.dtype)],
        name="moe_collect",
    )
    def run(y_hbm, dest_hbm, o_hbm, dest_v, rows_v):
        worker = lax.axis_index("s") * SC_CORES + lax.axis_index("c")
        for k in range(TOP_K):
            pltpu.sync_copy(dest_hbm.at[k, worker], dest_v.at[k])

        @pl.loop(0, chunks)
        def _(j):
            for k in range(TOP_K):
                pltpu.sync_copy(y_hbm.at[dest_v.at[k, j]], rows_v)
                pltpu.sync_copy(rows_v, o_hbm.at[k, pl.ds(worker * per_worker + j * SC_ROWS, SC_ROWS)])

    return run(y, dest4)


def _expert_kernel(tile_e_s, tile_rows_s, xs_ref, wg_ref, wu_ref, wd_ref, y_ref):
    n_rows = tile_rows_s[pl.program_id(0)]
    n_sub = EXPERT_TILE // ROW_SUB

    def run(sub_blocks):
        for r in range(sub_blocks):
            rows = pl.ds(r * ROW_SUB, ROW_SUB)
            real = lax.broadcasted_iota(jnp.int32, (ROW_SUB, PACKED), 0) < n_rows - r * ROW_SUB
            xs = _unpack_bf16_pair(jnp.where(real, xs_ref[rows, :], 0)).astype(BF16)
            y_ref[rows, :] = _pack_bf16_pair(_swiglu(xs, wg_ref.at[0], wu_ref.at[0], wd_ref.at[0]))
        if sub_blocks < n_sub:
            rest = pl.ds(sub_blocks * ROW_SUB, (n_sub - sub_blocks) * ROW_SUB)
            y_ref[rest, :] = jnp.zeros(((n_sub - sub_blocks) * ROW_SUB, PACKED), I32)

    for used in range(n_sub + 1):
        lo, hi = (used - 1) * ROW_SUB, used * ROW_SUB
        pl.when(jnp.logical_and(n_rows > lo, n_rows <= hi))(functools.partial(run, used))


def _experts(tile_e, tile_rows, xs, wg_bf, wu_bf, wd_bf):
    n_rows = xs.shape[0]
    weights = lambda shape: pl.BlockSpec((1,) + shape, lambda i, te, tr: (te[i], 0, 0),
                                         pipeline_mode=pl.Buffered(1))
    grid_spec = pltpu.PrefetchScalarGridSpec(
        num_scalar_prefetch=2,
        grid=(n_rows // EXPERT_TILE,),
        in_specs=[
            pl.BlockSpec((EXPERT_TILE, PACKED), lambda i, te, tr: (i, 0)),
            weights((D_MODEL, D_FF)), weights((D_MODEL, D_FF)), weights((D_FF, D_MODEL)),
        ],
        out_specs=pl.BlockSpec((EXPERT_TILE, PACKED), lambda i, te, tr: (i, 0)),
    )
    return pl.pallas_call(
        _expert_kernel,
        grid_spec=grid_spec,
        out_shape=jax.ShapeDtypeStruct((n_rows, PACKED), I32),
        compiler_params=_params("arbitrary"),
        name="moe_experts",
    )(tile_e, tile_rows, xs, wg_bf, wu_bf, wd_bf)


def _final_kernel(gf_ref, *rest):
    n = MOE_PARTS
    x_refs, y_refs, sel_g_refs, o_ref = rest[:n], rest[n:2 * n], rest[2 * n:3 * n], rest[-1]
    w = ROUTE_WINDOW
    part = pl.program_id(0) // (pl.num_programs(0) // n)

    def own(refs, index):
        value = refs[0][index]
        for p in range(1, n):
            value = jnp.where(part == p, refs[p][index], value)
        return value

    diag = (lax.broadcasted_iota(jnp.int32, (w, w), 0)
            == lax.broadcasted_iota(jnp.int32, (w, w), 1))
    acc = own(x_refs, (slice(None), slice(None)))
    for k in range(TOP_K):
        sel_g = own(sel_g_refs, (slice(k, k + 1), slice(None)))
        gate = jnp.sum(jnp.where(diag, sel_g, 0.0), axis=1, keepdims=True)
        acc = acc + gate * _unpack_bf16_pair(own(y_refs, k))
    o_ref[...] = _rmsnorm_f32(acc, gf_ref[...])


def _final(x_parts, y_parts, sel_g_parts, g_final):
    tp = x_parts[0].shape[0]
    w = ROUTE_WINDOW
    per_part = tp // w

    def window(p):
        return lambda i: jnp.clip(i - p * per_part, 0, per_part - 1)

    parts = range(MOE_PARTS)
    x_specs = [pl.BlockSpec((w, D_MODEL), lambda i, f=window(p): (f(i), 0)) for p in parts]
    y_specs = [pl.BlockSpec((TOP_K, w, PACKED), lambda i, f=window(p): (0, f(i), 0)) for p in parts]
    g_specs = [pl.BlockSpec((TOP_K, w), lambda i, f=window(p): (0, f(i))) for p in parts]
    return pl.pallas_call(
        _final_kernel,
        grid=(MOE_PARTS * per_part,),
        in_specs=[pl.BlockSpec((1, D_MODEL), lambda i: (0, 0))] + x_specs + y_specs + g_specs,
        out_specs=pl.BlockSpec((w, D_MODEL), lambda i: (i, 0)),
        out_shape=jax.ShapeDtypeStruct((MOE_PARTS * tp, D_MODEL), F32),
        compiler_params=_params("parallel"),
        name="moe_final",
    )(g_final, *x_parts, *y_parts, *sel_g_parts)


def _rotary_tables(seq):
    inv_freq = ROPE_BASE ** (-jnp.arange(0, RET_QK_DIM, 2, dtype=F32) / RET_QK_DIM)
    ang = jnp.arange(seq, dtype=F32)[:, None] * inv_freq[None, :]
    return jnp.cos(ang), jnp.sin(ang)


def _decay_tables():
    c = RET_CHUNK
    log_gamma = jnp.log(1.0 - 2.0 ** (-5.0 - jnp.arange(RET_HEADS, dtype=F32)))
    idx = jnp.arange(c, dtype=F32)
    diff = idx[:, None] - idx[None, :]
    causal = diff >= 0
    intra = jnp.where(causal[None],
                      jnp.exp(jnp.where(causal, diff, 0.0)[None] * log_gamma[:, None, None]), 0.0)
    qdec = jnp.exp((idx[None, :] + 1.0) * log_gamma[:, None])[..., None]
    kdec = jnp.exp((c - 1.0 - idx[None, :]) * log_gamma[:, None])[..., None]
    cdec = jnp.exp(c * log_gamma)[:, None, None]
    return intra, qdec, kdec, cdec


def kernel(x, norm_mix0, ret_w_in, ret_w_out, norm_ffn0, ffn_w_gate, ffn_w_up, ffn_w_down,
           norm_mix1, conv_w_in, conv_w, conv_w_out, norm_ffn1, moe_router, moe_w_gate,
           moe_w_up, moe_w_down, norm_final):
    b, seq, d = x.shape
    t = b * seq
    row = lambda g: g.reshape(1, d).astype(F32)
    bf = lambda w: w.astype(BF16)

    cos, sin = _rotary_tables(seq)
    intra, qdec, kdec, cdec = _decay_tables()

    proj = _ret_in(x.reshape(t, d), row(norm_mix0), bf(ret_w_in), cos, sin, seq)
    h, expert_w = _ret_core(proj, x, bf(ret_w_out), intra, qdec, kdec, cdec,
                            (moe_w_gate, moe_w_up, moe_w_down))
    h = _ffn(h.reshape(t, d), row(norm_ffn0), bf(ffn_w_gate), bf(ffn_w_up), bf(ffn_w_down))
    h = h.reshape(b, seq, d)

    conv_args = (row(norm_mix1), bf(conv_w_in), conv_w.astype(F32), bf(conv_w_out),
                 row(norm_ffn1), moe_router.astype(F32).T)
    routed = [_conv_route(h, part, *conv_args) for part in range(MOE_PARTS)]

    tp = t // MOE_PARTS
    n_rows = (TOP_K * tp // EXPERT_TILE + N_EXPERTS) * EXPERT_TILE
    x_parts, y_parts, gate_parts = [], [], []
    for h_p, xn, sel_e, sel_r, sel_g, cend in routed:
        tile_e, tile_rows, dest = _routing_plan(cend, sel_e, sel_r, tp)
        xs = _dispatch_rows(xn, dest, n_rows)
        y = _experts(tile_e, tile_rows, xs, *expert_w)
        x_parts.append(h_p.reshape(tp, d))
        y_parts.append(_collect_rows(y, dest))
        gate_parts.append(sel_g)
    out = _final(x_parts, y_parts, gate_parts, row(norm_final))
    return out.reshape(b, seq, d)
```

```python
import functools

import jax
import jax.numpy as jnp
from jax import lax
from jax.experimental import pallas as pl
from jax.experimental.pallas import tpu as pltpu
from jax.experimental.pallas import tpu_sc as plsc

D_MODEL = 1024
RET_HEADS = 4
RET_QK_DIM = D_MODEL // RET_HEADS
RET_V_DIM = 2 * RET_QK_DIM
RET_QK_TOTAL = RET_HEADS * RET_QK_DIM
RET_V_TOTAL = RET_HEADS * RET_V_DIM
RET_IN_COLS = 2 * RET_QK_TOTAL + 2 * RET_V_TOTAL
ROPE_BASE = 10000.0
CONV_WIDTH = 3
D_FF = 7 * D_MODEL // 2
N_EXPERTS = 8
TOP_K = 2
NORM_EPS = 1e-6

RET_CHUNK = 256
HALF = RET_QK_DIM // 2

VMEM_LIMIT_BYTES = 56 * 1024 * 1024
SC_CORES = 2
SC_SUBCORES = 16
SC_WORKERS = SC_CORES * SC_SUBCORES
SC_ROWS = 64

TOKEN_TILE = 1024
RET_SEQ_TILE = 512
CONV_SEQ_TILE = 512
FF_TILE = 1792
ROW_SUB = 512
IN_TOKEN_TILE = 512
IN_COL_TILE = 1024
ROUTE_WINDOW = 512
EXPERT_TILE = 1024
LAST_PART_DIVISOR = 4
LANES = 128
BF16_SUBLANES = 16
SIDE_SLAB_BYTES = 2 * 1024 * 1024
PACKED = D_MODEL // 2

BF16 = jnp.bfloat16
F32 = jnp.float32
I32 = jnp.int32


def _params(*semantics):
    return pltpu.CompilerParams(dimension_semantics=semantics,
                                vmem_limit_bytes=VMEM_LIMIT_BYTES)


def _rmsnorm_f32(x, g):
    return x * lax.rsqrt(jnp.mean(x * x, axis=-1, keepdims=True) + NORM_EPS) * g


def _silu(x):
    return x * (1.0 / (1.0 + jnp.exp(-x)))


def _pack_bf16_pair(x):
    half = x.shape[1] // 2
    hi = lax.bitcast_convert_type(x[:, :half].astype(BF16).astype(F32), I32)
    lo = lax.bitcast_convert_type(x[:, half:].astype(BF16).astype(F32), I32)
    return hi | lax.shift_right_logical(lo, 16)


def _unpack_bf16_pair(w):
    hi = lax.bitcast_convert_type(w & jnp.int32(-65536), F32)
    lo = lax.bitcast_convert_type(lax.shift_left(w, 16), F32)
    return jnp.concatenate([hi, lo], axis=1)


def _ret_in_kernel(x_ref, g_ref, w_ref, cos_ref, sin_ref, o_ref):
    xn = _rmsnorm_f32(x_ref[...], g_ref[...]).astype(BF16)
    cos = cos_ref[...]
    sin = sin_ref[...]
    for n in range(RET_IN_COLS // IN_COL_TILE):
        cols = slice(n * IN_COL_TILE, (n + 1) * IN_COL_TILE)
        acc = jnp.dot(xn, w_ref[:, cols], preferred_element_type=F32)
        if n * IN_COL_TILE >= 2 * RET_QK_TOTAL:
            o_ref[:, cols] = acc.astype(BF16)
            continue
        scale = 1.0 if n * IN_COL_TILE < RET_QK_TOTAL else RET_QK_DIM ** -0.5
        for h in range(IN_COL_TILE // RET_QK_DIM):
            lo = h * RET_QK_DIM
            t1 = acc[:, lo:lo + HALF]
            t2 = acc[:, lo + HALF:lo + RET_QK_DIM]
            base = n * IN_COL_TILE + lo
            o_ref[:, base:base + HALF] = ((t1 * cos - t2 * sin) * scale).astype(BF16)
            o_ref[:, base + HALF:base + RET_QK_DIM] = ((t1 * sin + t2 * cos) * scale).astype(BF16)


def _ret_in(x2, g, w_bf, cos, sin, seq):
    t = x2.shape[0]
    tm = IN_TOKEN_TILE
    seq_tiles = seq // tm
    return pl.pallas_call(
        _ret_in_kernel,
        grid=(t // tm,),
        in_specs=[
            pl.BlockSpec((tm, D_MODEL), lambda i: (i, 0)),
            pl.BlockSpec((1, D_MODEL), lambda i: (0, 0)),
            pl.BlockSpec((D_MODEL, RET_IN_COLS), lambda i: (0, 0), pipeline_mode=pl.Buffered(1)),
            pl.BlockSpec((tm, HALF), lambda i: (i % seq_tiles, 0)),
            pl.BlockSpec((tm, HALF), lambda i: (i % seq_tiles, 0)),
        ],
        out_specs=pl.BlockSpec((tm, RET_IN_COLS), lambda i: (i, 0)),
        out_shape=jax.ShapeDtypeStruct((t, RET_IN_COLS), BF16),
        compiler_params=_params("parallel"),
        name="ret_in_proj",
    )(x2, g, w_bf, cos, sin)


def _ret_core_kernel(q_ref, k_ref, v_ref, gate_ref, x_ref, wout_ref,
                     intra_ref, qdec_ref, kdec_ref, cdec_ref, *rest):
    n_side = (len(rest) - 2) // 2
    side_in, o_ref, side_out, state_ref = (rest[:n_side], rest[n_side],
                                           rest[n_side + 1:2 * n_side + 1], rest[-1])

    for src, dst in zip(side_in, side_out):
        dst[...] = src[...].astype(BF16)

    @pl.when(pl.program_id(1) == 0)
    def _():
        state_ref[...] = jnp.zeros_like(state_ref)

    gated = []
    for c in range(RET_SEQ_TILE // RET_CHUNK):
        rows = pl.ds(c * RET_CHUNK, RET_CHUNK)
        heads = []
        for h in range(RET_HEADS):
            qk_cols = slice(h * RET_QK_DIM, (h + 1) * RET_QK_DIM)
            v_cols = slice(h * RET_V_DIM, (h + 1) * RET_V_DIM)
            q = q_ref[rows, qk_cols]
            k = k_ref[rows, qk_cols]
            v = v_ref[rows, v_cols]
            state = state_ref[h]
            scores = lax.dot_general(q, k, (((1,), (1,)), ((), ())),
                                     preferred_element_type=F32) * intra_ref[h]
            inner = jnp.dot(scores.astype(BF16), v, preferred_element_type=F32)
            qd = (q.astype(F32) * qdec_ref[h]).astype(BF16)
            cross = jnp.dot(qd, state.astype(BF16), preferred_element_type=F32)
            kd = (k.astype(F32) * kdec_ref[h]).astype(BF16)
            state_ref[h] = state * cdec_ref[h] + lax.dot_general(
                kd, v, (((0,), (0,)), ((), ())), preferred_element_type=F32)
            o = inner + cross
            o = o * lax.rsqrt(jnp.mean(o * o, axis=-1, keepdims=True) + NORM_EPS)
            gate = gate_ref[rows, v_cols].astype(F32)
            heads.append((_silu(gate) * o).astype(BF16))
        gated.append(jnp.concatenate(heads, axis=1))
    og = jnp.concatenate(gated, axis=0)
    o_ref[0] = x_ref[0] + jnp.dot(og, wout_ref[...], preferred_element_type=F32)


def _ret_core(proj, x, wout_bf, intra, qdec, kdec, cdec, side_weights):
    b, seq, _ = x.shape
    ts = RET_SEQ_TILE
    st = seq // ts
    steps = b * st
    whole = lambda a: pl.BlockSpec(a.shape, lambda bi, s: (0,) * a.ndim)

    def slab_rows(wt):
        rows = wt.shape[0] * wt.shape[1]
        per_step = rows // steps
        ok = (rows % steps == 0 and per_step % BF16_SUBLANES == 0 and wt.shape[1] % per_step == 0
              and per_step * wt.shape[2] * 4 <= SIDE_SLAB_BYTES)
        return per_step if ok else None

    hosted = all(slab_rows(wt) is not None for wt in side_weights)
    side = tuple(side_weights) if hosted else ()
    side_specs = []
    for wt in side:
        per_step = slab_rows(wt)
        per_mat = wt.shape[1] // per_step
        side_specs.append(pl.BlockSpec(
            (1, per_step, wt.shape[2]),
            lambda bi, s, per_mat=per_mat: ((bi * st + s) // per_mat, (bi * st + s) % per_mat, 0)))

    outs = pl.pallas_call(
        _ret_core_kernel,
        grid=(b, st),
        in_specs=[
            pl.BlockSpec((ts, RET_QK_TOTAL), lambda bi, s: (bi * st + s, 0)),
            pl.BlockSpec((ts, RET_QK_TOTAL), lambda bi, s: (bi * st + s, 1)),
            pl.BlockSpec((ts, RET_V_TOTAL), lambda bi, s: (bi * st + s, 1)),
            pl.BlockSpec((ts, RET_V_TOTAL), lambda bi, s: (bi * st + s, 2)),
            pl.BlockSpec((1, ts, D_MODEL), lambda bi, s: (bi, s, 0)),
            pl.BlockSpec((RET_V_TOTAL, D_MODEL), lambda bi, s: (0, 0), pipeline_mode=pl.Buffered(1)),
            whole(intra), whole(qdec), whole(kdec), whole(cdec),
        ] + side_specs,
        out_specs=[pl.BlockSpec((1, ts, D_MODEL), lambda bi, s: (bi, s, 0))] + side_specs,
        out_shape=[jax.ShapeDtypeStruct(x.shape, F32)]
                  + [jax.ShapeDtypeStruct(wt.shape, BF16) for wt in side],
        scratch_shapes=[pltpu.VMEM((RET_HEADS, RET_QK_DIM, RET_V_DIM), F32)],
        compiler_params=_params("arbitrary", "arbitrary"),
        name="ret_core",
    )(proj, proj, proj, proj, x, wout_bf, intra, qdec, kdec, cdec, *side)
    if hosted:
        return outs[0], tuple(outs[1:])
    return outs[0], tuple(wt.astype(BF16) for wt in side_weights)


def _swiglu(x_bf, wg_ref, wu_ref, wd_ref):
    out = None
    for c in range(D_FF // FF_TILE):
        cols = slice(c * FF_TILE, (c + 1) * FF_TILE)
        hg = jnp.dot(x_bf, wg_ref[:, cols], preferred_element_type=F32)
        hu = jnp.dot(x_bf, wu_ref[:, cols], preferred_element_type=F32)
        hid = (_silu(hg) * hu).astype(BF16)
        part = jnp.dot(hid, wd_ref[cols, :], preferred_element_type=F32)
        out = part if out is None else out + part
    return out


def _ffn_kernel(x_ref, g_ref, wg_ref, wu_ref, wd_ref, o_ref):
    for r in range(TOKEN_TILE // ROW_SUB):
        rows = pl.ds(r * ROW_SUB, ROW_SUB)
        x = x_ref[rows, :]
        xn = _rmsnorm_f32(x, g_ref[...]).astype(BF16)
        o_ref[rows, :] = x + _swiglu(xn, wg_ref, wu_ref, wd_ref)


def _ffn(x2, g, wg_bf, wu_bf, wd_bf):
    t = x2.shape[0]
    resident = lambda shape: pl.BlockSpec(shape, lambda i: (0, 0), pipeline_mode=pl.Buffered(1))
    return pl.pallas_call(
        _ffn_kernel,
        grid=(t // TOKEN_TILE,),
        in_specs=[
            pl.BlockSpec((TOKEN_TILE, D_MODEL), lambda i: (i, 0)),
            pl.BlockSpec((1, D_MODEL), lambda i: (0, 0)),
            resident((D_MODEL, D_FF)), resident((D_MODEL, D_FF)), resident((D_FF, D_MODEL)),
        ],
        out_specs=pl.BlockSpec((TOKEN_TILE, D_MODEL), lambda i: (i, 0)),
        out_shape=jax.ShapeDtypeStruct(x2.shape, F32),
        compiler_params=_params("parallel"),
        name="ffn_swiglu",
    )(x2, g, wg_bf, wu_bf, wd_bf)


def _split_bf16(x):
    hi = x.astype(BF16)
    return hi, (x - hi.astype(F32)).astype(BF16)


def _route_normalise(h, g_ref, xn_ref):
    xn = _rmsnorm_f32(h, g_ref[...])
    xn_ref[...] = _pack_bf16_pair(xn)
    return _split_bf16(xn)


def _route_select(x_hi, x_lo, wrt_ref, before_ref, count_ref,
                  sel_e_ref, sel_r_ref, sel_g_ref, cend_ref):
    w_hi, w_lo = _split_bf16(wrt_ref[...])
    nt = lambda a, b: lax.dot_general(a, b, (((1,), (1,)), ((), ())), preferred_element_type=F32)
    both = nt(jnp.concatenate([w_hi, w_lo], axis=0), x_hi)
    logits = both[:N_EXPERTS] + (both[N_EXPERTS:] + nt(w_hi, x_lo))
    sub = lax.broadcasted_iota(jnp.int32, logits.shape, 0).astype(F32)
    none = float(N_EXPERTS)
    m1 = jnp.max(logits, axis=0, keepdims=True)
    i1 = jnp.min(jnp.where(logits == m1, sub, none), axis=0, keepdims=True)
    rest = jnp.where(sub == i1, -jnp.inf, logits)
    m2 = jnp.max(rest, axis=0, keepdims=True)
    i2 = jnp.min(jnp.where(rest == m2, sub, none), axis=0, keepdims=True)
    e2 = jnp.exp(m2 - m1)
    denom = 1.0 + e2

    chosen = jnp.logical_or(sub == i1, sub == i2)
    prefix = jnp.dot(jnp.where(chosen, 1.0, 0.0).astype(BF16), before_ref[...],
                     preferred_element_type=F32)
    count = count_ref[...]
    rank = prefix + count[:, :1]
    r1 = jnp.sum(jnp.where(sub == i1, rank, 0.0), axis=0, keepdims=True)
    r2 = jnp.sum(jnp.where(sub == i2, rank, 0.0), axis=0, keepdims=True)
    sel_e_ref[0:1, :] = i1.astype(I32)
    sel_e_ref[1:2, :] = i2.astype(I32)
    sel_r_ref[0:1, :] = r1.astype(I32)
    sel_r_ref[1:2, :] = r2.astype(I32)
    sel_g_ref[0:1, :] = 1.0 / denom
    sel_g_ref[1:2, :] = e2 / denom
    count = count + jnp.sum(jnp.where(chosen, 1.0, 0.0), axis=1, keepdims=True)
    count_ref[...] = count
    cend_ref[0] = count


def _conv_route_kernel(x_ref, g_ref, win_ref, wc_ref, wout_ref, g2_ref, wrt_ref, before_ref,
                       o_ref, xn_ref, sel_e_ref, sel_r_ref, sel_g_ref, cend_ref,
                       carry_ref, count_ref):
    s = pl.program_id(1)
    ts = CONV_SEQ_TILE

    @pl.when(s == 0)
    def _():
        carry_ref[...] = jnp.zeros_like(carry_ref)

    @pl.when(jnp.logical_and(pl.program_id(0) == 0, s == 0))
    def _():
        count_ref[...] = jnp.zeros_like(count_ref)

    x = x_ref[0]
    xn = _rmsnorm_f32(x, g_ref[...]).astype(BF16)
    proj = jnp.dot(xn, win_ref[...], preferred_element_type=F32)
    gate_b = proj[:, :D_MODEL]
    gate_c = proj[:, D_MODEL:2 * D_MODEL]
    hid = proj[:, 2 * D_MODEL:]
    u = gate_c * hid

    row = lax.broadcasted_iota(jnp.int32, (ts, D_MODEL), 0)
    prev1 = carry_ref[7:8, :]
    prev2 = carry_ref[6:7, :]
    u1 = jnp.where(row == 0, prev1, pltpu.roll(u, 1, axis=0))
    u2 = jnp.where(row == 0, prev2, jnp.where(row == 1, prev1, pltpu.roll(u, 2, axis=0)))
    carry_ref[...] = u[ts - 8:, :]

    wc = wc_ref[...]
    y = wc[2:3, :] * u + wc[1:2, :] * u1 + wc[0:1, :] * u2
    y = (gate_b * y).astype(BF16)
    h = x + jnp.dot(y, wout_ref[...], preferred_element_type=F32)
    o_ref[0] = h
    x_hi, x_lo = _route_normalise(h, g2_ref, xn_ref)
    _route_select(x_hi, x_lo, wrt_ref, before_ref, count_ref,
                  sel_e_ref, sel_r_ref, sel_g_ref, cend_ref)


def _conv_route(x, b0, bp, after, g, win_bf, wc, wout_bf, g2, wrt):
    b, seq, _ = x.shape
    ts = CONV_SEQ_TILE
    w = ROUTE_WINDOW
    assert ts == w
    st = seq // ts
    steps = bp * st
    tp = bp * seq
    flat = lambda bi, s: bi * st + s
    resident = lambda shape: pl.BlockSpec(shape, lambda bi, s: (0, 0), pipeline_mode=pl.Buffered(1))
    before = (jnp.arange(w)[:, None] < jnp.arange(w)[None, :]).astype(BF16)

    per_token = lambda dtype: jax.ShapeDtypeStruct((TOP_K, tp), dtype)
    n_after = len(after)

    def body(*refs):
        n_in = 8
        _conv_route_kernel(*refs[:n_in], *refs[n_in + n_after:])

    return pl.pallas_call(
        body,
        grid=(bp, st),
        in_specs=[
            pl.BlockSpec((1, ts, D_MODEL), lambda bi, s: (b0 + bi, s, 0)),
            pl.BlockSpec((1, D_MODEL), lambda bi, s: (0, 0)),
            resident((D_MODEL, 3 * D_MODEL)),
            pl.BlockSpec((CONV_WIDTH, D_MODEL), lambda bi, s: (0, 0)),
            resident((D_MODEL, D_MODEL)),
            pl.BlockSpec((1, D_MODEL), lambda bi, s: (0, 0)),
            pl.BlockSpec((N_EXPERTS, D_MODEL), lambda bi, s: (0, 0)),
            resident((w, w)),
        ] + [pl.BlockSpec(memory_space=pl.ANY)] * n_after,
        out_specs=[
            pl.BlockSpec((1, ts, D_MODEL), lambda bi, s: (bi, s, 0)),
            pl.BlockSpec((w, PACKED), lambda bi, s: (flat(bi, s), 0)),
            pl.BlockSpec((TOP_K, w), lambda bi, s: (0, flat(bi, s))),
            pl.BlockSpec((TOP_K, w), lambda bi, s: (0, flat(bi, s))),
            pl.BlockSpec((TOP_K, w), lambda bi, s: (0, flat(bi, s))),
            pl.BlockSpec((1, N_EXPERTS, LANES), lambda bi, s: (flat(bi, s), 0, 0)),
        ],
        out_shape=[jax.ShapeDtypeStruct((bp, seq, D_MODEL), F32),
                   jax.ShapeDtypeStruct((tp, PACKED), I32),
                   per_token(I32), per_token(I32), per_token(F32),
                   jax.ShapeDtypeStruct((steps, N_EXPERTS, LANES), F32)],
        scratch_shapes=[pltpu.VMEM((8, D_MODEL), F32), pltpu.VMEM((N_EXPERTS, LANES), F32)],
        compiler_params=_params("arbitrary", "arbitrary"),
        name="conv_route",
    )(x, g, win_bf, wc, wout_bf, g2, wrt, before, *after)


def _routing_plan(cend, sel_e, sel_r, t):
    counts = cend[-1, :, 0].astype(I32)
    ntiles = (counts + EXPERT_TILE - 1) // EXPERT_TILE
    tile_end = jnp.cumsum(ntiles)
    tile_start = tile_end - ntiles
    off = tile_start * EXPERT_TILE

    nt_max = TOP_K * t // EXPERT_TILE + N_EXPERTS
    ti = jnp.arange(nt_max, dtype=I32)
    tile_e = jnp.minimum(jnp.sum(tile_end[None, :] <= ti[:, None], axis=1).astype(I32),
                         N_EXPERTS - 1)
    tile_rows = jnp.clip(counts[tile_e] - (ti - tile_start[tile_e]) * EXPERT_TILE, 0, EXPERT_TILE)
    tile_rows = jnp.where(ti < tile_end[-1], tile_rows, 0)

    first_row = jnp.zeros_like(sel_e)
    for e in range(N_EXPERTS):
        first_row = first_row + jnp.where(sel_e == e, off[e], 0)
    dest = first_row + sel_r
    return tile_e, tile_rows, dest


def _sc_mesh():
    return plsc.VectorSubcoreMesh(core_axis_name="c", subcore_axis_name="s",
                                  num_cores=SC_CORES, num_subcores=SC_SUBCORES)


def _dispatch_rows(xn, dest, n_rows):
    t, d = xn.shape
    per_worker = t // SC_WORKERS
    chunks = per_worker // SC_ROWS
    dest4 = dest.reshape(TOP_K, SC_WORKERS, chunks, SC_ROWS)

    @functools.partial(
        pl.kernel, mesh=_sc_mesh(),
        out_type=jax.ShapeDtypeStruct((n_rows, d), xn.dtype),
        scratch_types=[pltpu.VMEM((TOP_K, chunks, SC_ROWS), I32),
                       pltpu.VMEM((SC_ROWS, d), xn.dtype)],
        name="moe_dispatch",
    )
    def run(x_hbm, dest_hbm, o_hbm, dest_v, rows_v):
        worker = lax.axis_index("s") * SC_CORES + lax.axis_index("c")
        for k in range(TOP_K):
            pltpu.sync_copy(dest_hbm.at[k, worker], dest_v.at[k])

        @pl.loop(0, chunks)
        def _(j):
            pltpu.sync_copy(x_hbm.at[pl.ds(worker * per_worker + j * SC_ROWS, SC_ROWS)], rows_v)
            for k in range(TOP_K):
                pltpu.sync_copy(rows_v, o_hbm.at[dest_v.at[k, j]])

    return run(xn, dest4)


def _collect_rows(y, dest):
    t = dest.shape[1]
    d = y.shape[1]
    per_worker = t // SC_WORKERS
    chunks = per_worker // SC_ROWS
    dest4 = dest.reshape(TOP_K, SC_WORKERS, chunks, SC_ROWS)

    @functools.partial(
        pl.kernel, mesh=_sc_mesh(),
        out_type=jax.ShapeDtypeStruct((TOP_K, t, d), y.dtype),
        scratch_types=[pltpu.VMEM((TOP_K, chunks, SC_ROWS), I32),
                       pltpu.VMEM((SC_ROWS, d), y.dtype)],
        name="moe_collect",
    )
    def run(y_hbm, dest_hbm, o_hbm, dest_v, rows_v):
        worker = lax.axis_index("s") * SC_CORES + lax.axis_index("c")
        for k in range(TOP_K):
            pltpu.sync_copy(dest_hbm.at[k, worker], dest_v.at[k])

        @pl.loop(0, chunks)
        def _(j):
            for k in range(TOP_K):
                pltpu.sync_copy(y_hbm.at[dest_v.at[k, j]], rows_v)
                pltpu.sync_copy(rows_v, o_hbm.at[k, pl.ds(worker * per_worker + j * SC_ROWS, SC_ROWS)])

    return run(y, dest4)


def _expert_kernel(tile_e_s, tile_rows_s, xs_ref, wg_ref, wu_ref, wd_ref, y_ref):
    n_rows = tile_rows_s[pl.program_id(0)]
    n_sub = EXPERT_TILE // ROW_SUB

    def run(sub_blocks):
        for r in range(sub_blocks):
            rows = pl.ds(r * ROW_SUB, ROW_SUB)
            real = lax.broadcasted_iota(jnp.int32, (ROW_SUB, PACKED), 0) < n_rows - r * ROW_SUB
            xs = _unpack_bf16_pair(jnp.where(real, xs_ref[rows, :], 0)).astype(BF16)
            y_ref[rows, :] = _pack_bf16_pair(_swiglu(xs, wg_ref.at[0], wu_ref.at[0], wd_ref.at[0]))
        if sub_blocks < n_sub:
            rest = pl.ds(sub_blocks * ROW_SUB, (n_sub - sub_blocks) * ROW_SUB)
            y_ref[rest, :] = jnp.zeros(((n_sub - sub_blocks) * ROW_SUB, PACKED), I32)

    for used in range(n_sub + 1):
        lo, hi = (used - 1) * ROW_SUB, used * ROW_SUB
        pl.when(jnp.logical_and(n_rows > lo, n_rows <= hi))(functools.partial(run, used))


def _experts(tile_e, tile_rows, xs, wg_bf, wu_bf, wd_bf):
    n_rows = xs.shape[0]
    weights = lambda shape: pl.BlockSpec((1,) + shape, lambda i, te, tr: (te[i], 0, 0),
                                         pipeline_mode=pl.Buffered(1))
    grid_spec = pltpu.PrefetchScalarGridSpec(
        num_scalar_prefetch=2,
        grid=(n_rows // EXPERT_TILE,),
        in_specs=[
            pl.BlockSpec((EXPERT_TILE, PACKED), lambda i, te, tr: (i, 0)),
            weights((D_MODEL, D_FF)), weights((D_MODEL, D_FF)), weights((D_FF, D_MODEL)),
        ],
        out_specs=pl.BlockSpec((EXPERT_TILE, PACKED), lambda i, te, tr: (i, 0)),
    )
    return pl.pallas_call(
        _expert_kernel,
        grid_spec=grid_spec,
        out_shape=jax.ShapeDtypeStruct((n_rows, PACKED), I32),
        compiler_params=_params("arbitrary"),
        name="moe_experts",
    )(tile_e, tile_rows, xs, wg_bf, wu_bf, wd_bf)


def _final_kernel(starts, gf_ref, *rest):
    n = len(starts)
    x_refs, y_refs, sel_g_refs, o_ref = rest[:n], rest[n:2 * n], rest[2 * n:3 * n], rest[-1]
    w = ROUTE_WINDOW
    part = sum((pl.program_id(0) >= s).astype(jnp.int32) for s in starts[1:])

    def own(refs, index):
        value = refs[0][index]
        for p in range(1, n):
            value = jnp.where(part == p, refs[p][index], value)
        return value

    diag = (lax.broadcasted_iota(jnp.int32, (w, w), 0)
            == lax.broadcasted_iota(jnp.int32, (w, w), 1))
    acc = own(x_refs, (slice(None), slice(None)))
    for k in range(TOP_K):
        sel_g = own(sel_g_refs, (slice(k, k + 1), slice(None)))
        gate = jnp.sum(jnp.where(diag, sel_g, 0.0), axis=1, keepdims=True)
        acc = acc + gate * _unpack_bf16_pair(own(y_refs, k))
    o_ref[...] = _rmsnorm_f32(acc, gf_ref[...])


def _final(x_parts, y_parts, sel_g_parts, g_final):
    w = ROUTE_WINDOW
    counts = [xp.shape[0] // w for xp in x_parts]
    starts = tuple(sum(counts[:p]) for p in range(len(counts)))

    def window(p):
        return lambda i: jnp.clip(i - starts[p], 0, counts[p] - 1)

    parts = range(len(counts))
    x_specs = [pl.BlockSpec((w, D_MODEL), lambda i, f=window(p): (f(i), 0)) for p in parts]
    y_specs = [pl.BlockSpec((TOP_K, w, PACKED), lambda i, f=window(p): (0, f(i), 0)) for p in parts]
    g_specs = [pl.BlockSpec((TOP_K, w), lambda i, f=window(p): (0, f(i))) for p in parts]
    return pl.pallas_call(
        functools.partial(_final_kernel, starts),
        grid=(sum(counts),),
        in_specs=[pl.BlockSpec((1, D_MODEL), lambda i: (0, 0))] + x_specs + y_specs + g_specs,
        out_specs=pl.BlockSpec((w, D_MODEL), lambda i: (i, 0)),
        out_shape=jax.ShapeDtypeStruct((sum(counts) * w, D_MODEL), F32),
        compiler_params=_params("parallel"),
        name="moe_final",
    )(g_final, *x_parts, *y_parts, *sel_g_parts)


def _rotary_tables(seq):
    inv_freq = ROPE_BASE ** (-jnp.arange(0, RET_QK_DIM, 2, dtype=F32) / RET_QK_DIM)
    ang = jnp.arange(seq, dtype=F32)[:, None] * inv_freq[None, :]
    return jnp.cos(ang), jnp.sin(ang)


def _decay_tables():
    c = RET_CHUNK
    log_gamma = jnp.log(1.0 - 2.0 ** (-5.0 - jnp.arange(RET_HEADS, dtype=F32)))
    idx = jnp.arange(c, dtype=F32)
    diff = idx[:, None] - idx[None, :]
    causal = diff >= 0
    intra = jnp.where(causal[None],
                      jnp.exp(jnp.where(causal, diff, 0.0)[None] * log_gamma[:, None, None]), 0.0)
    qdec = jnp.exp((idx[None, :] + 1.0) * log_gamma[:, None])[..., None]
    kdec = jnp.exp((c - 1.0 - idx[None, :]) * log_gamma[:, None])[..., None]
    cdec = jnp.exp(c * log_gamma)[:, None, None]
    return intra, qdec, kdec, cdec


def kernel(x, norm_mix0, ret_w_in, ret_w_out, norm_ffn0, ffn_w_gate, ffn_w_up, ffn_w_down,
           norm_mix1, conv_w_in, conv_w, conv_w_out, norm_ffn1, moe_router, moe_w_gate,
           moe_w_up, moe_w_down, norm_final):
    b, seq, d = x.shape
    t = b * seq
    row = lambda g: g.reshape(1, d).astype(F32)
    bf = lambda w: w.astype(BF16)

    cos, sin = _rotary_tables(seq)
    intra, qdec, kdec, cdec = _decay_tables()

    proj = _ret_in(x.reshape(t, d), row(norm_mix0), bf(ret_w_in), cos, sin, seq)
    h, expert_w = _ret_core(proj, x, bf(ret_w_out), intra, qdec, kdec, cdec,
                            (moe_w_gate, moe_w_up, moe_w_down))
    h = _ffn(h.reshape(t, d), row(norm_ffn0), bf(ffn_w_gate), bf(ffn_w_up), bf(ffn_w_down))
    h = h.reshape(b, seq, d)

    conv_args = (row(norm_mix1), bf(conv_w_in), conv_w.astype(F32), bf(conv_w_out),
                 row(norm_ffn1), moe_router.astype(F32).T)
    last = max(1, b // LAST_PART_DIVISOR)
    plans, after = [], ()
    for b0, bp in ((0, b - last), (b - last, last)):
        tp = bp * seq
        h_p, xn, sel_e, sel_r, sel_g, cend = _conv_route(h, b0, bp, after, *conv_args)
        tile_e, tile_rows, dest = _routing_plan(cend, sel_e, sel_r, tp)
        plans.append((h_p.reshape(tp, d), xn, sel_g, tile_e, tile_rows, dest))
        after = (dest,)

    x_parts, y_parts, gate_parts = [], [], []
    for h_p, xn, sel_g, tile_e, tile_rows, dest in plans:
        tp = h_p.shape[0]
        n_rows = (TOP_K * tp // EXPERT_TILE + N_EXPERTS) * EXPERT_TILE
        xs = _dispatch_rows(xn, dest, n_rows)
        y = _experts(tile_e, tile_rows, xs, *expert_w)
        x_parts.append(h_p)
        y_parts.append(_collect_rows(y, dest))
        gate_parts.append(sel_g)
    out = _final(x_parts, y_parts, gate_parts, row(norm_final))
    return out.reshape(b, seq, d)
```

```python
import functools

import jax
import jax.numpy as jnp
from jax import lax
from jax.experimental import pallas as pl
from jax.experimental.pallas import tpu as pltpu
from jax.experimental.pallas import tpu_sc as plsc

D_MODEL = 1024
RET_HEADS = 4
RET_QK_DIM = D_MODEL // RET_HEADS
RET_V_DIM = 2 * RET_QK_DIM
RET_QK_TOTAL = RET_HEADS * RET_QK_DIM
RET_V_TOTAL = RET_HEADS * RET_V_DIM
RET_IN_COLS = 2 * RET_QK_TOTAL + 2 * RET_V_TOTAL
ROPE_BASE = 10000.0
CONV_WIDTH = 3
D_FF = 7 * D_MODEL // 2
N_EXPERTS = 8
TOP_K = 2
NORM_EPS = 1e-6

RET_CHUNK = 256
HALF = RET_QK_DIM // 2

VMEM_LIMIT_BYTES = 56 * 1024 * 1024
SC_CORES = 2
SC_SUBCORES = 16
SC_WORKERS = SC_CORES * SC_SUBCORES
SC_ROWS = 128

TOKEN_TILE = 1024
RET_SEQ_TILE = 512
CONV_SEQ_TILE = 512
FF_TILE = 1792
ROW_SUB = 512
IN_TOKEN_TILE = 512
IN_COL_TILE = 1024
ROUTE_WINDOW = 512
EXPERT_TILE = 1024
LAST_PART_DIVISOR = 2
LANES = 128
BF16_SUBLANES = 16
SIDE_SLAB_BYTES = 2 * 1024 * 1024
PACKED = D_MODEL // 2

BF16 = jnp.bfloat16
F32 = jnp.float32
I32 = jnp.int32


def _params(*semantics):
    return pltpu.CompilerParams(dimension_semantics=semantics,
                                vmem_limit_bytes=VMEM_LIMIT_BYTES)


def _rmsnorm_f32(x, g):
    return x * lax.rsqrt(jnp.mean(x * x, axis=-1, keepdims=True) + NORM_EPS) * g


def _silu(x):
    return x * (1.0 / (1.0 + jnp.exp(-x)))


def _pack_bf16_pair(x):
    half = x.shape[1] // 2
    hi = lax.bitcast_convert_type(x[:, :half].astype(BF16).astype(F32), I32)
    lo = lax.bitcast_convert_type(x[:, half:].astype(BF16).astype(F32), I32)
    return hi | lax.shift_right_logical(lo, 16)


def _unpack_bf16_pair(w):
    hi = lax.bitcast_convert_type(w & jnp.int32(-65536), F32)
    lo = lax.bitcast_convert_type(lax.shift_left(w, 16), F32)
    return jnp.concatenate([hi, lo], axis=1)


def _ret_in_kernel(x_ref, g_ref, w_ref, cos_ref, sin_ref, o_ref):
    xn = _rmsnorm_f32(x_ref[...], g_ref[...]).astype(BF16)
    cos = cos_ref[...]
    sin = sin_ref[...]
    for n in range(RET_IN_COLS // IN_COL_TILE):
        cols = slice(n * IN_COL_TILE, (n + 1) * IN_COL_TILE)
        acc = jnp.dot(xn, w_ref[:, cols], preferred_element_type=F32)
        if n * IN_COL_TILE >= 2 * RET_QK_TOTAL:
            o_ref[:, cols] = acc.astype(BF16)
            continue
        scale = 1.0 if n * IN_COL_TILE < RET_QK_TOTAL else RET_QK_DIM ** -0.5
        for h in range(IN_COL_TILE // RET_QK_DIM):
            lo = h * RET_QK_DIM
            t1 = acc[:, lo:lo + HALF]
            t2 = acc[:, lo + HALF:lo + RET_QK_DIM]
            base = n * IN_COL_TILE + lo
            o_ref[:, base:base + HALF] = ((t1 * cos - t2 * sin) * scale).astype(BF16)
            o_ref[:, base + HALF:base + RET_QK_DIM] = ((t1 * sin + t2 * cos) * scale).astype(BF16)


def _ret_in(x2, g, w_bf, cos, sin, seq):
    t = x2.shape[0]
    tm = IN_TOKEN_TILE
    seq_tiles = seq // tm
    return pl.pallas_call(
        _ret_in_kernel,
        grid=(t // tm,),
        in_specs=[
            pl.BlockSpec((tm, D_MODEL), lambda i: (i, 0)),
            pl.BlockSpec((1, D_MODEL), lambda i: (0, 0)),
            pl.BlockSpec((D_MODEL, RET_IN_COLS), lambda i: (0, 0), pipeline_mode=pl.Buffered(1)),
            pl.BlockSpec((tm, HALF), lambda i: (i % seq_tiles, 0)),
            pl.BlockSpec((tm, HALF), lambda i: (i % seq_tiles, 0)),
        ],
        out_specs=pl.BlockSpec((tm, RET_IN_COLS), lambda i: (i, 0)),
        out_shape=jax.ShapeDtypeStruct((t, RET_IN_COLS), BF16),
        compiler_params=_params("parallel"),
        name="ret_in_proj",
    )(x2, g, w_bf, cos, sin)


def _ret_core_kernel(q_ref, k_ref, v_ref, gate_ref, x_ref, wout_ref,
                     intra_ref, qdec_ref, kdec_ref, cdec_ref, *rest):
    n_side = (len(rest) - 2) // 2
    side_in, o_ref, side_out, state_ref = (rest[:n_side], rest[n_side],
                                           rest[n_side + 1:2 * n_side + 1], rest[-1])

    for src, dst in zip(side_in, side_out):
        dst[...] = src[...].astype(BF16)

    @pl.when(pl.program_id(1) == 0)
    def _():
        state_ref[...] = jnp.zeros_like(state_ref)

    gated = []
    for c in range(RET_SEQ_TILE // RET_CHUNK):
        rows = pl.ds(c * RET_CHUNK, RET_CHUNK)
        heads = []
        for h in range(RET_HEADS):
            qk_cols = slice(h * RET_QK_DIM, (h + 1) * RET_QK_DIM)
            v_cols = slice(h * RET_V_DIM, (h + 1) * RET_V_DIM)
            q = q_ref[rows, qk_cols]
            k = k_ref[rows, qk_cols]
            v = v_ref[rows, v_cols]
            state = state_ref[h]
            scores = lax.dot_general(q, k, (((1,), (1,)), ((), ())),
                                     preferred_element_type=F32) * intra_ref[h]
            inner = jnp.dot(scores.astype(BF16), v, preferred_element_type=F32)
            qd = (q.astype(F32) * qdec_ref[h]).astype(BF16)
            cross = jnp.dot(qd, state.astype(BF16), preferred_element_type=F32)
            kd = (k.astype(F32) * kdec_ref[h]).astype(BF16)
            state_ref[h] = state * cdec_ref[h] + lax.dot_general(
                kd, v, (((0,), (0,)), ((), ())), preferred_element_type=F32)
            o = inner + cross
            o = o * lax.rsqrt(jnp.mean(o * o, axis=-1, keepdims=True) + NORM_EPS)
            gate = gate_ref[rows, v_cols].astype(F32)
            heads.append((_silu(gate) * o).astype(BF16))
        gated.append(jnp.concatenate(heads, axis=1))
    og = jnp.concatenate(gated, axis=0)
    o_ref[0] = x_ref[0] + jnp.dot(og, wout_ref[...], preferred_element_type=F32)


def _ret_core(proj, x, wout_bf, intra, qdec, kdec, cdec, side_weights):
    b, seq, _ = x.shape
    ts = RET_SEQ_TILE
    st = seq // ts
    steps = b * st
    whole = lambda a: pl.BlockSpec(a.shape, lambda bi, s: (0,) * a.ndim)

    def slab_rows(wt):
        rows = wt.shape[0] * wt.shape[1]
        per_step = rows // steps
        ok = (rows % steps == 0 and per_step % BF16_SUBLANES == 0 and wt.shape[1] % per_step == 0
              and per_step * wt.shape[2] * 4 <= SIDE_SLAB_BYTES)
        return per_step if ok else None

    hosted = all(slab_rows(wt) is not None for wt in side_weights)
    side = tuple(side_weights) if hosted else ()
    side_specs = []
    for wt in side:
        per_step = slab_rows(wt)
        per_mat = wt.shape[1] // per_step
        side_specs.append(pl.BlockSpec(
            (1, per_step, wt.shape[2]),
            lambda bi, s, per_mat=per_mat: ((bi * st + s) // per_mat, (bi * st + s) % per_mat, 0)))

    outs = pl.pallas_call(
        _ret_core_kernel,
        grid=(b, st),
        in_specs=[
            pl.BlockSpec((ts, RET_QK_TOTAL), lambda bi, s: (bi * st + s, 0)),
            pl.BlockSpec((ts, RET_QK_TOTAL), lambda bi, s: (bi * st + s, 1)),
            pl.BlockSpec((ts, RET_V_TOTAL), lambda bi, s: (bi * st + s, 1)),
            pl.BlockSpec((ts, RET_V_TOTAL), lambda bi, s: (bi * st + s, 2)),
            pl.BlockSpec((1, ts, D_MODEL), lambda bi, s: (bi, s, 0)),
            pl.BlockSpec((RET_V_TOTAL, D_MODEL), lambda bi, s: (0, 0), pipeline_mode=pl.Buffered(1)),
            whole(intra), whole(qdec), whole(kdec), whole(cdec),
        ] + side_specs,
        out_specs=[pl.BlockSpec((1, ts, D_MODEL), lambda bi, s: (bi, s, 0))] + side_specs,
        out_shape=[jax.ShapeDtypeStruct(x.shape, F32)]
                  + [jax.ShapeDtypeStruct(wt.shape, BF16) for wt in side],
        scratch_shapes=[pltpu.VMEM((RET_HEADS, RET_QK_DIM, RET_V_DIM), F32)],
        compiler_params=_params("arbitrary", "arbitrary"),
        name="ret_core",
    )(proj, proj, proj, proj, x, wout_bf, intra, qdec, kdec, cdec, *side)
    if hosted:
        return outs[0], tuple(outs[1:])
    return outs[0], tuple(wt.astype(BF16) for wt in side_weights)


def _swiglu(x_bf, wg_ref, wu_ref, wd_ref):
    out = None
    for c in range(D_FF // FF_TILE):
        cols = slice(c * FF_TILE, (c + 1) * FF_TILE)
        hg = jnp.dot(x_bf, wg_ref[:, cols], preferred_element_type=F32)
        hu = jnp.dot(x_bf, wu_ref[:, cols], preferred_element_type=F32)
        hid = (_silu(hg) * hu).astype(BF16)
        part = jnp.dot(hid, wd_ref[cols, :], preferred_element_type=F32)
        out = part if out is None else out + part
    return out


def _ffn_kernel(x_ref, g_ref, wg_ref, wu_ref, wd_ref, o_ref):
    for r in range(TOKEN_TILE // ROW_SUB):
        rows = pl.ds(r * ROW_SUB, ROW_SUB)
        x = x_ref[rows, :]
        xn = _rmsnorm_f32(x, g_ref[...]).astype(BF16)
        o_ref[rows, :] = x + _swiglu(xn, wg_ref, wu_ref, wd_ref)


def _ffn(x2, g, wg_bf, wu_bf, wd_bf):
    t = x2.shape[0]
    resident = lambda shape: pl.BlockSpec(shape, lambda i: (0, 0), pipeline_mode=pl.Buffered(1))
    return pl.pallas_call(
        _ffn_kernel,
        grid=(t // TOKEN_TILE,),
        in_specs=[
            pl.BlockSpec((TOKEN_TILE, D_MODEL), lambda i: (i, 0)),
            pl.BlockSpec((1, D_MODEL), lambda i: (0, 0)),
            resident((D_MODEL, D_FF)), resident((D_MODEL, D_FF)), resident((D_FF, D_MODEL)),
        ],
        out_specs=pl.BlockSpec((TOKEN_TILE, D_MODEL), lambda i: (i, 0)),
        out_shape=jax.ShapeDtypeStruct(x2.shape, F32),
        compiler_params=_params("parallel"),
        name="ffn_swiglu",
    )(x2, g, wg_bf, wu_bf, wd_bf)


def _split_bf16(x):
    hi = x.astype(BF16)
    return hi, (x - hi.astype(F32)).astype(BF16)


def _route_normalise(h, g_ref, xn_ref):
    xn = _rmsnorm_f32(h, g_ref[...])
    xn_ref[...] = _pack_bf16_pair(xn)
    return _split_bf16(xn)


def _route_select(x_hi, x_lo, wrt_ref, before_ref, count_ref,
                  sel_e_ref, sel_r_ref, sel_g_ref, cend_ref):
    w_hi, w_lo = _split_bf16(wrt_ref[...])
    nt = lambda a, b: lax.dot_general(a, b, (((1,), (1,)), ((), ())), preferred_element_type=F32)
    both = nt(jnp.concatenate([w_hi, w_lo], axis=0), x_hi)
    logits = both[:N_EXPERTS] + (both[N_EXPERTS:] + nt(w_hi, x_lo))
    sub = lax.broadcasted_iota(jnp.int32, logits.shape, 0).astype(F32)
    none = float(N_EXPERTS)
    m1 = jnp.max(logits, axis=0, keepdims=True)
    i1 = jnp.min(jnp.where(logits == m1, sub, none), axis=0, keepdims=True)
    rest = jnp.where(sub == i1, -jnp.inf, logits)
    m2 = jnp.max(rest, axis=0, keepdims=True)
    i2 = jnp.min(jnp.where(rest == m2, sub, none), axis=0, keepdims=True)
    e2 = jnp.exp(m2 - m1)
    denom = 1.0 + e2

    chosen = jnp.logical_or(sub == i1, sub == i2)
    prefix = jnp.dot(jnp.where(chosen, 1.0, 0.0).astype(BF16), before_ref[...],
                     preferred_element_type=F32)
    count = count_ref[...]
    rank = prefix + count[:, :1]
    r1 = jnp.sum(jnp.where(sub == i1, rank, 0.0), axis=0, keepdims=True)
    r2 = jnp.sum(jnp.where(sub == i2, rank, 0.0), axis=0, keepdims=True)
    sel_e_ref[0:1, :] = i1.astype(I32)
    sel_e_ref[1:2, :] = i2.astype(I32)
    sel_r_ref[0:1, :] = r1.astype(I32)
    sel_r_ref[1:2, :] = r2.astype(I32)
    sel_g_ref[0:1, :] = 1.0 / denom
    sel_g_ref[1:2, :] = e2 / denom
    count = count + jnp.sum(jnp.where(chosen, 1.0, 0.0), axis=1, keepdims=True)
    count_ref[...] = count
    cend_ref[0] = count


def _conv_route_kernel(x_ref, g_ref, win_ref, wc_ref, wout_ref, g2_ref, wrt_ref, before_ref,
                       o_ref, xn_ref, sel_e_ref, sel_r_ref, sel_g_ref, cend_ref,
                       carry_ref, count_ref):
    s = pl.program_id(1)
    ts = CONV_SEQ_TILE

    @pl.when(s == 0)
    def _():
        carry_ref[...] = jnp.zeros_like(carry_ref)

    @pl.when(jnp.logical_and(pl.program_id(0) == 0, s == 0))
    def _():
        count_ref[...] = jnp.zeros_like(count_ref)

    x = x_ref[0]
    xn = _rmsnorm_f32(x, g_ref[...]).astype(BF16)
    proj = jnp.dot(xn, win_ref[...], preferred_element_type=F32)
    gate_b = proj[:, :D_MODEL]
    gate_c = proj[:, D_MODEL:2 * D_MODEL]
    hid = proj[:, 2 * D_MODEL:]
    u = gate_c * hid

    row = lax.broadcasted_iota(jnp.int32, (ts, D_MODEL), 0)
    prev1 = carry_ref[7:8, :]
    prev2 = carry_ref[6:7, :]
    u1 = jnp.where(row == 0, prev1, pltpu.roll(u, 1, axis=0))
    u2 = jnp.where(row == 0, prev2, jnp.where(row == 1, prev1, pltpu.roll(u, 2, axis=0)))
    carry_ref[...] = u[ts - 8:, :]

    wc = wc_ref[...]
    y = wc[2:3, :] * u + wc[1:2, :] * u1 + wc[0:1, :] * u2
    y = (gate_b * y).astype(BF16)
    h = x + jnp.dot(y, wout_ref[...], preferred_element_type=F32)
    o_ref[0] = h
    x_hi, x_lo = _route_normalise(h, g2_ref, xn_ref)
    _route_select(x_hi, x_lo, wrt_ref, before_ref, count_ref,
                  sel_e_ref, sel_r_ref, sel_g_ref, cend_ref)


def _conv_route(x, b0, bp, after, g, win_bf, wc, wout_bf, g2, wrt):
    b, seq, _ = x.shape
    ts = CONV_SEQ_TILE
    w = ROUTE_WINDOW
    assert ts == w
    st = seq // ts
    steps = bp * st
    tp = bp * seq
    flat = lambda bi, s: bi * st + s
    resident = lambda shape: pl.BlockSpec(shape, lambda bi, s: (0, 0), pipeline_mode=pl.Buffered(1))
    before = (jnp.arange(w)[:, None] < jnp.arange(w)[None, :]).astype(BF16)

    per_token = lambda dtype: jax.ShapeDtypeStruct((TOP_K, tp), dtype)
    n_after = len(after)

    def body(*refs):
        n_in = 8
        _conv_route_kernel(*refs[:n_in], *refs[n_in + n_after:])

    return pl.pallas_call(
        body,
        grid=(bp, st),
        in_specs=[
            pl.BlockSpec((1, ts, D_MODEL), lambda bi, s: (b0 + bi, s, 0)),
            pl.BlockSpec((1, D_MODEL), lambda bi, s: (0, 0)),
            resident((D_MODEL, 3 * D_MODEL)),
            pl.BlockSpec((CONV_WIDTH, D_MODEL), lambda bi, s: (0, 0)),
            resident((D_MODEL, D_MODEL)),
            pl.BlockSpec((1, D_MODEL), lambda bi, s: (0, 0)),
            pl.BlockSpec((N_EXPERTS, D_MODEL), lambda bi, s: (0, 0)),
            resident((w, w)),
        ] + [pl.BlockSpec(memory_space=pl.ANY)] * n_after,
        out_specs=[
            pl.BlockSpec((1, ts, D_MODEL), lambda bi, s: (bi, s, 0)),
            pl.BlockSpec((w, PACKED), lambda bi, s: (flat(bi, s), 0)),
            pl.BlockSpec((TOP_K, w), lambda bi, s: (0, flat(bi, s))),
            pl.BlockSpec((TOP_K, w), lambda bi, s: (0, flat(bi, s))),
            pl.BlockSpec((TOP_K, w), lambda bi, s: (0, flat(bi, s))),
            pl.BlockSpec((1, N_EXPERTS, LANES), lambda bi, s: (flat(bi, s), 0, 0)),
        ],
        out_shape=[jax.ShapeDtypeStruct((bp, seq, D_MODEL), F32),
                   jax.ShapeDtypeStruct((tp, PACKED), I32),
                   per_token(I32), per_token(I32), per_token(F32),
                   jax.ShapeDtypeStruct((steps, N_EXPERTS, LANES), F32)],
        scratch_shapes=[pltpu.VMEM((8, D_MODEL), F32), pltpu.VMEM((N_EXPERTS, LANES), F32)],
        compiler_params=_params("arbitrary", "arbitrary"),
        name="conv_route",
    )(x, g, win_bf, wc, wout_bf, g2, wrt, before, *after)


def _routing_plan(cend, sel_e, sel_r, t):
    counts = cend[-1, :, 0].astype(I32)
    ntiles = (counts + EXPERT_TILE - 1) // EXPERT_TILE
    tile_end = jnp.cumsum(ntiles)
    tile_start = tile_end - ntiles
    off = tile_start * EXPERT_TILE

    nt_max = TOP_K * t // EXPERT_TILE + N_EXPERTS
    ti = jnp.arange(nt_max, dtype=I32)
    tile_e = jnp.minimum(jnp.sum(tile_end[None, :] <= ti[:, None], axis=1).astype(I32),
                         N_EXPERTS - 1)
    tile_rows = jnp.clip(counts[tile_e] - (ti - tile_start[tile_e]) * EXPERT_TILE, 0, EXPERT_TILE)
    tile_rows = jnp.where(ti < tile_end[-1], tile_rows, 0)

    first_row = jnp.zeros_like(sel_e)
    for e in range(N_EXPERTS):
        first_row = first_row + jnp.where(sel_e == e, off[e], 0)
    dest = first_row + sel_r
    return tile_e, tile_rows, dest


def _sc_mesh():
    return plsc.VectorSubcoreMesh(core_axis_name="c", subcore_axis_name="s",
                                  num_cores=SC_CORES, num_subcores=SC_SUBCORES)


def _dispatch_rows(xn, dest, n_rows):
    t, d = xn.shape
    per_worker = t // SC_WORKERS
    chunks = per_worker // SC_ROWS
    dest4 = dest.reshape(TOP_K, SC_WORKERS, chunks, SC_ROWS)

    @functools.partial(
        pl.kernel, mesh=_sc_mesh(),
        out_type=jax.ShapeDtypeStruct((n_rows, d), xn.dtype),
        scratch_types=[pltpu.VMEM((TOP_K, chunks, SC_ROWS), I32),
                       pltpu.VMEM((SC_ROWS, d), xn.dtype)],
        name="moe_dispatch",
    )
    def run(x_hbm, dest_hbm, o_hbm, dest_v, rows_v):
        worker = lax.axis_index("s") * SC_CORES + lax.axis_index("c")
        for k in range(TOP_K):
            pltpu.sync_copy(dest_hbm.at[k, worker], dest_v.at[k])

        @pl.loop(0, chunks)
        def _(j):
            pltpu.sync_copy(x_hbm.at[pl.ds(worker * per_worker + j * SC_ROWS, SC_ROWS)], rows_v)
            for k in range(TOP_K):
                pltpu.sync_copy(rows_v, o_hbm.at[dest_v.at[k, j]])

    return run(xn, dest4)


def _collect_rows(y, dest):
    t = dest.shape[1]
    d = y.shape[1]
    per_worker = t // SC_WORKERS
    chunks = per_worker // SC_ROWS
    dest4 = dest.reshape(TOP_K, SC_WORKERS, chunks, SC_ROWS)

    @functools.partial(
        pl.kernel, mesh=_sc_mesh(),
        out_type=jax.ShapeDtypeStruct((TOP_K, t, d), y.dtype),
        scratch_types=[pltpu.VMEM((TOP_K, chunks, SC_ROWS), I32),
                       pltpu.VMEM((SC_ROWS, d), y.dtype)],
        name="moe_collect",
    )
    def run(y_hbm, dest_hbm, o_hbm, dest_v, rows_v):
        worker = lax.axis_index("s") * SC_CORES + lax.axis_index("c")
        for k in range(TOP_K):
            pltpu.sync_copy(dest_hbm.at[k, worker], dest_v.at[k])

        @pl.loop(0, chunks)
        def _(j):
            for k in range(TOP_K):
                pltpu.sync_copy(y_hbm.at[dest_v.at[k, j]], rows_v)
                pltpu.sync_copy(rows_v, o_hbm.at[k, pl.ds(worker * per_worker + j * SC_ROWS, SC_ROWS)])

    return run(y, dest4)


def _expert_kernel(tile_e_s, tile_rows_s, xs_ref, wg_ref, wu_ref, wd_ref, y_ref):
    n_rows = tile_rows_s[pl.program_id(0)]
    n_sub = EXPERT_TILE // ROW_SUB

    def run(sub_blocks):
        for r in range(sub_blocks):
            rows = pl.ds(r * ROW_SUB, ROW_SUB)
            real = lax.broadcasted_iota(jnp.int32, (ROW_SUB, PACKED), 0) < n_rows - r * ROW_SUB
            xs = _unpack_bf16_pair(jnp.where(real, xs_ref[rows, :], 0)).astype(BF16)
            y_ref[rows, :] = _pack_bf16_pair(_swiglu(xs, wg_ref.at[0], wu_ref.at[0], wd_ref.at[0]))
        if sub_blocks < n_sub:
            rest = pl.ds(sub_blocks * ROW_SUB, (n_sub - sub_blocks) * ROW_SUB)
            y_ref[rest, :] = jnp.zeros(((n_sub - sub_blocks) * ROW_SUB, PACKED), I32)

    for used in range(n_sub + 1):
        lo, hi = (used - 1) * ROW_SUB, used * ROW_SUB
        pl.when(jnp.logical_and(n_rows > lo, n_rows <= hi))(functools.partial(run, used))


def _experts(tile_e, tile_rows, xs, wg_bf, wu_bf, wd_bf):
    n_rows = xs.shape[0]
    weights = lambda shape: pl.BlockSpec((1,) + shape, lambda i, te, tr: (te[i], 0, 0),
                                         pipeline_mode=pl.Buffered(1))
    grid_spec = pltpu.PrefetchScalarGridSpec(
        num_scalar_prefetch=2,
        grid=(n_rows // EXPERT_TILE,),
        in_specs=[
            pl.BlockSpec((EXPERT_TILE, PACKED), lambda i, te, tr: (i, 0)),
            weights((D_MODEL, D_FF)), weights((D_MODEL, D_FF)), weights((D_FF, D_MODEL)),
        ],
        out_specs=pl.BlockSpec((EXPERT_TILE, PACKED), lambda i, te, tr: (i, 0)),
    )
    return pl.pallas_call(
        _expert_kernel,
        grid_spec=grid_spec,
        out_shape=jax.ShapeDtypeStruct((n_rows, PACKED), I32),
        compiler_params=_params("arbitrary"),
        name="moe_experts",
    )(tile_e, tile_rows, xs, wg_bf, wu_bf, wd_bf)


def _final_kernel(starts, gf_ref, *rest):
    n = len(starts)
    x_refs, y_refs, sel_g_refs, o_ref = rest[:n], rest[n:2 * n], rest[2 * n:3 * n], rest[-1]
    w = ROUTE_WINDOW
    part = sum((pl.program_id(0) >= s).astype(jnp.int32) for s in starts[1:])

    def own(refs, index):
        value = refs[0][index]
        for p in range(1, n):
            value = jnp.where(part == p, refs[p][index], value)
        return value

    diag = (lax.broadcasted_iota(jnp.int32, (w, w), 0)
            == lax.broadcasted_iota(jnp.int32, (w, w), 1))
    acc = own(x_refs, (slice(None), slice(None)))
    for k in range(TOP_K):
        sel_g = own(sel_g_refs, (slice(k, k + 1), slice(None)))
        gate = jnp.sum(jnp.where(diag, sel_g, 0.0), axis=1, keepdims=True)
        acc = acc + gate * _unpack_bf16_pair(own(y_refs, k))
    o_ref[...] = _rmsnorm_f32(acc, gf_ref[...])


def _final(x_parts, y_parts, sel_g_parts, g_final):
    w = ROUTE_WINDOW
    counts = [xp.shape[0] // w for xp in x_parts]
    starts = tuple(sum(counts[:p]) for p in range(len(counts)))

    def window(p):
        return lambda i: jnp.clip(i - starts[p], 0, counts[p] - 1)

    parts = range(len(counts))
    x_specs = [pl.BlockSpec((w, D_MODEL), lambda i, f=window(p): (f(i), 0)) for p in parts]
    y_specs = [pl.BlockSpec((TOP_K, w, PACKED), lambda i, f=window(p): (0, f(i), 0)) for p in parts]
    g_specs = [pl.BlockSpec((TOP_K, w), lambda i, f=window(p): (0, f(i))) for p in parts]
    return pl.pallas_call(
        functools.partial(_final_kernel, starts),
        grid=(sum(counts),),
        in_specs=[pl.BlockSpec((1, D_MODEL), lambda i: (0, 0))] + x_specs + y_specs + g_specs,
        out_specs=pl.BlockSpec((w, D_MODEL), lambda i: (i, 0)),
        out_shape=jax.ShapeDtypeStruct((sum(counts) * w, D_MODEL), F32),
        compiler_params=_params("parallel"),
        name="moe_final",
    )(g_final, *x_parts, *y_parts, *sel_g_parts)


def _rotary_tables(seq):
    inv_freq = ROPE_BASE ** (-jnp.arange(0, RET_QK_DIM, 2, dtype=F32) / RET_QK_DIM)
    ang = jnp.arange(seq, dtype=F32)[:, None] * inv_freq[None, :]
    return jnp.cos(ang), jnp.sin(ang)


def _decay_tables():
    c = RET_CHUNK
    log_gamma = jnp.log(1.0 - 2.0 ** (-5.0 - jnp.arange(RET_HEADS, dtype=F32)))
    idx = jnp.arange(c, dtype=F32)
    diff = idx[:, None] - idx[None, :]
    causal = diff >= 0
    intra = jnp.where(causal[None],
                      jnp.exp(jnp.where(causal, diff, 0.0)[None] * log_gamma[:, None, None]), 0.0)
    qdec = jnp.exp((idx[None, :] + 1.0) * log_gamma[:, None])[..., None]
    kdec = jnp.exp((c - 1.0 - idx[None, :]) * log_gamma[:, None])[..., None]
    cdec = jnp.exp(c * log_gamma)[:, None, None]
    return intra, qdec, kdec, cdec


def kernel(x, norm_mix0, ret_w_in, ret_w_out, norm_ffn0, ffn_w_gate, ffn_w_up, ffn_w_down,
           norm_mix1, conv_w_in, conv_w, conv_w_out, norm_ffn1, moe_router, moe_w_gate,
           moe_w_up, moe_w_down, norm_final):
    b, seq, d = x.shape
    t = b * seq
    row = lambda g: g.reshape(1, d).astype(F32)
    bf = lambda w: w.astype(BF16)

    cos, sin = _rotary_tables(seq)
    intra, qdec, kdec, cdec = _decay_tables()

    proj = _ret_in(x.reshape(t, d), row(norm_mix0), bf(ret_w_in), cos, sin, seq)
    h, expert_w = _ret_core(proj, x, bf(ret_w_out), intra, qdec, kdec, cdec,
                            (moe_w_gate, moe_w_up, moe_w_down))
    h = _ffn(h.reshape(t, d), row(norm_ffn0), bf(ffn_w_gate), bf(ffn_w_up), bf(ffn_w_down))
    h = h.reshape(b, seq, d)

    conv_args = (row(norm_mix1), bf(conv_w_in), conv_w.astype(F32), bf(conv_w_out),
                 row(norm_ffn1), moe_router.astype(F32).T)
    last = max(1, b // LAST_PART_DIVISOR)
    plans, after = [], ()
    for b0, bp in ((0, b - last), (b - last, last)):
        tp = bp * seq
        h_p, xn, sel_e, sel_r, sel_g, cend = _conv_route(h, b0, bp, after, *conv_args)
        tile_e, tile_rows, dest = _routing_plan(cend, sel_e, sel_r, tp)
        plans.append((h_p.reshape(tp, d), xn, sel_g, tile_e, tile_rows, dest))
        after = (dest,)

    x_parts, y_parts, gate_parts = [], [], []
    for h_p, xn, sel_g, tile_e, tile_rows, dest in plans:
        tp = h_p.shape[0]
        n_rows = (TOP_K * tp // EXPERT_TILE + N_EXPERTS) * EXPERT_TILE
        xs = _dispatch_rows(xn, dest, n_rows)
        y = _experts(tile_e, tile_rows, xs, *expert_w)
        x_parts.append(h_p)
        y_parts.append(_collect_rows(y, dest))
        gate_parts.append(sel_g)
    out = _final(x_parts, y_parts, gate_parts, row(norm_final))
    return out.reshape(b, seq, d)
```

```python
import functools

import jax
import jax.numpy as jnp
from jax import lax
from jax.experimental import pallas as pl
from jax.experimental.pallas import tpu as pltpu
from jax.experimental.pallas import tpu_sc as plsc

D_MODEL = 1024
RET_HEADS = 4
RET_QK_DIM = D_MODEL // RET_HEADS
RET_V_DIM = 2 * RET_QK_DIM
RET_QK_TOTAL = RET_HEADS * RET_QK_DIM
RET_V_TOTAL = RET_HEADS * RET_V_DIM
RET_IN_COLS = 2 * RET_QK_TOTAL + 2 * RET_V_TOTAL
ROPE_BASE = 10000.0
CONV_WIDTH = 3
D_FF = 7 * D_MODEL // 2
N_EXPERTS = 8
TOP_K = 2
NORM_EPS = 1e-6

RET_CHUNK = 256
HALF = RET_QK_DIM // 2

VMEM_LIMIT_BYTES = 56 * 1024 * 1024
SC_CORES = 2
SC_SUBCORES = 16
SC_WORKERS = SC_CORES * SC_SUBCORES
SC_ROWS = 64

TOKEN_TILE = 1024
RET_SEQ_TILE = 512
CONV_SEQ_TILE = 512
FF_TILE = 1792
ROW_SUB = 512
IN_TOKEN_TILE = 512
IN_COL_TILE = 1024
ROUTE_WINDOW = 512
EXPERT_TILE = 1024
MOE_PARTS = 2
LANES = 128
BF16_SUBLANES = 16
SIDE_SLAB_BYTES = 2 * 1024 * 1024
PACKED = D_MODEL // 2

BF16 = jnp.bfloat16
F32 = jnp.float32
I32 = jnp.int32


def _params(*semantics):
    return pltpu.CompilerParams(dimension_semantics=semantics,
                                vmem_limit_bytes=VMEM_LIMIT_BYTES)


def _rmsnorm_f32(x, g):
    return x * lax.rsqrt(jnp.mean(x * x, axis=-1, keepdims=True) + NORM_EPS) * g


def _silu(x):
    return x * (1.0 / (1.0 + jnp.exp(-x)))


def _pack_bf16_pair(x):
    half = x.shape[1] // 2
    hi = lax.bitcast_convert_type(x[:, :half].astype(BF16).astype(F32), I32)
    lo = lax.bitcast_convert_type(x[:, half:].astype(BF16).astype(F32), I32)
    return hi | lax.shift_right_logical(lo, 16)


def _unpack_bf16_pair(w):
    hi = lax.bitcast_convert_type(w & jnp.int32(-65536), F32)
    lo = lax.bitcast_convert_type(lax.shift_left(w, 16), F32)
    return jnp.concatenate([hi, lo], axis=1)


def _ret_in_kernel(x_ref, g_ref, w_ref, cos_ref, sin_ref, o_ref):
    xn = _rmsnorm_f32(x_ref[...], g_ref[...]).astype(BF16)
    cos = cos_ref[...]
    sin = sin_ref[...]
    for n in range(RET_IN_COLS // IN_COL_TILE):
        cols = slice(n * IN_COL_TILE, (n + 1) * IN_COL_TILE)
        acc = jnp.dot(xn, w_ref[:, cols], preferred_element_type=F32)
        if n * IN_COL_TILE >= 2 * RET_QK_TOTAL:
            o_ref[:, cols] = acc.astype(BF16)
            continue
        scale = 1.0 if n * IN_COL_TILE < RET_QK_TOTAL else RET_QK_DIM ** -0.5
        for h in range(IN_COL_TILE // RET_QK_DIM):
            lo = h * RET_QK_DIM
            t1 = acc[:, lo:lo + HALF]
            t2 = acc[:, lo + HALF:lo + RET_QK_DIM]
            base = n * IN_COL_TILE + lo
            o_ref[:, base:base + HALF] = ((t1 * cos - t2 * sin) * scale).astype(BF16)
            o_ref[:, base + HALF:base + RET_QK_DIM] = ((t1 * sin + t2 * cos) * scale).astype(BF16)


def _ret_in(x2, g, w_bf, cos, sin, seq):
    t = x2.shape[0]
    tm = IN_TOKEN_TILE
    seq_tiles = seq // tm
    return pl.pallas_call(
        _ret_in_kernel,
        grid=(t // tm,),
        in_specs=[
            pl.BlockSpec((tm, D_MODEL), lambda i: (i, 0)),
            pl.BlockSpec((1, D_MODEL), lambda i: (0, 0)),
            pl.BlockSpec((D_MODEL, RET_IN_COLS), lambda i: (0, 0), pipeline_mode=pl.Buffered(1)),
            pl.BlockSpec((tm, HALF), lambda i: (i % seq_tiles, 0)),
            pl.BlockSpec((tm, HALF), lambda i: (i % seq_tiles, 0)),
        ],
        out_specs=pl.BlockSpec((tm, RET_IN_COLS), lambda i: (i, 0)),
        out_shape=jax.ShapeDtypeStruct((t, RET_IN_COLS), BF16),
        compiler_params=_params("parallel"),
        name="ret_in_proj",
    )(x2, g, w_bf, cos, sin)


def _ret_core_kernel(q_ref, k_ref, v_ref, gate_ref, x_ref, wout_ref,
                     intra_ref, qdec_ref, kdec_ref, cdec_ref, *rest):
    n_side = (len(rest) - 2) // 2
    side_in, o_ref, side_out, state_ref = (rest[:n_side], rest[n_side],
                                           rest[n_side + 1:2 * n_side + 1], rest[-1])

    for src, dst in zip(side_in, side_out):
        dst[...] = src[...].astype(BF16)

    @pl.when(pl.program_id(1) == 0)
    def _():
        state_ref[...] = jnp.zeros_like(state_ref)

    gated = []
    for c in range(RET_SEQ_TILE // RET_CHUNK):
        rows = pl.ds(c * RET_CHUNK, RET_CHUNK)
        heads = []
        for h in range(RET_HEADS):
            qk_cols = slice(h * RET_QK_DIM, (h + 1) * RET_QK_DIM)
            v_cols = slice(h * RET_V_DIM, (h + 1) * RET_V_DIM)
            q = q_ref[rows, qk_cols]
            k = k_ref[rows, qk_cols]
            v = v_ref[rows, v_cols]
            state = state_ref[h]
            scores = lax.dot_general(q, k, (((1,), (1,)), ((), ())),
                                     preferred_element_type=F32) * intra_ref[h]
            inner = jnp.dot(scores.astype(BF16), v, preferred_element_type=F32)
            qd = (q.astype(F32) * qdec_ref[h]).astype(BF16)
            cross = jnp.dot(qd, state.astype(BF16), preferred_element_type=F32)
            kd = (k.astype(F32) * kdec_ref[h]).astype(BF16)
            state_ref[h] = state * cdec_ref[h] + lax.dot_general(
                kd, v, (((0,), (0,)), ((), ())), preferred_element_type=F32)
            o = inner + cross
            o = o * lax.rsqrt(jnp.mean(o * o, axis=-1, keepdims=True) + NORM_EPS)
            gate = gate_ref[rows, v_cols].astype(F32)
            heads.append((_silu(gate) * o).astype(BF16))
        gated.append(jnp.concatenate(heads, axis=1))
    og = jnp.concatenate(gated, axis=0)
    o_ref[0] = x_ref[0] + jnp.dot(og, wout_ref[...], preferred_element_type=F32)


def _ret_core(proj, x, wout_bf, intra, qdec, kdec, cdec, side_weights):
    b, seq, _ = x.shape
    ts = RET_SEQ_TILE
    st = seq // ts
    steps = b * st
    whole = lambda a: pl.BlockSpec(a.shape, lambda bi, s: (0,) * a.ndim)

    def slab_rows(wt):
        rows = wt.shape[0] * wt.shape[1]
        per_step = rows // steps
        ok = (rows % steps == 0 and per_step % BF16_SUBLANES == 0 and wt.shape[1] % per_step == 0
              and per_step * wt.shape[2] * 4 <= SIDE_SLAB_BYTES)
        return per_step if ok else None

    hosted = all(slab_rows(wt) is not None for wt in side_weights)
    side = tuple(side_weights) if hosted else ()
    side_specs = []
    for wt in side:
        per_step = slab_rows(wt)
        per_mat = wt.shape[1] // per_step
        side_specs.append(pl.BlockSpec(
            (1, per_step, wt.shape[2]),
            lambda bi, s, per_mat=per_mat: ((bi * st + s) // per_mat, (bi * st + s) % per_mat, 0)))

    outs = pl.pallas_call(
        _ret_core_kernel,
        grid=(b, st),
        in_specs=[
            pl.BlockSpec((ts, RET_QK_TOTAL), lambda bi, s: (bi * st + s, 0)),
            pl.BlockSpec((ts, RET_QK_TOTAL), lambda bi, s: (bi * st + s, 1)),
            pl.BlockSpec((ts, RET_V_TOTAL), lambda bi, s: (bi * st + s, 1)),
            pl.BlockSpec((ts, RET_V_TOTAL), lambda bi, s: (bi * st + s, 2)),
            pl.BlockSpec((1, ts, D_MODEL), lambda bi, s: (bi, s, 0)),
            pl.BlockSpec((RET_V_TOTAL, D_MODEL), lambda bi, s: (0, 0), pipeline_mode=pl.Buffered(1)),
            whole(intra), whole(qdec), whole(kdec), whole(cdec),
        ] + side_specs,
        out_specs=[pl.BlockSpec((1, ts, D_MODEL), lambda bi, s: (bi, s, 0))] + side_specs,
        out_shape=[jax.ShapeDtypeStruct(x.shape, F32)]
                  + [jax.ShapeDtypeStruct(wt.shape, BF16) for wt in side],
        scratch_shapes=[pltpu.VMEM((RET_HEADS, RET_QK_DIM, RET_V_DIM), F32)],
        compiler_params=_params("arbitrary", "arbitrary"),
        name="ret_core",
    )(proj, proj, proj, proj, x, wout_bf, intra, qdec, kdec, cdec, *side)
    if hosted:
        return outs[0], tuple(outs[1:])
    return outs[0], tuple(wt.astype(BF16) for wt in side_weights)


def _swiglu(x_bf, wg_ref, wu_ref, wd_ref):
    out = None
    for c in range(D_FF // FF_TILE):
        cols = slice(c * FF_TILE, (c + 1) * FF_TILE)
        hg = jnp.dot(x_bf, wg_ref[:, cols], preferred_element_type=F32)
        hu = jnp.dot(x_bf, wu_ref[:, cols], preferred_element_type=F32)
        hid = (_silu(hg) * hu).astype(BF16)
        part = jnp.dot(hid, wd_ref[cols, :], preferred_element_type=F32)
        out = part if out is None else out + part
    return out


def _ffn_kernel(x_ref, g_ref, wg_ref, wu_ref, wd_ref, o_ref):
    for r in range(TOKEN_TILE // ROW_SUB):
        rows = pl.ds(r * ROW_SUB, ROW_SUB)
        x = x_ref[rows, :]
        xn = _rmsnorm_f32(x, g_ref[...]).astype(BF16)
        o_ref[rows, :] = x + _swiglu(xn, wg_ref, wu_ref, wd_ref)


def _ffn(x2, g, wg_bf, wu_bf, wd_bf):
    t = x2.shape[0]
    resident = lambda shape: pl.BlockSpec(shape, lambda i: (0, 0), pipeline_mode=pl.Buffered(1))
    return pl.pallas_call(
        _ffn_kernel,
        grid=(t // TOKEN_TILE,),
        in_specs=[
            pl.BlockSpec((TOKEN_TILE, D_MODEL), lambda i: (i, 0)),
            pl.BlockSpec((1, D_MODEL), lambda i: (0, 0)),
            resident((D_MODEL, D_FF)), resident((D_MODEL, D_FF)), resident((D_FF, D_MODEL)),
        ],
        out_specs=pl.BlockSpec((TOKEN_TILE, D_MODEL), lambda i: (i, 0)),
        out_shape=jax.ShapeDtypeStruct(x2.shape, F32),
        compiler_params=_params("parallel"),
        name="ffn_swiglu",
    )(x2, g, wg_bf, wu_bf, wd_bf)


def _split_bf16(x):
    hi = x.astype(BF16)
    return hi, (x - hi.astype(F32)).astype(BF16)


def _route_normalise(h, g_ref, xn_ref):
    xn = _rmsnorm_f32(h, g_ref[...])
    xn_ref[...] = _pack_bf16_pair(xn)
    return _split_bf16(xn)


def _route_select(x_hi, x_lo, wrt_ref, before_ref, count_ref,
                  sel_e_ref, sel_r_ref, sel_g_ref, cend_ref):
    w_hi, w_lo = _split_bf16(wrt_ref[...])
    nt = lambda a, b: lax.dot_general(a, b, (((1,), (1,)), ((), ())), preferred_element_type=F32)
    both = nt(jnp.concatenate([w_hi, w_lo], axis=0), x_hi)
    logits = both[:N_EXPERTS] + (both[N_EXPERTS:] + nt(w_hi, x_lo))
    sub = lax.broadcasted_iota(jnp.int32, logits.shape, 0).astype(F32)
    none = float(N_EXPERTS)
    m1 = jnp.max(logits, axis=0, keepdims=True)
    i1 = jnp.min(jnp.where(logits == m1, sub, none), axis=0, keepdims=True)
    rest = jnp.where(sub == i1, -jnp.inf, logits)
    m2 = jnp.max(rest, axis=0, keepdims=True)
    i2 = jnp.min(jnp.where(rest == m2, sub, none), axis=0, keepdims=True)
    e2 = jnp.exp(m2 - m1)
    denom = 1.0 + e2

    chosen = jnp.logical_or(sub == i1, sub == i2)
    prefix = jnp.dot(jnp.where(chosen, 1.0, 0.0).astype(BF16), before_ref[...],
                     preferred_element_type=F32)
    count = count_ref[...]
    rank = prefix + count[:, :1]
    r1 = jnp.sum(jnp.where(sub == i1, rank, 0.0), axis=0, keepdims=True)
    r2 = jnp.sum(jnp.where(sub == i2, rank, 0.0), axis=0, keepdims=True)
    sel_e_ref[0:1, :] = i1.astype(I32)
    sel_e_ref[1:2, :] = i2.astype(I32)
    sel_r_ref[0:1, :] = r1.astype(I32)
    sel_r_ref[1:2, :] = r2.astype(I32)
    sel_g_ref[0:1, :] = 1.0 / denom
    sel_g_ref[1:2, :] = e2 / denom
    count = count + jnp.sum(jnp.where(chosen, 1.0, 0.0), axis=1, keepdims=True)
    count_ref[...] = count
    cend_ref[0] = count


def _conv_route_kernel(x_ref, g_ref, win_ref, wc_ref, wout_ref, g2_ref, wrt_ref, before_ref,
                       o_ref, xn_ref, sel_e_ref, sel_r_ref, sel_g_ref, cend_ref,
                       carry_ref, count_ref):
    s = pl.program_id(1)
    ts = CONV_SEQ_TILE

    @pl.when(s == 0)
    def _():
        carry_ref[...] = jnp.zeros_like(carry_ref)

    @pl.when(jnp.logical_and(pl.program_id(0) == 0, s == 0))
    def _():
        count_ref[...] = jnp.zeros_like(count_ref)

    x = x_ref[0]
    xn = _rmsnorm_f32(x, g_ref[...]).astype(BF16)
    proj = jnp.dot(xn, win_ref[...], preferred_element_type=F32)
    gate_b = proj[:, :D_MODEL]
    gate_c = proj[:, D_MODEL:2 * D_MODEL]
    hid = proj[:, 2 * D_MODEL:]
    u = gate_c * hid

    row = lax.broadcasted_iota(jnp.int32, (ts, D_MODEL), 0)
    prev1 = carry_ref[7:8, :]
    prev2 = carry_ref[6:7, :]
    u1 = jnp.where(row == 0, prev1, pltpu.roll(u, 1, axis=0))
    u2 = jnp.where(row == 0, prev2, jnp.where(row == 1, prev1, pltpu.roll(u, 2, axis=0)))
    carry_ref[...] = u[ts - 8:, :]

    wc = wc_ref[...]
    y = wc[2:3, :] * u + wc[1:2, :] * u1 + wc[0:1, :] * u2
    y = (gate_b * y).astype(BF16)
    h = x + jnp.dot(y, wout_ref[...], preferred_element_type=F32)
    o_ref[0] = h
    x_hi, x_lo = _route_normalise(h, g2_ref, xn_ref)
    _route_select(x_hi, x_lo, wrt_ref, before_ref, count_ref,
                  sel_e_ref, sel_r_ref, sel_g_ref, cend_ref)


def _conv_route(x, part, g, win_bf, wc, wout_bf, g2, wrt):
    b, seq, _ = x.shape
    ts = CONV_SEQ_TILE
    w = ROUTE_WINDOW
    assert ts == w and b % MOE_PARTS == 0
    bp = b // MOE_PARTS
    b0 = part * bp
    st = seq // ts
    steps = bp * st
    tp = bp * seq
    flat = lambda bi, s: bi * st + s
    resident = lambda shape: pl.BlockSpec(shape, lambda bi, s: (0, 0), pipeline_mode=pl.Buffered(1))
    before = (jnp.arange(w)[:, None] < jnp.arange(w)[None, :]).astype(BF16)

    per_token = lambda dtype: jax.ShapeDtypeStruct((TOP_K, tp), dtype)
    return pl.pallas_call(
        _conv_route_kernel,
        grid=(bp, st),
        in_specs=[
            pl.BlockSpec((1, ts, D_MODEL), lambda bi, s: (b0 + bi, s, 0)),
            pl.BlockSpec((1, D_MODEL), lambda bi, s: (0, 0)),
            resident((D_MODEL, 3 * D_MODEL)),
            pl.BlockSpec((CONV_WIDTH, D_MODEL), lambda bi, s: (0, 0)),
            resident((D_MODEL, D_MODEL)),
            pl.BlockSpec((1, D_MODEL), lambda bi, s: (0, 0)),
            pl.BlockSpec((N_EXPERTS, D_MODEL), lambda bi, s: (0, 0)),
            resident((w, w)),
        ],
        out_specs=[
            pl.BlockSpec((1, ts, D_MODEL), lambda bi, s: (bi, s, 0)),
            pl.BlockSpec((w, PACKED), lambda bi, s: (flat(bi, s), 0)),
            pl.BlockSpec((TOP_K, w), lambda bi, s: (0, flat(bi, s))),
            pl.BlockSpec((TOP_K, w), lambda bi, s: (0, flat(bi, s))),
            pl.BlockSpec((TOP_K, w), lambda bi, s: (0, flat(bi, s))),
            pl.BlockSpec((1, N_EXPERTS, LANES), lambda bi, s: (flat(bi, s), 0, 0)),
        ],
        out_shape=[jax.ShapeDtypeStruct((bp, seq, D_MODEL), F32),
                   jax.ShapeDtypeStruct((tp, PACKED), I32),
                   per_token(I32), per_token(I32), per_token(F32),
                   jax.ShapeDtypeStruct((steps, N_EXPERTS, LANES), F32)],
        scratch_shapes=[pltpu.VMEM((8, D_MODEL), F32), pltpu.VMEM((N_EXPERTS, LANES), F32)],
        compiler_params=_params("arbitrary", "arbitrary"),
        name="conv_route",
    )(x, g, win_bf, wc, wout_bf, g2, wrt, before)


def _routing_plan(cend, sel_e, sel_r, t):
    counts = cend[-1, :, 0].astype(I32)
    ntiles = (counts + EXPERT_TILE - 1) // EXPERT_TILE
    tile_end = jnp.cumsum(ntiles)
    tile_start = tile_end - ntiles
    off = tile_start * EXPERT_TILE

    nt_max = TOP_K * t // EXPERT_TILE + N_EXPERTS
    ti = jnp.arange(nt_max, dtype=I32)
    tile_e = jnp.minimum(jnp.sum(tile_end[None, :] <= ti[:, None], axis=1).astype(I32),
                         N_EXPERTS - 1)
    tile_rows = jnp.clip(counts[tile_e] - (ti - tile_start[tile_e]) * EXPERT_TILE, 0, EXPERT_TILE)
    tile_rows = jnp.where(ti < tile_end[-1], tile_rows, 0)

    first_row = jnp.zeros_like(sel_e)
    for e in range(N_EXPERTS):
        first_row = first_row + jnp.where(sel_e == e, off[e], 0)
    dest = first_row + sel_r
    return tile_e, tile_rows, dest


def _sc_mesh():
    return plsc.VectorSubcoreMesh(core_axis_name="c", subcore_axis_name="s",
                                  num_cores=SC_CORES, num_subcores=SC_SUBCORES)


def _dispatch_rows(xn, dest, n_rows):
    t, d = xn.shape
    per_worker = t // SC_WORKERS
    chunks = per_worker // SC_ROWS
    dest4 = dest.reshape(TOP_K, SC_WORKERS, chunks, SC_ROWS)

    @functools.partial(
        pl.kernel, mesh=_sc_mesh(),
        out_type=jax.ShapeDtypeStruct((n_rows, d), xn.dtype),
        scratch_types=[pltpu.VMEM((TOP_K, chunks, SC_ROWS), I32),
                       pltpu.VMEM((SC_ROWS, d), xn.dtype),
                       pltpu.SemaphoreType.DMA((TOP_K,))],
        name="moe_dispatch",
    )
    def run(x_hbm, dest_hbm, o_hbm, dest_v, rows_v, scatter_sem):
        worker = lax.axis_index("s") * SC_CORES + lax.axis_index("c")
        for k in range(TOP_K):
            pltpu.sync_copy(dest_hbm.at[k, worker], dest_v.at[k])

        def scatter(j, k):
            return pltpu.make_async_copy(rows_v, o_hbm.at[dest_v.at[k, j]], scatter_sem.at[k])

        @pl.loop(0, chunks)
        def _(j):
            pltpu.sync_copy(x_hbm.at[pl.ds(worker * per_worker + j * SC_ROWS, SC_ROWS)], rows_v)
            for k in range(TOP_K):
                scatter(j, k).start()
            for k in range(TOP_K):
                scatter(j, k).wait()

    return run(xn, dest4)


def _collect_rows(y, dest):
    t = dest.shape[1]
    d = y.shape[1]
    per_worker = t // SC_WORKERS
    chunks = per_worker // SC_ROWS
    dest4 = dest.reshape(TOP_K, SC_WORKERS, chunks, SC_ROWS)

    @functools.partial(
        pl.kernel, mesh=_sc_mesh(),
        out_type=jax.ShapeDtypeStruct((TOP_K, t, d), y.dtype),
        scratch_types=[pltpu.VMEM((TOP_K, chunks, SC_ROWS), I32),
                       pltpu.VMEM((TOP_K, SC_ROWS, d), y.dtype),
                       pltpu.SemaphoreType.DMA((TOP_K,)),
                       pltpu.SemaphoreType.DMA((TOP_K,))],
        name="moe_collect",
    )
    def run(y_hbm, dest_hbm, o_hbm, dest_v, rows_v, gather_sem, write_sem):
        worker = lax.axis_index("s") * SC_CORES + lax.axis_index("c")
        for k in range(TOP_K):
            pltpu.sync_copy(dest_hbm.at[k, worker], dest_v.at[k])

        def gather(j, k):
            return pltpu.make_async_copy(y_hbm.at[dest_v.at[k, j]], rows_v.at[k], gather_sem.at[k])

        def write(j, k):
            rows = pl.ds(worker * per_worker + j * SC_ROWS, SC_ROWS)
            return pltpu.make_async_copy(rows_v.at[k], o_hbm.at[k, rows], write_sem.at[k])

        for k in range(TOP_K):
            gather(0, k).start()

        @pl.loop(0, chunks)
        def _(j):
            for k in range(TOP_K):
                gather(j, k).wait()
                write(j, k).start()
            for k in range(TOP_K):
                write(j, k).wait()

                @pl.when(j + 1 < chunks)
                def _():
                    gather(j + 1, k).start()

    return run(y, dest4)


def _expert_kernel(tile_e_s, tile_rows_s, xs_ref, wg_ref, wu_ref, wd_ref, y_ref):
    n_rows = tile_rows_s[pl.program_id(0)]
    n_sub = EXPERT_TILE // ROW_SUB

    def run(sub_blocks):
        for r in range(sub_blocks):
            rows = pl.ds(r * ROW_SUB, ROW_SUB)
            real = lax.broadcasted_iota(jnp.int32, (ROW_SUB, PACKED), 0) < n_rows - r * ROW_SUB
            xs = _unpack_bf16_pair(jnp.where(real, xs_ref[rows, :], 0)).astype(BF16)
            y_ref[rows, :] = _pack_bf16_pair(_swiglu(xs, wg_ref.at[0], wu_ref.at[0], wd_ref.at[0]))
        if sub_blocks < n_sub:
            rest = pl.ds(sub_blocks * ROW_SUB, (n_sub - sub_blocks) * ROW_SUB)
            y_ref[rest, :] = jnp.zeros(((n_sub - sub_blocks) * ROW_SUB, PACKED), I32)

    for used in range(n_sub + 1):
        lo, hi = (used - 1) * ROW_SUB, used * ROW_SUB
        pl.when(jnp.logical_and(n_rows > lo, n_rows <= hi))(functools.partial(run, used))


def _experts(tile_e, tile_rows, xs, wg_bf, wu_bf, wd_bf):
    n_rows = xs.shape[0]
    weights = lambda shape: pl.BlockSpec((1,) + shape, lambda i, te, tr: (te[i], 0, 0),
                                         pipeline_mode=pl.Buffered(1))
    grid_spec = pltpu.PrefetchScalarGridSpec(
        num_scalar_prefetch=2,
        grid=(n_rows // EXPERT_TILE,),
        in_specs=[
            pl.BlockSpec((EXPERT_TILE, PACKED), lambda i, te, tr: (i, 0)),
            weights((D_MODEL, D_FF)), weights((D_MODEL, D_FF)), weights((D_FF, D_MODEL)),
        ],
        out_specs=pl.BlockSpec((EXPERT_TILE, PACKED), lambda i, te, tr: (i, 0)),
    )
    return pl.pallas_call(
        _expert_kernel,
        grid_spec=grid_spec,
        out_shape=jax.ShapeDtypeStruct((n_rows, PACKED), I32),
        compiler_params=_params("arbitrary"),
        name="moe_experts",
    )(tile_e, tile_rows, xs, wg_bf, wu_bf, wd_bf)


def _final_kernel(gf_ref, *rest):
    n = MOE_PARTS
    x_refs, y_refs, sel_g_refs, o_ref = rest[:n], rest[n:2 * n], rest[2 * n:3 * n], rest[-1]
    w = ROUTE_WINDOW
    part = pl.program_id(0) // (pl.num_programs(0) // n)

    def own(refs, index):
        value = refs[0][index]
        for p in range(1, n):
            value = jnp.where(part == p, refs[p][index], value)
        return value

    diag = (lax.broadcasted_iota(jnp.int32, (w, w), 0)
            == lax.broadcasted_iota(jnp.int32, (w, w), 1))
    acc = own(x_refs, (slice(None), slice(None)))
    for k in range(TOP_K):
        sel_g = own(sel_g_refs, (slice(k, k + 1), slice(None)))
        gate = jnp.sum(jnp.where(diag, sel_g, 0.0), axis=1, keepdims=True)
        acc = acc + gate * _unpack_bf16_pair(own(y_refs, k))
    o_ref[...] = _rmsnorm_f32(acc, gf_ref[...])


def _final(x_parts, y_parts, sel_g_parts, g_final):
    tp = x_parts[0].shape[0]
    w = ROUTE_WINDOW
    per_part = tp // w

    def window(p):
        return lambda i: jnp.clip(i - p * per_part, 0, per_part - 1)

    parts = range(MOE_PARTS)
    x_specs = [pl.BlockSpec((w, D_MODEL), lambda i, f=window(p): (f(i), 0)) for p in parts]
    y_specs = [pl.BlockSpec((TOP_K, w, PACKED), lambda i, f=window(p): (0, f(i), 0)) for p in parts]
    g_specs = [pl.BlockSpec((TOP_K, w), lambda i, f=window(p): (0, f(i))) for p in parts]
    return pl.pallas_call(
        _final_kernel,
        grid=(MOE_PARTS * per_part,),
        in_specs=[pl.BlockSpec((1, D_MODEL), lambda i: (0, 0))] + x_specs + y_specs + g_specs,
        out_specs=pl.BlockSpec((w, D_MODEL), lambda i: (i, 0)),
        out_shape=jax.ShapeDtypeStruct((MOE_PARTS * tp, D_MODEL), F32),
        compiler_params=_params("parallel"),
        name="moe_final",
    )(g_final, *x_parts, *y_parts, *sel_g_parts)


def _rotary_tables(seq):
    inv_freq = ROPE_BASE ** (-jnp.arange(0, RET_QK_DIM, 2, dtype=F32) / RET_QK_DIM)
    ang = jnp.arange(seq, dtype=F32)[:, None] * inv_freq[None, :]
    return jnp.cos(ang), jnp.sin(ang)


def _decay_tables():
    c = RET_CHUNK
    log_gamma = jnp.log(1.0 - 2.0 ** (-5.0 - jnp.arange(RET_HEADS, dtype=F32)))
    idx = jnp.arange(c, dtype=F32)
    diff = idx[:, None] - idx[None, :]
    causal = diff >= 0
    intra = jnp.where(causal[None],
                      jnp.exp(jnp.where(causal, diff, 0.0)[None] * log_gamma[:, None, None]), 0.0)
    qdec = jnp.exp((idx[None, :] + 1.0) * log_gamma[:, None])[..., None]
    kdec = jnp.exp((c - 1.0 - idx[None, :]) * log_gamma[:, None])[..., None]
    cdec = jnp.exp(c * log_gamma)[:, None, None]
    return intra, qdec, kdec, cdec


def kernel(x, norm_mix0, ret_w_in, ret_w_out, norm_ffn0, ffn_w_gate, ffn_w_up, ffn_w_down,
           norm_mix1, conv_w_in, conv_w, conv_w_out, norm_ffn1, moe_router, moe_w_gate,
           moe_w_up, moe_w_down, norm_final):
    b, seq, d = x.shape
    t = b * seq
    row = lambda g: g.reshape(1, d).astype(F32)
    bf = lambda w: w.astype(BF16)

    cos, sin = _rotary_tables(seq)
    intra, qdec, kdec, cdec = _decay_tables()

    proj = _ret_in(x.reshape(t, d), row(norm_mix0), bf(ret_w_in), cos, sin, seq)
    h, expert_w = _ret_core(proj, x, bf(ret_w_out), intra, qdec, kdec, cdec,
                            (moe_w_gate, moe_w_up, moe_w_down))
    h = _ffn(h.reshape(t, d), row(norm_ffn0), bf(ffn_w_gate), bf(ffn_w_up), bf(ffn_w_down))
    h = h.reshape(b, seq, d)

    conv_args = (row(norm_mix1), bf(conv_w_in), conv_w.astype(F32), bf(conv_w_out),
                 row(norm_ffn1), moe_router.astype(F32).T)
    routed = [_conv_route(h, part, *conv_args) for part in range(MOE_PARTS)]

    tp = t // MOE_PARTS
    n_rows = (TOP_K * tp // EXPERT_TILE + N_EXPERTS) * EXPERT_TILE
    x_parts, y_parts, gate_parts = [], [], []
    for h_p, xn, sel_e, sel_r, sel_g, cend in routed:
        tile_e, tile_rows, dest = _routing_plan(cend, sel_e, sel_r, tp)
        xs = _dispatch_rows(xn, dest, n_rows)
        y = _experts(tile_e, tile_rows, xs, *expert_w)
        x_parts.append(h_p.reshape(tp, d))
        y_parts.append(_collect_rows(y, dest))
        gate_parts.append(sel_g)
    out = _final(x_parts, y_parts, gate_parts, row(norm_final))
    return out.reshape(b, seq, d)
```

```python
import functools

import jax
import jax.numpy as jnp
from jax import lax
from jax.experimental import pallas as pl
from jax.experimental.pallas import tpu as pltpu
from jax.experimental.pallas import tpu_sc as plsc

D_MODEL = 1024
RET_HEADS = 4
RET_QK_DIM = D_MODEL // RET_HEADS
RET_V_DIM = 2 * RET_QK_DIM
RET_QK_TOTAL = RET_HEADS * RET_QK_DIM
RET_V_TOTAL = RET_HEADS * RET_V_DIM
RET_IN_COLS = 2 * RET_QK_TOTAL + 2 * RET_V_TOTAL
ROPE_BASE = 10000.0
CONV_WIDTH = 3
D_FF = 7 * D_MODEL // 2
N_EXPERTS = 8
TOP_K = 2
NORM_EPS = 1e-6

RET_CHUNK = 256
HALF = RET_QK_DIM // 2

VMEM_LIMIT_BYTES = 56 * 1024 * 1024
SC_CORES = 2
SC_SUBCORES = 16
SC_WORKERS = SC_CORES * SC_SUBCORES
SC_ROWS = 64

TOKEN_TILE = 1024
RET_SEQ_TILE = 512
CONV_SEQ_TILE = 512
FF_TILE = 1792
ROW_SUB = 512
IN_TOKEN_TILE = 512
IN_COL_TILE = 1024
ROUTE_WINDOW = 512
EXPERT_TILE = 1024
MOE_PARTS = 2
LANES = 128
BF16_SUBLANES = 16
SIDE_SLAB_BYTES = 2 * 1024 * 1024
PACKED = D_MODEL // 2

BF16 = jnp.bfloat16
F32 = jnp.float32
I32 = jnp.int32


def _params(*semantics):
    return pltpu.CompilerParams(dimension_semantics=semantics,
                                vmem_limit_bytes=VMEM_LIMIT_BYTES)


def _rmsnorm_f32(x, g):
    return x * lax.rsqrt(jnp.mean(x * x, axis=-1, keepdims=True) + NORM_EPS) * g


def _silu(x):
    return x * (1.0 / (1.0 + jnp.exp(-x)))


def _pack_bf16_pair(x):
    half = x.shape[1] // 2
    hi = lax.bitcast_convert_type(x[:, :half].astype(BF16).astype(F32), I32)
    lo = lax.bitcast_convert_type(x[:, half:].astype(BF16).astype(F32), I32)
    return hi | lax.shift_right_logical(lo, 16)


def _unpack_bf16_pair(w):
    hi = lax.bitcast_convert_type(w & jnp.int32(-65536), F32)
    lo = lax.bitcast_convert_type(lax.shift_left(w, 16), F32)
    return jnp.concatenate([hi, lo], axis=1)


def _ret_in_kernel(x_ref, g_ref, w_ref, cos_ref, sin_ref, o_ref):
    xn = _rmsnorm_f32(x_ref[...], g_ref[...]).astype(BF16)
    cos = cos_ref[...]
    sin = sin_ref[...]
    for n in range(RET_IN_COLS // IN_COL_TILE):
        cols = slice(n * IN_COL_TILE, (n + 1) * IN_COL_TILE)
        acc = jnp.dot(xn, w_ref[:, cols], preferred_element_type=F32)
        if n * IN_COL_TILE >= 2 * RET_QK_TOTAL:
            o_ref[:, cols] = acc.astype(BF16)
            continue
        scale = 1.0 if n * IN_COL_TILE < RET_QK_TOTAL else RET_QK_DIM ** -0.5
        for h in range(IN_COL_TILE // RET_QK_DIM):
            lo = h * RET_QK_DIM
            t1 = acc[:, lo:lo + HALF]
            t2 = acc[:, lo + HALF:lo + RET_QK_DIM]
            base = n * IN_COL_TILE + lo
            o_ref[:, base:base + HALF] = ((t1 * cos - t2 * sin) * scale).astype(BF16)
            o_ref[:, base + HALF:base + RET_QK_DIM] = ((t1 * sin + t2 * cos) * scale).astype(BF16)


def _ret_in(x2, g, w_bf, cos, sin, seq):
    t = x2.shape[0]
    tm = IN_TOKEN_TILE
    seq_tiles = seq // tm
    return pl.pallas_call(
        _ret_in_kernel,
        grid=(t // tm,),
        in_specs=[
            pl.BlockSpec((tm, D_MODEL), lambda i: (i, 0)),
            pl.BlockSpec((1, D_MODEL), lambda i: (0, 0)),
            pl.BlockSpec((D_MODEL, RET_IN_COLS), lambda i: (0, 0), pipeline_mode=pl.Buffered(1)),
            pl.BlockSpec((tm, HALF), lambda i: (i % seq_tiles, 0)),
            pl.BlockSpec((tm, HALF), lambda i: (i % seq_tiles, 0)),
        ],
        out_specs=pl.BlockSpec((tm, RET_IN_COLS), lambda i: (i, 0)),
        out_shape=jax.ShapeDtypeStruct((t, RET_IN_COLS), BF16),
        compiler_params=_params("parallel"),
        name="ret_in_proj",
    )(x2, g, w_bf, cos, sin)


def _ret_core_kernel(q_ref, k_ref, v_ref, gate_ref, x_ref, wout_ref,
                     intra_ref, qdec_ref, kdec_ref, cdec_ref, *rest):
    n_side = (len(rest) - 2) // 2
    side_in, o_ref, side_out, state_ref = (rest[:n_side], rest[n_side],
                                           rest[n_side + 1:2 * n_side + 1], rest[-1])

    for src, dst in zip(side_in, side_out):
        dst[...] = src[...].astype(BF16)

    @pl.when(pl.program_id(1) == 0)
    def _():
        state_ref[...] = jnp.zeros_like(state_ref)

    gated = []
    for c in range(RET_SEQ_TILE // RET_CHUNK):
        rows = pl.ds(c * RET_CHUNK, RET_CHUNK)
        heads = []
        for h in range(RET_HEADS):
            qk_cols = slice(h * RET_QK_DIM, (h + 1) * RET_QK_DIM)
            v_cols = slice(h * RET_V_DIM, (h + 1) * RET_V_DIM)
            q = q_ref[rows, qk_cols]
            k = k_ref[rows, qk_cols]
            v = v_ref[rows, v_cols]
            state = state_ref[h]
            scores = lax.dot_general(q, k, (((1,), (1,)), ((), ())),
                                     preferred_element_type=F32) * intra_ref[h]
            inner = jnp.dot(scores.astype(BF16), v, preferred_element_type=F32)
            qd = (q.astype(F32) * qdec_ref[h]).astype(BF16)
            cross = jnp.dot(qd, state.astype(BF16), preferred_element_type=F32)
            kd = (k.astype(F32) * kdec_ref[h]).astype(BF16)
            state_ref[h] = state * cdec_ref[h] + lax.dot_general(
                kd, v, (((0,), (0,)), ((), ())), preferred_element_type=F32)
            o = inner + cross
            o = o * lax.rsqrt(jnp.mean(o * o, axis=-1, keepdims=True) + NORM_EPS)
            gate = gate_ref[rows, v_cols].astype(F32)
            heads.append((_silu(gate) * o).astype(BF16))
        gated.append(jnp.concatenate(heads, axis=1))
    og = jnp.concatenate(gated, axis=0)
    o_ref[0] = x_ref[0] + jnp.dot(og, wout_ref[...], preferred_element_type=F32)


def _ret_core(proj, x, wout_bf, intra, qdec, kdec, cdec, side_weights):
    b, seq, _ = x.shape
    ts = RET_SEQ_TILE
    st = seq // ts
    steps = b * st
    whole = lambda a: pl.BlockSpec(a.shape, lambda bi, s: (0,) * a.ndim)

    def slab_rows(wt):
        rows = wt.shape[0] * wt.shape[1]
        per_step = rows // steps
        ok = (rows % steps == 0 and per_step % BF16_SUBLANES == 0 and wt.shape[1] % per_step == 0
              and per_step * wt.shape[2] * 4 <= SIDE_SLAB_BYTES)
        return per_step if ok else None

    hosted = all(slab_rows(wt) is not None for wt in side_weights)
    side = tuple(side_weights) if hosted else ()
    side_specs = []
    for wt in side:
        per_step = slab_rows(wt)
        per_mat = wt.shape[1] // per_step
        side_specs.append(pl.BlockSpec(
            (1, per_step, wt.shape[2]),
            lambda bi, s, per_mat=per_mat: ((bi * st + s) // per_mat, (bi * st + s) % per_mat, 0)))

    outs = pl.pallas_call(
        _ret_core_kernel,
        grid=(b, st),
        in_specs=[
            pl.BlockSpec((ts, RET_QK_TOTAL), lambda bi, s: (bi * st + s, 0)),
            pl.BlockSpec((ts, RET_QK_TOTAL), lambda bi, s: (bi * st + s, 1)),
            pl.BlockSpec((ts, RET_V_TOTAL), lambda bi, s: (bi * st + s, 1)),
            pl.BlockSpec((ts, RET_V_TOTAL), lambda bi, s: (bi * st + s, 2)),
            pl.BlockSpec((1, ts, D_MODEL), lambda bi, s: (bi, s, 0)),
            pl.BlockSpec((RET_V_TOTAL, D_MODEL), lambda bi, s: (0, 0), pipeline_mode=pl.Buffered(1)),
            whole(intra), whole(qdec), whole(kdec), whole(cdec),
        ] + side_specs,
        out_specs=[pl.BlockSpec((1, ts, D_MODEL), lambda bi, s: (bi, s, 0))] + side_specs,
        out_shape=[jax.ShapeDtypeStruct(x.shape, F32)]
                  + [jax.ShapeDtypeStruct(wt.shape, BF16) for wt in side],
        scratch_shapes=[pltpu.VMEM((RET_HEADS, RET_QK_DIM, RET_V_DIM), F32)],
        compiler_params=_params("arbitrary", "arbitrary"),
        name="ret_core",
    )(proj, proj, proj, proj, x, wout_bf, intra, qdec, kdec, cdec, *side)
    if hosted:
        return outs[0], tuple(outs[1:])
    return outs[0], tuple(wt.astype(BF16) for wt in side_weights)


def _swiglu(x_bf, wg_ref, wu_ref, wd_ref):
    out = None
    for c in range(D_FF // FF_TILE):
        cols = slice(c * FF_TILE, (c + 1) * FF_TILE)
        hg = jnp.dot(x_bf, wg_ref[:, cols], preferred_element_type=F32)
        hu = jnp.dot(x_bf, wu_ref[:, cols], preferred_element_type=F32)
        hid = (_silu(hg) * hu).astype(BF16)
        part = jnp.dot(hid, wd_ref[cols, :], preferred_element_type=F32)
        out = part if out is None else out + part
    return out


def _ffn_kernel(x_ref, g_ref, wg_ref, wu_ref, wd_ref, o_ref):
    for r in range(TOKEN_TILE // ROW_SUB):
        rows = pl.ds(r * ROW_SUB, ROW_SUB)
        x = x_ref[rows, :]
        xn = _rmsnorm_f32(x, g_ref[...]).astype(BF16)
        o_ref[rows, :] = x + _swiglu(xn, wg_ref, wu_ref, wd_ref)


def _ffn(x2, g, wg_bf, wu_bf, wd_bf):
    t = x2.shape[0]
    resident = lambda shape: pl.BlockSpec(shape, lambda i: (0, 0), pipeline_mode=pl.Buffered(1))
    return pl.pallas_call(
        _ffn_kernel,
        grid=(t // TOKEN_TILE,),
        in_specs=[
            pl.BlockSpec((TOKEN_TILE, D_MODEL), lambda i: (i, 0)),
            pl.BlockSpec((1, D_MODEL), lambda i: (0, 0)),
            resident((D_MODEL, D_FF)), resident((D_MODEL, D_FF)), resident((D_FF, D_MODEL)),
        ],
        out_specs=pl.BlockSpec((TOKEN_TILE, D_MODEL), lambda i: (i, 0)),
        out_shape=jax.ShapeDtypeStruct(x2.shape, F32),
        compiler_params=_params("parallel"),
        name="ffn_swiglu",
    )(x2, g, wg_bf, wu_bf, wd_bf)


def _split_bf16(x):
    hi = x.astype(BF16)
    return hi, (x - hi.astype(F32)).astype(BF16)


def _route_normalise(h, g_ref, xn_ref):
    xn = _rmsnorm_f32(h, g_ref[...])
    xn_ref[...] = _pack_bf16_pair(xn)
    return _split_bf16(xn)


def _route_select(x_hi, x_lo, wrt_ref, before_ref, count_ref,
                  sel_e_ref, sel_r_ref, sel_g_ref, cend_ref):
    w_hi, w_lo = _split_bf16(wrt_ref[...])
    nt = lambda a, b: lax.dot_general(a, b, (((1,), (1,)), ((), ())), preferred_element_type=F32)
    both = nt(jnp.concatenate([w_hi, w_lo], axis=0), x_hi)
    logits = both[:N_EXPERTS] + (both[N_EXPERTS:] + nt(w_hi, x_lo))
    sub = lax.broadcasted_iota(jnp.int32, logits.shape, 0).astype(F32)
    none = float(N_EXPERTS)
    m1 = jnp.max(logits, axis=0, keepdims=True)
    i1 = jnp.min(jnp.where(logits == m1, sub, none), axis=0, keepdims=True)
    rest = jnp.where(sub == i1, -jnp.inf, logits)
    m2 = jnp.max(rest, axis=0, keepdims=True)
    i2 = jnp.min(jnp.where(rest == m2, sub, none), axis=0, keepdims=True)
    e2 = jnp.exp(m2 - m1)
    denom = 1.0 + e2

    chosen = jnp.logical_or(sub == i1, sub == i2)
    prefix = jnp.dot(jnp.where(chosen, 1.0, 0.0).astype(BF16), before_ref[...],
                     preferred_element_type=F32)
    count = count_ref[...]
    rank = prefix + count[:, :1]
    r1 = jnp.sum(jnp.where(sub == i1, rank, 0.0), axis=0, keepdims=True)
    r2 = jnp.sum(jnp.where(sub == i2, rank, 0.0), axis=0, keepdims=True)
    sel_e_ref[0:1, :] = i1.astype(I32)
    sel_e_ref[1:2, :] = i2.astype(I32)
    sel_r_ref[0:1, :] = r1.astype(I32)
    sel_r_ref[1:2, :] = r2.astype(I32)
    sel_g_ref[0:1, :] = 1.0 / denom
    sel_g_ref[1:2, :] = e2 / denom
    count = count + jnp.sum(jnp.where(chosen, 1.0, 0.0), axis=1, keepdims=True)
    count_ref[...] = count
    cend_ref[0] = count


def _conv_route_kernel(x_ref, g_ref, win_ref, wc_ref, wout_ref, g2_ref, wrt_ref, before_ref,
                       o_ref, xn_ref, sel_e_ref, sel_r_ref, sel_g_ref, cend_ref,
                       carry_ref, count_ref):
    s = pl.program_id(1)
    ts = CONV_SEQ_TILE

    @pl.when(s == 0)
    def _():
        carry_ref[...] = jnp.zeros_like(carry_ref)

    @pl.when(jnp.logical_and(pl.program_id(0) == 0, s == 0))
    def _():
        count_ref[...] = jnp.zeros_like(count_ref)

    x = x_ref[0]
    xn = _rmsnorm_f32(x, g_ref[...]).astype(BF16)
    proj = jnp.dot(xn, win_ref[...], preferred_element_type=F32)
    gate_b = proj[:, :D_MODEL]
    gate_c = proj[:, D_MODEL:2 * D_MODEL]
    hid = proj[:, 2 * D_MODEL:]
    u = gate_c * hid

    row = lax.broadcasted_iota(jnp.int32, (ts, D_MODEL), 0)
    prev1 = carry_ref[7:8, :]
    prev2 = carry_ref[6:7, :]
    u1 = jnp.where(row == 0, prev1, pltpu.roll(u, 1, axis=0))
    u2 = jnp.where(row == 0, prev2, jnp.where(row == 1, prev1, pltpu.roll(u, 2, axis=0)))
    carry_ref[...] = u[ts - 8:, :]

    wc = wc_ref[...]
    y = wc[2:3, :] * u + wc[1:2, :] * u1 + wc[0:1, :] * u2
    y = (gate_b * y).astype(BF16)
    h = x + jnp.dot(y, wout_ref[...], preferred_element_type=F32)
    o_ref[0] = h
    x_hi, x_lo = _route_normalise(h, g2_ref, xn_ref)
    _route_select(x_hi, x_lo, wrt_ref, before_ref, count_ref,
                  sel_e_ref, sel_r_ref, sel_g_ref, cend_ref)


def _conv_route(x, part, g, win_bf, wc, wout_bf, g2, wrt):
    b, seq, _ = x.shape
    ts = CONV_SEQ_TILE
    w = ROUTE_WINDOW
    assert ts == w and b % MOE_PARTS == 0
    bp = b // MOE_PARTS
    b0 = part * bp
    st = seq // ts
    steps = bp * st
    tp = bp * seq
    flat = lambda bi, s: bi * st + s
    resident = lambda shape: pl.BlockSpec(shape, lambda bi, s: (0, 0), pipeline_mode=pl.Buffered(1))
    before = (jnp.arange(w)[:, None] < jnp.arange(w)[None, :]).astype(BF16)

    per_token = lambda dtype: jax.ShapeDtypeStruct((TOP_K, tp), dtype)
    return pl.pallas_call(
        _conv_route_kernel,
        grid=(bp, st),
        in_specs=[
            pl.BlockSpec((1, ts, D_MODEL), lambda bi, s: (b0 + bi, s, 0)),
            pl.BlockSpec((1, D_MODEL), lambda bi, s: (0, 0)),
            resident((D_MODEL, 3 * D_MODEL)),
            pl.BlockSpec((CONV_WIDTH, D_MODEL), lambda bi, s: (0, 0)),
            resident((D_MODEL, D_MODEL)),
            pl.BlockSpec((1, D_MODEL), lambda bi, s: (0, 0)),
            pl.BlockSpec((N_EXPERTS, D_MODEL), lambda bi, s: (0, 0)),
            resident((w, w)),
        ],
        out_specs=[
            pl.BlockSpec((1, ts, D_MODEL), lambda bi, s: (bi, s, 0)),
            pl.BlockSpec((w, PACKED), lambda bi, s: (flat(bi, s), 0)),
            pl.BlockSpec((TOP_K, w), lambda bi, s: (0, flat(bi, s))),
            pl.BlockSpec((TOP_K, w), lambda bi, s: (0, flat(bi, s))),
            pl.BlockSpec((TOP_K, w), lambda bi, s: (0, flat(bi, s))),
            pl.BlockSpec((1, N_EXPERTS, LANES), lambda bi, s: (flat(bi, s), 0, 0)),
        ],
        out_shape=[jax.ShapeDtypeStruct((bp, seq, D_MODEL), F32),
                   jax.ShapeDtypeStruct((tp, PACKED), I32),
                   per_token(I32), per_token(I32), per_token(F32),
                   jax.ShapeDtypeStruct((steps, N_EXPERTS, LANES), F32)],
        scratch_shapes=[pltpu.VMEM((8, D_MODEL), F32), pltpu.VMEM((N_EXPERTS, LANES), F32)],
        compiler_params=_params("arbitrary", "arbitrary"),
        name="conv_route",
    )(x, g, win_bf, wc, wout_bf, g2, wrt, before)


def _routing_plan(cend, sel_e, sel_r, t):
    counts = cend[-1, :, 0].astype(I32)
    ntiles = (counts + EXPERT_TILE - 1) // EXPERT_TILE
    tile_end = jnp.cumsum(ntiles)
    tile_start = tile_end - ntiles
    off = tile_start * EXPERT_TILE

    nt_max = TOP_K * t // EXPERT_TILE + N_EXPERTS
    ti = jnp.arange(nt_max, dtype=I32)
    tile_e = jnp.minimum(jnp.sum(tile_end[None, :] <= ti[:, None], axis=1).astype(I32),
                         N_EXPERTS - 1)
    tile_rows = jnp.clip(counts[tile_e] - (ti - tile_start[tile_e]) * EXPERT_TILE, 0, EXPERT_TILE)
    tile_rows = jnp.where(ti < tile_end[-1], tile_rows, 0)

    first_row = jnp.zeros_like(sel_e)
    for e in range(N_EXPERTS):
        first_row = first_row + jnp.where(sel_e == e, off[e], 0)
    dest = first_row + sel_r
    return tile_e, tile_rows, dest


def _sc_mesh():
    return plsc.VectorSubcoreMesh(core_axis_name="c", subcore_axis_name="s",
                                  num_cores=SC_CORES, num_subcores=SC_SUBCORES)


def _dispatch_rows(xn, dest, n_rows):
    t, d = xn.shape
    per_worker = t // SC_WORKERS
    chunks = per_worker // SC_ROWS
    dest4 = dest.reshape(TOP_K, SC_WORKERS, chunks, SC_ROWS)

    @functools.partial(
        pl.kernel, mesh=_sc_mesh(),
        out_type=jax.ShapeDtypeStruct((n_rows, d), xn.dtype),
        scratch_types=[pltpu.VMEM((TOP_K, chunks, SC_ROWS), I32),
                       pltpu.VMEM((SC_ROWS, d), xn.dtype)],
        name="moe_dispatch",
    )
    def run(x_hbm, dest_hbm, o_hbm, dest_v, rows_v):
        worker = lax.axis_index("s") * SC_CORES + lax.axis_index("c")
        for k in range(TOP_K):
            pltpu.sync_copy(dest_hbm.at[k, worker], dest_v.at[k])

        @pl.loop(0, chunks)
        def _(j):
            pltpu.sync_copy(x_hbm.at[pl.ds(worker * per_worker + j * SC_ROWS, SC_ROWS)], rows_v)
            for k in range(TOP_K):
                pltpu.sync_copy(rows_v, o_hbm.at[dest_v.at[k, j]])

    return run(xn, dest4)


def _collect_rows(y, dest):
    t = dest.shape[1]
    d = y.shape[1]
    per_worker = t // SC_WORKERS
    chunks = per_worker // SC_ROWS
    dest4 = dest.reshape(TOP_K, SC_WORKERS, chunks, SC_ROWS)

    @functools.partial(
        pl.kernel, mesh=_sc_mesh(),
        out_type=jax.ShapeDtypeStruct((TOP_K, t, d), y.dtype),
        scratch_types=[pltpu.VMEM((TOP_K, chunks, SC_ROWS), I32),
                       pltpu.VMEM((SC_ROWS, d), y.dtype)],
        name="moe_collect",
    )
    def run(y_hbm, dest_hbm, o_hbm, dest_v, rows_v):
        worker = lax.axis_index("s") * SC_CORES + lax.axis_index("c")
        for k in range(TOP_K):
            pltpu.sync_copy(dest_hbm.at[k, worker], dest_v.at[k])

        @pl.loop(0, chunks)
        def _(j):
            for k in range(TOP_K):
                pltpu.sync_copy(y_hbm.at[dest_v.at[k, j]], rows_v)
                pltpu.sync_copy(rows_v, o_hbm.at[k, pl.ds(worker * per_worker + j * SC_ROWS, SC_ROWS)])

    return run(y, dest4)


def _expert_kernel(tile_e_s, tile_rows_s, xs_ref, wg_ref, wu_ref, wd_ref, y_ref):
    n_rows = tile_rows_s[pl.program_id(0)]
    n_sub = EXPERT_TILE // ROW_SUB

    def run(sub_blocks):
        for r in range(sub_blocks):
            rows = pl.ds(r * ROW_SUB, ROW_SUB)
            real = lax.broadcasted_iota(jnp.int32, (ROW_SUB, PACKED), 0) < n_rows - r * ROW_SUB
            xs = _unpack_bf16_pair(jnp.where(real, xs_ref[rows, :], 0)).astype(BF16)
            y_ref[rows, :] = _pack_bf16_pair(_swiglu(xs, wg_ref.at[0], wu_ref.at[0], wd_ref.at[0]))
        if sub_blocks < n_sub:
            rest = pl.ds(sub_blocks * ROW_SUB, (n_sub - sub_blocks) * ROW_SUB)
            y_ref[rest, :] = jnp.zeros(((n_sub - sub_blocks) * ROW_SUB, PACKED), I32)

    for used in range(n_sub + 1):
        lo, hi = (used - 1) * ROW_SUB, used * ROW_SUB
        pl.when(jnp.logical_and(n_rows > lo, n_rows <= hi))(functools.partial(run, used))


def _experts(tile_e, tile_rows, xs, wg_bf, wu_bf, wd_bf):
    n_rows = xs.shape[0]
    weights = lambda shape: pl.BlockSpec((1,) + shape, lambda i, te, tr: (te[i], 0, 0),
                                         pipeline_mode=pl.Buffered(1))
    grid_spec = pltpu.PrefetchScalarGridSpec(
        num_scalar_prefetch=2,
        grid=(n_rows // EXPERT_TILE,),
        in_specs=[
            pl.BlockSpec((EXPERT_TILE, PACKED), lambda i, te, tr: (i, 0)),
            weights((D_MODEL, D_FF)), weights((D_MODEL, D_FF)), weights((D_FF, D_MODEL)),
        ],
        out_specs=pl.BlockSpec((EXPERT_TILE, PACKED), lambda i, te, tr: (i, 0)),
    )
    return pl.pallas_call(
        _expert_kernel,
        grid_spec=grid_spec,
        out_shape=jax.ShapeDtypeStruct((n_rows, PACKED), I32),
        compiler_params=_params("arbitrary"),
        name="moe_experts",
    )(tile_e, tile_rows, xs, wg_bf, wu_bf, wd_bf)


def _final_kernel(x_ref, y_ref, sel_g_ref, gf_ref, buf_ref, o_ref):
    del buf_ref
    w = ROUTE_WINDOW
    diag = (lax.broadcasted_iota(jnp.int32, (w, w), 0)
            == lax.broadcasted_iota(jnp.int32, (w, w), 1))
    acc = x_ref[...]
    for k in range(TOP_K):
        gate = jnp.sum(jnp.where(diag, sel_g_ref[k:k + 1, :], 0.0), axis=1, keepdims=True)
        acc = acc + gate * _unpack_bf16_pair(y_ref[k])
    o_ref[...] = _rmsnorm_f32(acc, gf_ref[...])


def _final(x_part, y_part, sel_g_part, g_final, buf, part):
    tp = x_part.shape[0]
    w = ROUTE_WINDOW
    first = part * (tp // w)
    return pl.pallas_call(
        _final_kernel,
        grid=(tp // w,),
        in_specs=[
            pl.BlockSpec((w, D_MODEL), lambda i: (i, 0)),
            pl.BlockSpec((TOP_K, w, PACKED), lambda i: (0, i, 0)),
            pl.BlockSpec((TOP_K, w), lambda i: (0, i)),
            pl.BlockSpec((1, D_MODEL), lambda i: (0, 0)),
            pl.BlockSpec(memory_space=pl.ANY),
        ],
        out_specs=pl.BlockSpec((w, D_MODEL), lambda i: (first + i, 0)),
        out_shape=jax.ShapeDtypeStruct(buf.shape, F32),
        input_output_aliases={4: 0},
        compiler_params=_params("parallel"),
        name="moe_final",
    )(x_part, y_part, sel_g_part, g_final, buf)


def _rotary_tables(seq):
    inv_freq = ROPE_BASE ** (-jnp.arange(0, RET_QK_DIM, 2, dtype=F32) / RET_QK_DIM)
    ang = jnp.arange(seq, dtype=F32)[:, None] * inv_freq[None, :]
    return jnp.cos(ang), jnp.sin(ang)


def _decay_tables():
    c = RET_CHUNK
    log_gamma = jnp.log(1.0 - 2.0 ** (-5.0 - jnp.arange(RET_HEADS, dtype=F32)))
    idx = jnp.arange(c, dtype=F32)
    diff = idx[:, None] - idx[None, :]
    causal = diff >= 0
    intra = jnp.where(causal[None],
                      jnp.exp(jnp.where(causal, diff, 0.0)[None] * log_gamma[:, None, None]), 0.0)
    qdec = jnp.exp((idx[None, :] + 1.0) * log_gamma[:, None])[..., None]
    kdec = jnp.exp((c - 1.0 - idx[None, :]) * log_gamma[:, None])[..., None]
    cdec = jnp.exp(c * log_gamma)[:, None, None]
    return intra, qdec, kdec, cdec


def kernel(x, norm_mix0, ret_w_in, ret_w_out, norm_ffn0, ffn_w_gate, ffn_w_up, ffn_w_down,
           norm_mix1, conv_w_in, conv_w, conv_w_out, norm_ffn1, moe_router, moe_w_gate,
           moe_w_up, moe_w_down, norm_final):
    b, seq, d = x.shape
    t = b * seq
    row = lambda g: g.reshape(1, d).astype(F32)
    bf = lambda w: w.astype(BF16)

    cos, sin = _rotary_tables(seq)
    intra, qdec, kdec, cdec = _decay_tables()

    proj = _ret_in(x.reshape(t, d), row(norm_mix0), bf(ret_w_in), cos, sin, seq)
    h, expert_w = _ret_core(proj, x, bf(ret_w_out), intra, qdec, kdec, cdec,
                            (moe_w_gate, moe_w_up, moe_w_down))
    h_ffn = _ffn(h.reshape(t, d), row(norm_ffn0), bf(ffn_w_gate), bf(ffn_w_up), bf(ffn_w_down))
    h = h_ffn.reshape(b, seq, d)

    conv_args = (row(norm_mix1), bf(conv_w_in), conv_w.astype(F32), bf(conv_w_out),
                 row(norm_ffn1), moe_router.astype(F32).T)
    routed = [_conv_route(h, part, *conv_args) for part in range(MOE_PARTS)]

    tp = t // MOE_PARTS
    n_rows = (TOP_K * tp // EXPERT_TILE + N_EXPERTS) * EXPERT_TILE
    collected = []
    for h_p, xn, sel_e, sel_r, sel_g, cend in routed:
        tile_e, tile_rows, dest = _routing_plan(cend, sel_e, sel_r, tp)
        xs = _dispatch_rows(xn, dest, n_rows)
        y = _experts(tile_e, tile_rows, xs, *expert_w)
        collected.append((h_p.reshape(tp, d), _collect_rows(y, dest), sel_g))
    out = h_ffn
    for part, (h_p, y_tok, sel_g) in enumerate(collected):
        out = _final(h_p, y_tok, sel_g, row(norm_final), out, part)
    return out.reshape(b, seq, d)
```

```python
import functools

import jax
import jax.numpy as jnp
from jax import lax
from jax.experimental import pallas as pl
from jax.experimental.pallas import tpu as pltpu
from jax.experimental.pallas import tpu_sc as plsc

D_MODEL = 1024
RET_HEADS = 4
RET_QK_DIM = D_MODEL // RET_HEADS
RET_V_DIM = 2 * RET_QK_DIM
RET_QK_TOTAL = RET_HEADS * RET_QK_DIM
RET_V_TOTAL = RET_HEADS * RET_V_DIM
RET_IN_COLS = 2 * RET_QK_TOTAL + 2 * RET_V_TOTAL
ROPE_BASE = 10000.0
CONV_WIDTH = 3
D_FF = 7 * D_MODEL // 2
N_EXPERTS = 8
TOP_K = 2
NORM_EPS = 1e-6

RET_CHUNK = 256
HALF = RET_QK_DIM // 2

VMEM_LIMIT_BYTES = 56 * 1024 * 1024
SC_CORES = 2
SC_SUBCORES = 16
SC_WORKERS = SC_CORES * SC_SUBCORES
SC_ROWS = 64

TOKEN_TILE = 1024
RET_SEQ_TILE = 512
CONV_SEQ_TILE = 512
FF_TILE = 1792
ROW_SUB = 512
IN_TOKEN_TILE = 512
IN_COL_TILE = 1024
ROUTE_WINDOW = 512
EXPERT_TILE = 1024
MOE_PARTS = 2
LANES = 128
BF16_SUBLANES = 16
SIDE_SLAB_BYTES = 2 * 1024 * 1024
PACKED = D_MODEL // 2

BF16 = jnp.bfloat16
F32 = jnp.float32
I32 = jnp.int32


def _params(*semantics):
    return pltpu.CompilerParams(dimension_semantics=semantics,
                                vmem_limit_bytes=VMEM_LIMIT_BYTES)


def _rmsnorm_f32(x, g):
    return x * lax.rsqrt(jnp.mean(x * x, axis=-1, keepdims=True) + NORM_EPS) * g


def _silu(x):
    return x * (1.0 / (1.0 + jnp.exp(-x)))


def _pack_bf16_pair(x):
    half = x.shape[1] // 2
    hi = lax.bitcast_convert_type(x[:, :half].astype(BF16).astype(F32), I32)
    lo = lax.bitcast_convert_type(x[:, half:].astype(BF16).astype(F32), I32)
    return hi | lax.shift_right_logical(lo, 16)


def _unpack_bf16_pair(w):
    hi = lax.bitcast_convert_type(w & jnp.int32(-65536), F32)
    lo = lax.bitcast_convert_type(lax.shift_left(w, 16), F32)
    return jnp.concatenate([hi, lo], axis=1)


def _ret_in_kernel(x_ref, g_ref, w_ref, cos_ref, sin_ref, o_ref):
    xn = _rmsnorm_f32(x_ref[...], g_ref[...]).astype(BF16)
    cos = cos_ref[...]
    sin = sin_ref[...]
    for n in range(RET_IN_COLS // IN_COL_TILE):
        cols = slice(n * IN_COL_TILE, (n + 1) * IN_COL_TILE)
        acc = jnp.dot(xn, w_ref[:, cols], preferred_element_type=F32)
        if n * IN_COL_TILE >= 2 * RET_QK_TOTAL:
            o_ref[:, cols] = acc.astype(BF16)
            continue
        scale = 1.0 if n * IN_COL_TILE < RET_QK_TOTAL else RET_QK_DIM ** -0.5
        for h in range(IN_COL_TILE // RET_QK_DIM):
            lo = h * RET_QK_DIM
            t1 = acc[:, lo:lo + HALF]
            t2 = acc[:, lo + HALF:lo + RET_QK_DIM]
            base = n * IN_COL_TILE + lo
            o_ref[:, base:base + HALF] = ((t1 * cos - t2 * sin) * scale).astype(BF16)
            o_ref[:, base + HALF:base + RET_QK_DIM] = ((t1 * sin + t2 * cos) * scale).astype(BF16)


def _ret_in(x2, g, w_bf, cos, sin, seq):
    t = x2.shape[0]
    tm = IN_TOKEN_TILE
    seq_tiles = seq // tm
    return pl.pallas_call(
        _ret_in_kernel,
        grid=(t // tm,),
        in_specs=[
            pl.BlockSpec((tm, D_MODEL), lambda i: (i, 0)),
            pl.BlockSpec((1, D_MODEL), lambda i: (0, 0)),
            pl.BlockSpec((D_MODEL, RET_IN_COLS), lambda i: (0, 0), pipeline_mode=pl.Buffered(1)),
            pl.BlockSpec((tm, HALF), lambda i: (i % seq_tiles, 0)),
            pl.BlockSpec((tm, HALF), lambda i: (i % seq_tiles, 0)),
        ],
        out_specs=pl.BlockSpec((tm, RET_IN_COLS), lambda i: (i, 0)),
        out_shape=jax.ShapeDtypeStruct((t, RET_IN_COLS), BF16),
        compiler_params=_params("parallel"),
        name="ret_in_proj",
    )(x2, g, w_bf, cos, sin)


def _ret_core_kernel(q_ref, k_ref, v_ref, gate_ref, x_ref, wout_ref,
                     intra_ref, qdec_ref, kdec_ref, cdec_ref, *rest):
    n_side = (len(rest) - 2) // 2
    side_in, o_ref, side_out, state_ref = (rest[:n_side], rest[n_side],
                                           rest[n_side + 1:2 * n_side + 1], rest[-1])

    for src, dst in zip(side_in, side_out):
        dst[...] = src[...].astype(BF16)

    @pl.when(pl.program_id(1) == 0)
    def _():
        state_ref[...] = jnp.zeros_like(state_ref)

    gated = []
    for c in range(RET_SEQ_TILE // RET_CHUNK):
        rows = pl.ds(c * RET_CHUNK, RET_CHUNK)
        heads = []
        for h in range(RET_HEADS):
            qk_cols = slice(h * RET_QK_DIM, (h + 1) * RET_QK_DIM)
            v_cols = slice(h * RET_V_DIM, (h + 1) * RET_V_DIM)
            q = q_ref[rows, qk_cols]
            k = k_ref[rows, qk_cols]
            v = v_ref[rows, v_cols]
            state = state_ref[h]
            scores = lax.dot_general(q, k, (((1,), (1,)), ((), ())),
                                     preferred_element_type=F32) * intra_ref[h]
            inner = jnp.dot(scores.astype(BF16), v, preferred_element_type=F32)
            qd = (q.astype(F32) * qdec_ref[h]).astype(BF16)
            cross = jnp.dot(qd, state.astype(BF16), preferred_element_type=F32)
            kd = (k.astype(F32) * kdec_ref[h]).astype(BF16)
            state_ref[h] = state * cdec_ref[h] + lax.dot_general(
                kd, v, (((0,), (0,)), ((), ())), preferred_element_type=F32)
            o = inner + cross
            o = o * lax.rsqrt(jnp.mean(o * o, axis=-1, keepdims=True) + NORM_EPS)
            gate = gate_ref[rows, v_cols].astype(F32)
            heads.append((gate * (0.5 * jnp.tanh(0.5 * gate) + 0.5) * o).astype(BF16))
        gated.append(jnp.concatenate(heads, axis=1))
    og = jnp.concatenate(gated, axis=0)
    o_ref[0] = x_ref[0] + jnp.dot(og, wout_ref[...], preferred_element_type=F32)


def _ret_core(proj, x, wout_bf, intra, qdec, kdec, cdec, side_weights):
    b, seq, _ = x.shape
    ts = RET_SEQ_TILE
    st = seq // ts
    steps = b * st
    whole = lambda a: pl.BlockSpec(a.shape, lambda bi, s: (0,) * a.ndim)

    def slab_rows(wt):
        rows = wt.shape[0] * wt.shape[1]
        per_step = rows // steps
        ok = (rows % steps == 0 and per_step % BF16_SUBLANES == 0 and wt.shape[1] % per_step == 0
              and per_step * wt.shape[2] * 4 <= SIDE_SLAB_BYTES)
        return per_step if ok else None

    hosted = all(slab_rows(wt) is not None for wt in side_weights)
    side = tuple(side_weights) if hosted else ()
    side_specs = []
    for wt in side:
        per_step = slab_rows(wt)
        per_mat = wt.shape[1] // per_step
        side_specs.append(pl.BlockSpec(
            (1, per_step, wt.shape[2]),
            lambda bi, s, per_mat=per_mat: ((bi * st + s) // per_mat, (bi * st + s) % per_mat, 0)))

    outs = pl.pallas_call(
        _ret_core_kernel,
        grid=(b, st),
        in_specs=[
            pl.BlockSpec((ts, RET_QK_TOTAL), lambda bi, s: (bi * st + s, 0)),
            pl.BlockSpec((ts, RET_QK_TOTAL), lambda bi, s: (bi * st + s, 1)),
            pl.BlockSpec((ts, RET_V_TOTAL), lambda bi, s: (bi * st + s, 1)),
            pl.BlockSpec((ts, RET_V_TOTAL), lambda bi, s: (bi * st + s, 2)),
            pl.BlockSpec((1, ts, D_MODEL), lambda bi, s: (bi, s, 0)),
            pl.BlockSpec((RET_V_TOTAL, D_MODEL), lambda bi, s: (0, 0), pipeline_mode=pl.Buffered(1)),
            whole(intra), whole(qdec), whole(kdec), whole(cdec),
        ] + side_specs,
        out_specs=[pl.BlockSpec((1, ts, D_MODEL), lambda bi, s: (bi, s, 0))] + side_specs,
        out_shape=[jax.ShapeDtypeStruct(x.shape, F32)]
                  + [jax.ShapeDtypeStruct(wt.shape, BF16) for wt in side],
        scratch_shapes=[pltpu.VMEM((RET_HEADS, RET_QK_DIM, RET_V_DIM), F32)],
        compiler_params=_params("arbitrary", "arbitrary"),
        name="ret_core",
    )(proj, proj, proj, proj, x, wout_bf, intra, qdec, kdec, cdec, *side)
    if hosted:
        return outs[0], tuple(outs[1:])
    return outs[0], tuple(wt.astype(BF16) for wt in side_weights)


def _swiglu(x_bf, wg_ref, wu_ref, wd_ref):
    out = None
    for c in range(D_FF // FF_TILE):
        cols = slice(c * FF_TILE, (c + 1) * FF_TILE)
        hg = jnp.dot(x_bf, wg_ref[:, cols], preferred_element_type=F32)
        hu = jnp.dot(x_bf, wu_ref[:, cols], preferred_element_type=F32)
        hid = (_silu(hg) * hu).astype(BF16)
        part = jnp.dot(hid, wd_ref[cols, :], preferred_element_type=F32)
        out = part if out is None else out + part
    return out


def _ffn_kernel(x_ref, g_ref, wg_ref, wu_ref, wd_ref, o_ref):
    for r in range(TOKEN_TILE // ROW_SUB):
        rows = pl.ds(r * ROW_SUB, ROW_SUB)
        x = x_ref[rows, :]
        xn = _rmsnorm_f32(x, g_ref[...]).astype(BF16)
        o_ref[rows, :] = x + _swiglu(xn, wg_ref, wu_ref, wd_ref)


def _ffn(x2, g, wg_bf, wu_bf, wd_bf):
    t = x2.shape[0]
    resident = lambda shape: pl.BlockSpec(shape, lambda i: (0, 0), pipeline_mode=pl.Buffered(1))
    return pl.pallas_call(
        _ffn_kernel,
        grid=(t // TOKEN_TILE,),
        in_specs=[
            pl.BlockSpec((TOKEN_TILE, D_MODEL), lambda i: (i, 0)),
            pl.BlockSpec((1, D_MODEL), lambda i: (0, 0)),
            resident((D_MODEL, D_FF)), resident((D_MODEL, D_FF)), resident((D_FF, D_MODEL)),
        ],
        out_specs=pl.BlockSpec((TOKEN_TILE, D_MODEL), lambda i: (i, 0)),
        out_shape=jax.ShapeDtypeStruct(x2.shape, F32),
        compiler_params=_params("parallel"),
        name="ffn_swiglu",
    )(x2, g, wg_bf, wu_bf, wd_bf)


def _split_bf16(x):
    hi = x.astype(BF16)
    return hi, (x - hi.astype(F32)).astype(BF16)


def _route_normalise(h, g_ref, xn_ref):
    xn = _rmsnorm_f32(h, g_ref[...])
    xn_ref[...] = _pack_bf16_pair(xn)
    return _split_bf16(xn)


def _route_select(x_hi, x_lo, wrt_ref, before_ref, count_ref,
                  sel_e_ref, sel_r_ref, sel_g_ref, cend_ref):
    w_hi, w_lo = _split_bf16(wrt_ref[...])
    nt = lambda a, b: lax.dot_general(a, b, (((1,), (1,)), ((), ())), preferred_element_type=F32)
    both = nt(jnp.concatenate([w_hi, w_lo], axis=0), x_hi)
    logits = both[:N_EXPERTS] + (both[N_EXPERTS:] + nt(w_hi, x_lo))
    sub = lax.broadcasted_iota(jnp.int32, logits.shape, 0).astype(F32)
    none = float(N_EXPERTS)
    m1 = jnp.max(logits, axis=0, keepdims=True)
    i1 = jnp.min(jnp.where(logits == m1, sub, none), axis=0, keepdims=True)
    rest = jnp.where(sub == i1, -jnp.inf, logits)
    m2 = jnp.max(rest, axis=0, keepdims=True)
    i2 = jnp.min(jnp.where(rest == m2, sub, none), axis=0, keepdims=True)
    e2 = jnp.exp(m2 - m1)
    denom = 1.0 + e2

    chosen = jnp.logical_or(sub == i1, sub == i2)
    prefix = jnp.dot(jnp.where(chosen, 1.0, 0.0).astype(BF16), before_ref[...],
                     preferred_element_type=F32)
    count = count_ref[...]
    rank = prefix + count[:, :1]
    r1 = jnp.sum(jnp.where(sub == i1, rank, 0.0), axis=0, keepdims=True)
    r2 = jnp.sum(jnp.where(sub == i2, rank, 0.0), axis=0, keepdims=True)
    sel_e_ref[0:1, :] = i1.astype(I32)
    sel_e_ref[1:2, :] = i2.astype(I32)
    sel_r_ref[0:1, :] = r1.astype(I32)
    sel_r_ref[1:2, :] = r2.astype(I32)
    sel_g_ref[0:1, :] = 1.0 / denom
    sel_g_ref[1:2, :] = e2 / denom
    count = count + jnp.sum(jnp.where(chosen, 1.0, 0.0), axis=1, keepdims=True)
    count_ref[...] = count
    cend_ref[0] = count


def _conv_route_kernel(x_ref, g_ref, win_ref, wc_ref, wout_ref, g2_ref, wrt_ref, before_ref,
                       o_ref, xn_ref, sel_e_ref, sel_r_ref, sel_g_ref, cend_ref,
                       carry_ref, count_ref):
    s = pl.program_id(1)
    ts = CONV_SEQ_TILE

    @pl.when(s == 0)
    def _():
        carry_ref[...] = jnp.zeros_like(carry_ref)

    @pl.when(jnp.logical_and(pl.program_id(0) == 0, s == 0))
    def _():
        count_ref[...] = jnp.zeros_like(count_ref)

    x = x_ref[0]
    xn = _rmsnorm_f32(x, g_ref[...]).astype(BF16)
    proj = jnp.dot(xn, win_ref[...], preferred_element_type=F32)
    gate_b = proj[:, :D_MODEL]
    gate_c = proj[:, D_MODEL:2 * D_MODEL]
    hid = proj[:, 2 * D_MODEL:]
    u = gate_c * hid

    row = lax.broadcasted_iota(jnp.int32, (ts, D_MODEL), 0)
    prev1 = carry_ref[7:8, :]
    prev2 = carry_ref[6:7, :]
    u1 = jnp.where(row == 0, prev1, pltpu.roll(u, 1, axis=0))
    u2 = jnp.where(row == 0, prev2, jnp.where(row == 1, prev1, pltpu.roll(u, 2, axis=0)))
    carry_ref[...] = u[ts - 8:, :]

    wc = wc_ref[...]
    y = wc[2:3, :] * u + wc[1:2, :] * u1 + wc[0:1, :] * u2
    y = (gate_b * y).astype(BF16)
    h = x + jnp.dot(y, wout_ref[...], preferred_element_type=F32)
    o_ref[0] = h
    x_hi, x_lo = _route_normalise(h, g2_ref, xn_ref)
    _route_select(x_hi, x_lo, wrt_ref, before_ref, count_ref,
                  sel_e_ref, sel_r_ref, sel_g_ref, cend_ref)


def _conv_route(x, part, g, win_bf, wc, wout_bf, g2, wrt):
    b, seq, _ = x.shape
    ts = CONV_SEQ_TILE
    w = ROUTE_WINDOW
    assert ts == w and b % MOE_PARTS == 0
    bp = b // MOE_PARTS
    b0 = part * bp
    st = seq // ts
    steps = bp * st
    tp = bp * seq
    flat = lambda bi, s: bi * st + s
    resident = lambda shape: pl.BlockSpec(shape, lambda bi, s: (0, 0), pipeline_mode=pl.Buffered(1))
    before = (jnp.arange(w)[:, None] < jnp.arange(w)[None, :]).astype(BF16)

    per_token = lambda dtype: jax.ShapeDtypeStruct((TOP_K, tp), dtype)
    return pl.pallas_call(
        _conv_route_kernel,
        grid=(bp, st),
        in_specs=[
            pl.BlockSpec((1, ts, D_MODEL), lambda bi, s: (b0 + bi, s, 0)),
            pl.BlockSpec((1, D_MODEL), lambda bi, s: (0, 0)),
            resident((D_MODEL, 3 * D_MODEL)),
            pl.BlockSpec((CONV_WIDTH, D_MODEL), lambda bi, s: (0, 0)),
            resident((D_MODEL, D_MODEL)),
            pl.BlockSpec((1, D_MODEL), lambda bi, s: (0, 0)),
            pl.BlockSpec((N_EXPERTS, D_MODEL), lambda bi, s: (0, 0)),
            resident((w, w)),
        ],
        out_specs=[
            pl.BlockSpec((1, ts, D_MODEL), lambda bi, s: (bi, s, 0)),
            pl.BlockSpec((w, PACKED), lambda bi, s: (flat(bi, s), 0)),
            pl.BlockSpec((TOP_K, w), lambda bi, s: (0, flat(bi, s))),
            pl.BlockSpec((TOP_K, w), lambda bi, s: (0, flat(bi, s))),
            pl.BlockSpec((TOP_K, w), lambda bi, s: (0, flat(bi, s))),
            pl.BlockSpec((1, N_EXPERTS, LANES), lambda bi, s: (flat(bi, s), 0, 0)),
        ],
        out_shape=[jax.ShapeDtypeStruct((bp, seq, D_MODEL), F32),
                   jax.ShapeDtypeStruct((tp, PACKED), I32),
                   per_token(I32), per_token(I32), per_token(F32),
                   jax.ShapeDtypeStruct((steps, N_EXPERTS, LANES), F32)],
        scratch_shapes=[pltpu.VMEM((8, D_MODEL), F32), pltpu.VMEM((N_EXPERTS, LANES), F32)],
        compiler_params=_params("arbitrary", "arbitrary"),
        name="conv_route",
    )(x, g, win_bf, wc, wout_bf, g2, wrt, before)


def _routing_plan(cend, sel_e, sel_r, t):
    counts = cend[-1, :, 0].astype(I32)
    ntiles = (counts + EXPERT_TILE - 1) // EXPERT_TILE
    tile_end = jnp.cumsum(ntiles)
    tile_start = tile_end - ntiles
    off = tile_start * EXPERT_TILE

    nt_max = TOP_K * t // EXPERT_TILE + N_EXPERTS
    ti = jnp.arange(nt_max, dtype=I32)
    tile_e = jnp.minimum(jnp.sum(tile_end[None, :] <= ti[:, None], axis=1).astype(I32),
                         N_EXPERTS - 1)
    tile_rows = jnp.clip(counts[tile_e] - (ti - tile_start[tile_e]) * EXPERT_TILE, 0, EXPERT_TILE)
    tile_rows = jnp.where(ti < tile_end[-1], tile_rows, 0)

    first_row = jnp.zeros_like(sel_e)
    for e in range(N_EXPERTS):
        first_row = first_row + jnp.where(sel_e == e, off[e], 0)
    dest = first_row + sel_r
    return tile_e, tile_rows, dest


def _sc_mesh():
    return plsc.VectorSubcoreMesh(core_axis_name="c", subcore_axis_name="s",
                                  num_cores=SC_CORES, num_subcores=SC_SUBCORES)


def _dispatch_rows(xn, dest, n_rows):
    t, d = xn.shape
    per_worker = t // SC_WORKERS
    chunks = per_worker // SC_ROWS
    dest4 = dest.reshape(TOP_K, SC_WORKERS, chunks, SC_ROWS)

    @functools.partial(
        pl.kernel, mesh=_sc_mesh(),
        out_type=jax.ShapeDtypeStruct((n_rows, d), xn.dtype),
        scratch_types=[pltpu.VMEM((TOP_K, chunks, SC_ROWS), I32),
                       pltpu.VMEM((SC_ROWS, d), xn.dtype)],
        name="moe_dispatch",
    )
    def run(x_hbm, dest_hbm, o_hbm, dest_v, rows_v):
        worker = lax.axis_index("s") * SC_CORES + lax.axis_index("c")
        for k in range(TOP_K):
            pltpu.sync_copy(dest_hbm.at[k, worker], dest_v.at[k])

        @pl.loop(0, chunks)
        def _(j):
            pltpu.sync_copy(x_hbm.at[pl.ds(worker * per_worker + j * SC_ROWS, SC_ROWS)], rows_v)
            for k in range(TOP_K):
                pltpu.sync_copy(rows_v, o_hbm.at[dest_v.at[k, j]])

    return run(xn, dest4)


def _collect_rows(y, dest):
    t = dest.shape[1]
    d = y.shape[1]
    per_worker = t // SC_WORKERS
    chunks = per_worker // SC_ROWS
    dest4 = dest.reshape(TOP_K, SC_WORKERS, chunks, SC_ROWS)

    @functools.partial(
        pl.kernel, mesh=_sc_mesh(),
        out_type=jax.ShapeDtypeStruct((TOP_K, t, d), y.dtype),
        scratch_types=[pltpu.VMEM((TOP_K, chunks, SC_ROWS), I32),
                       pltpu.VMEM((SC_ROWS, d), y.dtype)],
        name="moe_collect",
    )
    def run(y_hbm, dest_hbm, o_hbm, dest_v, rows_v):
        worker = lax.axis_index("s") * SC_CORES + lax.axis_index("c")
        for k in range(TOP_K):
            pltpu.sync_copy(dest_hbm.at[k, worker], dest_v.at[k])

        @pl.loop(0, chunks)
        def _(j):
            for k in range(TOP_K):
                pltpu.sync_copy(y_hbm.at[dest_v.at[k, j]], rows_v)
                pltpu.sync_copy(rows_v, o_hbm.at[k, pl.ds(worker * per_worker + j * SC_ROWS, SC_ROWS)])

    return run(y, dest4)


def _expert_kernel(tile_e_s, tile_rows_s, xs_ref, wg_ref, wu_ref, wd_ref, y_ref):
    n_rows = tile_rows_s[pl.program_id(0)]
    n_sub = EXPERT_TILE // ROW_SUB

    def run(sub_blocks):
        for r in range(sub_blocks):
            rows = pl.ds(r * ROW_SUB, ROW_SUB)
            real = lax.broadcasted_iota(jnp.int32, (ROW_SUB, PACKED), 0) < n_rows - r * ROW_SUB
            xs = _unpack_bf16_pair(jnp.where(real, xs_ref[rows, :], 0)).astype(BF16)
            y_ref[rows, :] = _pack_bf16_pair(_swiglu(xs, wg_ref.at[0], wu_ref.at[0], wd_ref.at[0]))
        if sub_blocks < n_sub:
            rest = pl.ds(sub_blocks * ROW_SUB, (n_sub - sub_blocks) * ROW_SUB)
            y_ref[rest, :] = jnp.zeros(((n_sub - sub_blocks) * ROW_SUB, PACKED), I32)

    for used in range(n_sub + 1):
        lo, hi = (used - 1) * ROW_SUB, used * ROW_SUB
        pl.when(jnp.logical_and(n_rows > lo, n_rows <= hi))(functools.partial(run, used))


def _experts(tile_e, tile_rows, xs, wg_bf, wu_bf, wd_bf):
    n_rows = xs.shape[0]
    weights = lambda shape, buffers: pl.BlockSpec(
        (1,) + shape, lambda i, te, tr: (te[i], 0, 0), pipeline_mode=pl.Buffered(buffers))
    grid_spec = pltpu.PrefetchScalarGridSpec(
        num_scalar_prefetch=2,
        grid=(n_rows // EXPERT_TILE,),
        in_specs=[
            pl.BlockSpec((EXPERT_TILE, PACKED), lambda i, te, tr: (i, 0)),
            weights((D_MODEL, D_FF), 1), weights((D_MODEL, D_FF), 1), weights((D_FF, D_MODEL), 2),
        ],
        out_specs=pl.BlockSpec((EXPERT_TILE, PACKED), lambda i, te, tr: (i, 0)),
    )
    return pl.pallas_call(
        _expert_kernel,
        grid_spec=grid_spec,
        out_shape=jax.ShapeDtypeStruct((n_rows, PACKED), I32),
        compiler_params=_params("arbitrary"),
        name="moe_experts",
    )(tile_e, tile_rows, xs, wg_bf, wu_bf, wd_bf)


def _final_kernel(x_ref, y_ref, sel_g_ref, gf_ref, buf_ref, o_ref):
    del buf_ref
    w = ROUTE_WINDOW
    diag = (lax.broadcasted_iota(jnp.int32, (w, w), 0)
            == lax.broadcasted_iota(jnp.int32, (w, w), 1))
    acc = x_ref[...]
    for k in range(TOP_K):
        gate = jnp.sum(jnp.where(diag, sel_g_ref[k:k + 1, :], 0.0), axis=1, keepdims=True)
        acc = acc + gate * _unpack_bf16_pair(y_ref[k])
    o_ref[...] = _rmsnorm_f32(acc, gf_ref[...])


def _final(x_part, y_part, sel_g_part, g_final, buf, part):
    tp = x_part.shape[0]
    w = ROUTE_WINDOW
    first = part * (tp // w)
    return pl.pallas_call(
        _final_kernel,
        grid=(tp // w,),
        in_specs=[
            pl.BlockSpec((w, D_MODEL), lambda i: (i, 0)),
            pl.BlockSpec((TOP_K, w, PACKED), lambda i: (0, i, 0)),
            pl.BlockSpec((TOP_K, w), lambda i: (0, i)),
            pl.BlockSpec((1, D_MODEL), lambda i: (0, 0)),
            pl.BlockSpec(memory_space=pl.ANY),
        ],
        out_specs=pl.BlockSpec((w, D_MODEL), lambda i: (first + i, 0)),
        out_shape=jax.ShapeDtypeStruct(buf.shape, F32),
        input_output_aliases={4: 0},
        compiler_params=_params("parallel"),
        name="moe_final",
    )(x_part, y_part, sel_g_part, g_final, buf)


def _rotary_tables(seq):
    inv_freq = ROPE_BASE ** (-jnp.arange(0, RET_QK_DIM, 2, dtype=F32) / RET_QK_DIM)
    ang = jnp.arange(seq, dtype=F32)[:, None] * inv_freq[None, :]
    return jnp.cos(ang), jnp.sin(ang)


def _decay_tables():
    c = RET_CHUNK
    log_gamma = jnp.log(1.0 - 2.0 ** (-5.0 - jnp.arange(RET_HEADS, dtype=F32)))
    idx = jnp.arange(c, dtype=F32)
    diff = idx[:, None] - idx[None, :]
    causal = diff >= 0
    intra = jnp.where(causal[None],
                      jnp.exp(jnp.where(causal, diff, 0.0)[None] * log_gamma[:, None, None]), 0.0)
    qdec = jnp.exp((idx[None, :] + 1.0) * log_gamma[:, None])[..., None]
    kdec = jnp.exp((c - 1.0 - idx[None, :]) * log_gamma[:, None])[..., None]
    cdec = jnp.exp(c * log_gamma)[:, None, None]
    return intra, qdec, kdec, cdec


def kernel(x, norm_mix0, ret_w_in, ret_w_out, norm_ffn0, ffn_w_gate, ffn_w_up, ffn_w_down,
           norm_mix1, conv_w_in, conv_w, conv_w_out, norm_ffn1, moe_router, moe_w_gate,
           moe_w_up, moe_w_down, norm_final):
    b, seq, d = x.shape
    t = b * seq
    row = lambda g: g.reshape(1, d).astype(F32)
    bf = lambda w: w.astype(BF16)

    cos, sin = _rotary_tables(seq)
    intra, qdec, kdec, cdec = _decay_tables()

    proj = _ret_in(x.reshape(t, d), row(norm_mix0), bf(ret_w_in), cos, sin, seq)
    h, expert_w = _ret_core(proj, x, bf(ret_w_out), intra, qdec, kdec, cdec,
                            (moe_w_gate, moe_w_up, moe_w_down))
    h_ffn = _ffn(h.reshape(t, d), row(norm_ffn0), bf(ffn_w_gate), bf(ffn_w_up), bf(ffn_w_down))
    h = h_ffn.reshape(b, seq, d)

    conv_args = (row(norm_mix1), bf(conv_w_in), conv_w.astype(F32), bf(conv_w_out),
                 row(norm_ffn1), moe_router.astype(F32).T)
    routed = [_conv_route(h, part, *conv_args) for part in range(MOE_PARTS)]

    tp = t // MOE_PARTS
    n_rows = (TOP_K * tp // EXPERT_TILE + N_EXPERTS) * EXPERT_TILE
    collected = []
    for h_p, xn, sel_e, sel_r, sel_g, cend in routed:
        tile_e, tile_rows, dest = _routing_plan(cend, sel_e, sel_r, tp)
        xs = _dispatch_rows(xn, dest, n_rows)
        y = _experts(tile_e, tile_rows, xs, *expert_w)
        collected.append((h_p.reshape(tp, d), _collect_rows(y, dest), sel_g))
    out = h_ffn
    for part, (h_p, y_tok, sel_g) in enumerate(collected):
        out = _final(h_p, y_tok, sel_g, row(norm_final), out, part)
    return out.reshape(b, seq, d)
```

```python
import functools

import jax
import jax.numpy as jnp
from jax import lax
from jax.experimental import pallas as pl
from jax.experimental.pallas import tpu as pltpu
from jax.experimental.pallas import tpu_sc as plsc

D_MODEL = 1024
RET_HEADS = 4
RET_QK_DIM = D_MODEL // RET_HEADS
RET_V_DIM = 2 * RET_QK_DIM
RET_QK_TOTAL = RET_HEADS * RET_QK_DIM
RET_V_TOTAL = RET_HEADS * RET_V_DIM
RET_IN_COLS = 2 * RET_QK_TOTAL + 2 * RET_V_TOTAL
ROPE_BASE = 10000.0
CONV_WIDTH = 3
D_FF = 7 * D_MODEL // 2
N_EXPERTS = 8
TOP_K = 2
NORM_EPS = 1e-6

RET_CHUNK = 256
HALF = RET_QK_DIM // 2

VMEM_LIMIT_BYTES = 56 * 1024 * 1024
SC_CORES = 2
SC_SUBCORES = 16
SC_WORKERS = SC_CORES * SC_SUBCORES
SC_ROWS = 64

TOKEN_TILE = 1024
RET_SEQ_TILE = 512
CONV_SEQ_TILE = 512
FF_TILE = 1792
ROW_SUB = 512
IN_TOKEN_TILE = 512
IN_COL_TILE = 1024
ROUTE_WINDOW = 512
EXPERT_TILE = 1024
MOE_PARTS = 2
LANES = 128
BF16_SUBLANES = 16
SIDE_SLAB_BYTES = 2 * 1024 * 1024
PACKED = D_MODEL // 2

BF16 = jnp.bfloat16
F32 = jnp.float32
I32 = jnp.int32


def _params(*semantics):
    return pltpu.CompilerParams(dimension_semantics=semantics,
                                vmem_limit_bytes=VMEM_LIMIT_BYTES)


def _rmsnorm_f32(x, g):
    return x * lax.rsqrt(jnp.mean(x * x, axis=-1, keepdims=True) + NORM_EPS) * g


def _silu(x):
    return x * (1.0 / (1.0 + jnp.exp(-x)))


def _pack_bf16_pair(x):
    half = x.shape[1] // 2
    hi = lax.bitcast_convert_type(x[:, :half].astype(BF16).astype(F32), I32)
    lo = lax.bitcast_convert_type(x[:, half:].astype(BF16).astype(F32), I32)
    return hi | lax.shift_right_logical(lo, 16)


def _unpack_bf16_pair(w):
    hi = lax.bitcast_convert_type(w & jnp.int32(-65536), F32)
    lo = lax.bitcast_convert_type(lax.shift_left(w, 16), F32)
    return jnp.concatenate([hi, lo], axis=1)


def _ret_in_kernel(x_ref, g_ref, w_ref, cos_ref, sin_ref, o_ref):
    xn = _rmsnorm_f32(x_ref[...], g_ref[...]).astype(BF16)
    cos = cos_ref[...]
    sin = sin_ref[...]
    for n in range(RET_IN_COLS // IN_COL_TILE):
        cols = slice(n * IN_COL_TILE, (n + 1) * IN_COL_TILE)
        acc = jnp.dot(xn, w_ref[:, cols], preferred_element_type=F32)
        if n * IN_COL_TILE >= 2 * RET_QK_TOTAL:
            o_ref[:, cols] = acc.astype(BF16)
            continue
        scale = 1.0 if n * IN_COL_TILE < RET_QK_TOTAL else RET_QK_DIM ** -0.5
        for h in range(IN_COL_TILE // RET_QK_DIM):
            lo = h * RET_QK_DIM
            t1 = acc[:, lo:lo + HALF]
            t2 = acc[:, lo + HALF:lo + RET_QK_DIM]
            base = n * IN_COL_TILE + lo
            o_ref[:, base:base + HALF] = ((t1 * cos - t2 * sin) * scale).astype(BF16)
            o_ref[:, base + HALF:base + RET_QK_DIM] = ((t1 * sin + t2 * cos) * scale).astype(BF16)


def _ret_in(x2, g, w_bf, cos, sin, seq):
    t = x2.shape[0]
    tm = IN_TOKEN_TILE
    seq_tiles = seq // tm
    return pl.pallas_call(
        _ret_in_kernel,
        grid=(t // tm,),
        in_specs=[
            pl.BlockSpec((tm, D_MODEL), lambda i: (i, 0)),
            pl.BlockSpec((1, D_MODEL), lambda i: (0, 0)),
            pl.BlockSpec((D_MODEL, RET_IN_COLS), lambda i: (0, 0), pipeline_mode=pl.Buffered(1)),
            pl.BlockSpec((tm, HALF), lambda i: (i % seq_tiles, 0)),
            pl.BlockSpec((tm, HALF), lambda i: (i % seq_tiles, 0)),
        ],
        out_specs=pl.BlockSpec((tm, RET_IN_COLS), lambda i: (i, 0)),
        out_shape=jax.ShapeDtypeStruct((t, RET_IN_COLS), BF16),
        compiler_params=_params("parallel"),
        name="ret_in_proj",
    )(x2, g, w_bf, cos, sin)


def _ret_core_kernel(q_ref, k_ref, v_ref, gate_ref, x_ref, wout_ref,
                     intra_ref, qdec_ref, kdec_ref, cdec_ref, *rest):
    n_side = (len(rest) - 2) // 2
    side_in, o_ref, side_out, state_ref = (rest[:n_side], rest[n_side],
                                           rest[n_side + 1:2 * n_side + 1], rest[-1])

    for src, dst in zip(side_in, side_out):
        dst[...] = src[...].astype(BF16)

    @pl.when(pl.program_id(1) == 0)
    def _():
        state_ref[...] = jnp.zeros_like(state_ref)

    gated = []
    for c in range(RET_SEQ_TILE // RET_CHUNK):
        rows = pl.ds(c * RET_CHUNK, RET_CHUNK)
        heads = []
        for h in range(RET_HEADS):
            qk_cols = slice(h * RET_QK_DIM, (h + 1) * RET_QK_DIM)
            v_cols = slice(h * RET_V_DIM, (h + 1) * RET_V_DIM)
            q = q_ref[rows, qk_cols]
            k = k_ref[rows, qk_cols]
            v = v_ref[rows, v_cols]
            state = state_ref[h]
            scores = lax.dot_general(q, k, (((1,), (1,)), ((), ())),
                                     preferred_element_type=F32) * intra_ref[h]
            inner = jnp.dot(scores.astype(BF16), v, preferred_element_type=F32)
            qd = (q.astype(F32) * qdec_ref[h]).astype(BF16)
            cross = jnp.dot(qd, state.astype(BF16), preferred_element_type=F32)
            kd = (k.astype(F32) * kdec_ref[h]).astype(BF16)
            state_ref[h] = state * cdec_ref[h] + lax.dot_general(
                kd, v, (((0,), (0,)), ((), ())), preferred_element_type=F32)
            o = inner + cross
            o = o * lax.rsqrt(jnp.mean(o * o, axis=-1, keepdims=True) + NORM_EPS)
            gate = gate_ref[rows, v_cols].astype(F32)
            heads.append((gate * (0.5 * jnp.tanh(0.5 * gate) + 0.5) * o).astype(BF16))
        gated.append(jnp.concatenate(heads, axis=1))
    og = jnp.concatenate(gated, axis=0)
    o_ref[0] = x_ref[0] + jnp.dot(og, wout_ref[...], preferred_element_type=F32)


def _ret_core(proj, x, wout_bf, intra, qdec, kdec, cdec, side_weights):
    b, seq, _ = x.shape
    ts = RET_SEQ_TILE
    st = seq // ts
    steps = b * st
    whole = lambda a: pl.BlockSpec(a.shape, lambda bi, s: (0,) * a.ndim)

    def slab_rows(wt):
        rows = wt.shape[0] * wt.shape[1]
        per_step = rows // steps
        ok = (rows % steps == 0 and per_step % BF16_SUBLANES == 0 and wt.shape[1] % per_step == 0
              and per_step * wt.shape[2] * 4 <= SIDE_SLAB_BYTES)
        return per_step if ok else None

    hosted = all(slab_rows(wt) is not None for wt in side_weights)
    side = tuple(side_weights) if hosted else ()
    side_specs = []
    for wt in side:
        per_step = slab_rows(wt)
        per_mat = wt.shape[1] // per_step
        side_specs.append(pl.BlockSpec(
            (1, per_step, wt.shape[2]),
            lambda bi, s, per_mat=per_mat: ((bi * st + s) // per_mat, (bi * st + s) % per_mat, 0)))

    outs = pl.pallas_call(
        _ret_core_kernel,
        grid=(b, st),
        in_specs=[
            pl.BlockSpec((ts, RET_QK_TOTAL), lambda bi, s: (bi * st + s, 0)),
            pl.BlockSpec((ts, RET_QK_TOTAL), lambda bi, s: (bi * st + s, 1)),
            pl.BlockSpec((ts, RET_V_TOTAL), lambda bi, s: (bi * st + s, 1)),
            pl.BlockSpec((ts, RET_V_TOTAL), lambda bi, s: (bi * st + s, 2)),
            pl.BlockSpec((1, ts, D_MODEL), lambda bi, s: (bi, s, 0)),
            pl.BlockSpec((RET_V_TOTAL, D_MODEL), lambda bi, s: (0, 0), pipeline_mode=pl.Buffered(1)),
            whole(intra), whole(qdec), whole(kdec), whole(cdec),
        ] + side_specs,
        out_specs=[pl.BlockSpec((1, ts, D_MODEL), lambda bi, s: (bi, s, 0))] + side_specs,
        out_shape=[jax.ShapeDtypeStruct(x.shape, F32)]
                  + [jax.ShapeDtypeStruct(wt.shape, BF16) for wt in side],
        scratch_shapes=[pltpu.VMEM((RET_HEADS, RET_QK_DIM, RET_V_DIM), F32)],
        compiler_params=_params("arbitrary", "arbitrary"),
        name="ret_core",
    )(proj, proj, proj, proj, x, wout_bf, intra, qdec, kdec, cdec, *side)
    if hosted:
        return outs[0], tuple(outs[1:])
    return outs[0], tuple(wt.astype(BF16) for wt in side_weights)


def _swiglu(x_bf, wg_ref, wu_ref, wd_ref):
    out = None
    for c in range(D_FF // FF_TILE):
        cols = slice(c * FF_TILE, (c + 1) * FF_TILE)
        hg = jnp.dot(x_bf, wg_ref[:, cols], preferred_element_type=F32)
        hu = jnp.dot(x_bf, wu_ref[:, cols], preferred_element_type=F32)
        hid = (_silu(hg) * hu).astype(BF16)
        part = jnp.dot(hid, wd_ref[cols, :], preferred_element_type=F32)
        out = part if out is None else out + part
    return out


def _ffn_kernel(x_ref, g_ref, wg_ref, wu_ref, wd_ref, o_ref):
    for r in range(TOKEN_TILE // ROW_SUB):
        rows = pl.ds(r * ROW_SUB, ROW_SUB)
        x = x_ref[rows, :]
        xn = _rmsnorm_f32(x, g_ref[...]).astype(BF16)
        o_ref[rows, :] = x + _swiglu(xn, wg_ref, wu_ref, wd_ref)


def _ffn(x2, g, wg_bf, wu_bf, wd_bf):
    t = x2.shape[0]
    resident = lambda shape: pl.BlockSpec(shape, lambda i: (0, 0), pipeline_mode=pl.Buffered(1))
    return pl.pallas_call(
        _ffn_kernel,
        grid=(t // TOKEN_TILE,),
        in_specs=[
            pl.BlockSpec((TOKEN_TILE, D_MODEL), lambda i: (i, 0)),
            pl.BlockSpec((1, D_MODEL), lambda i: (0, 0)),
            resident((D_MODEL, D_FF)), resident((D_MODEL, D_FF)), resident((D_FF, D_MODEL)),
        ],
        out_specs=pl.BlockSpec((TOKEN_TILE, D_MODEL), lambda i: (i, 0)),
        out_shape=jax.ShapeDtypeStruct(x2.shape, F32),
        compiler_params=_params("parallel"),
        name="ffn_swiglu",
    )(x2, g, wg_bf, wu_bf, wd_bf)


def _split_bf16(x):
    hi = x.astype(BF16)
    return hi, (x - hi.astype(F32)).astype(BF16)


def _route_normalise(h, g_ref, xn_ref):
    xn = _rmsnorm_f32(h, g_ref[...])
    xn_ref[...] = _pack_bf16_pair(xn)
    return _split_bf16(xn)


def _route_select(x_hi, x_lo, wrt_ref, before_ref, count_ref,
                  sel_e_ref, sel_r_ref, sel_g_ref, cend_ref):
    w_hi, w_lo = _split_bf16(wrt_ref[...])
    nt = lambda a, b: lax.dot_general(a, b, (((1,), (1,)), ((), ())), preferred_element_type=F32)
    both = nt(jnp.concatenate([w_hi, w_lo], axis=0), x_hi)
    logits = both[:N_EXPERTS] + (both[N_EXPERTS:] + nt(w_hi, x_lo))
    sub = lax.broadcasted_iota(jnp.int32, logits.shape, 0).astype(F32)
    none = float(N_EXPERTS)
    m1 = jnp.max(logits, axis=0, keepdims=True)
    i1 = jnp.min(jnp.where(logits == m1, sub, none), axis=0, keepdims=True)
    rest = jnp.where(sub == i1, -jnp.inf, logits)
    m2 = jnp.max(rest, axis=0, keepdims=True)
    i2 = jnp.min(jnp.where(rest == m2, sub, none), axis=0, keepdims=True)
    e2 = jnp.exp(m2 - m1)
    denom = 1.0 + e2

    chosen = jnp.logical_or(sub == i1, sub == i2)
    prefix = jnp.dot(jnp.where(chosen, 1.0, 0.0).astype(BF16), before_ref[...],
                     preferred_element_type=F32)
    count = count_ref[...]
    rank = prefix + count[:, :1]
    r1 = jnp.sum(jnp.where(sub == i1, rank, 0.0), axis=0, keepdims=True)
    r2 = jnp.sum(jnp.where(sub == i2, rank, 0.0), axis=0, keepdims=True)
    sel_e_ref[0:1, :] = i1.astype(I32)
    sel_e_ref[1:2, :] = i2.astype(I32)
    sel_r_ref[0:1, :] = r1.astype(I32)
    sel_r_ref[1:2, :] = r2.astype(I32)
    sel_g_ref[0:1, :] = 1.0 / denom
    sel_g_ref[1:2, :] = e2 / denom
    count = count + jnp.sum(jnp.where(chosen, 1.0, 0.0), axis=1, keepdims=True)
    count_ref[...] = count
    cend_ref[0] = count


def _conv_route_kernel(x_ref, g_ref, win_ref, wc_ref, wout_ref, g2_ref, wrt_ref, before_ref,
                       o_ref, xn_ref, sel_e_ref, sel_r_ref, sel_g_ref, cend_ref,
                       carry_ref, count_ref):
    s = pl.program_id(1)
    ts = CONV_SEQ_TILE

    @pl.when(s == 0)
    def _():
        carry_ref[...] = jnp.zeros_like(carry_ref)

    @pl.when(jnp.logical_and(pl.program_id(0) == 0, s == 0))
    def _():
        count_ref[...] = jnp.zeros_like(count_ref)

    x = x_ref[0]
    xn = _rmsnorm_f32(x, g_ref[...]).astype(BF16)
    proj = jnp.dot(xn, win_ref[...], preferred_element_type=F32)
    gate_b = proj[:, :D_MODEL]
    gate_c = proj[:, D_MODEL:2 * D_MODEL]
    hid = proj[:, 2 * D_MODEL:]
    u = gate_c * hid

    row = lax.broadcasted_iota(jnp.int32, (ts, D_MODEL), 0)
    prev1 = carry_ref[7:8, :]
    prev2 = carry_ref[6:7, :]
    u1 = jnp.where(row == 0, prev1, pltpu.roll(u, 1, axis=0))
    u2 = jnp.where(row == 0, prev2, jnp.where(row == 1, prev1, pltpu.roll(u, 2, axis=0)))
    carry_ref[...] = u[ts - 8:, :]

    wc = wc_ref[...]
    y = wc[2:3, :] * u + wc[1:2, :] * u1 + wc[0:1, :] * u2
    y = (gate_b * y).astype(BF16)
    h = x + jnp.dot(y, wout_ref[...], preferred_element_type=F32)
    o_ref[0] = h
    x_hi, x_lo = _route_normalise(h, g2_ref, xn_ref)
    _route_select(x_hi, x_lo, wrt_ref, before_ref, count_ref,
                  sel_e_ref, sel_r_ref, sel_g_ref, cend_ref)


def _conv_route(x, part, g, win_bf, wc, wout_bf, g2, wrt):
    b, seq, _ = x.shape
    ts = CONV_SEQ_TILE
    w = ROUTE_WINDOW
    assert ts == w and b % MOE_PARTS == 0
    bp = b // MOE_PARTS
    b0 = part * bp
    st = seq // ts
    steps = bp * st
    tp = bp * seq
    flat = lambda bi, s: bi * st + s
    resident = lambda shape: pl.BlockSpec(shape, lambda bi, s: (0, 0), pipeline_mode=pl.Buffered(1))
    before = (jnp.arange(w)[:, None] < jnp.arange(w)[None, :]).astype(BF16)

    per_token = lambda dtype: jax.ShapeDtypeStruct((TOP_K, tp), dtype)
    return pl.pallas_call(
        _conv_route_kernel,
        grid=(bp, st),
        in_specs=[
            pl.BlockSpec((1, ts, D_MODEL), lambda bi, s: (b0 + bi, s, 0)),
            pl.BlockSpec((1, D_MODEL), lambda bi, s: (0, 0)),
            resident((D_MODEL, 3 * D_MODEL)),
            pl.BlockSpec((CONV_WIDTH, D_MODEL), lambda bi, s: (0, 0)),
            resident((D_MODEL, D_MODEL)),
            pl.BlockSpec((1, D_MODEL), lambda bi, s: (0, 0)),
            pl.BlockSpec((N_EXPERTS, D_MODEL), lambda bi, s: (0, 0)),
            resident((w, w)),
        ],
        out_specs=[
            pl.BlockSpec((1, ts, D_MODEL), lambda bi, s: (bi, s, 0)),
            pl.BlockSpec((w, PACKED), lambda bi, s: (flat(bi, s), 0)),
            pl.BlockSpec((TOP_K, w), lambda bi, s: (0, flat(bi, s))),
            pl.BlockSpec((TOP_K, w), lambda bi, s: (0, flat(bi, s))),
            pl.BlockSpec((TOP_K, w), lambda bi, s: (0, flat(bi, s))),
            pl.BlockSpec((1, N_EXPERTS, LANES), lambda bi, s: (flat(bi, s), 0, 0)),
        ],
        out_shape=[jax.ShapeDtypeStruct((bp, seq, D_MODEL), F32),
                   jax.ShapeDtypeStruct((tp, PACKED), I32),
                   per_token(I32), per_token(I32), per_token(F32),
                   jax.ShapeDtypeStruct((steps, N_EXPERTS, LANES), F32)],
        scratch_shapes=[pltpu.VMEM((8, D_MODEL), F32), pltpu.VMEM((N_EXPERTS, LANES), F32)],
        compiler_params=_params("arbitrary", "arbitrary"),
        name="conv_route",
    )(x, g, win_bf, wc, wout_bf, g2, wrt, before)


def _routing_plan(cend, sel_e, sel_r, t):
    counts = cend[-1, :, 0].astype(I32)
    ntiles = (counts + EXPERT_TILE - 1) // EXPERT_TILE
    tile_end = jnp.cumsum(ntiles)
    tile_start = tile_end - ntiles
    off = tile_start * EXPERT_TILE

    nt_max = TOP_K * t // EXPERT_TILE + N_EXPERTS
    ti = jnp.arange(nt_max, dtype=I32)
    tile_e = jnp.minimum(jnp.sum(tile_end[None, :] <= ti[:, None], axis=1).astype(I32),
                         N_EXPERTS - 1)
    tile_rows = jnp.clip(counts[tile_e] - (ti - tile_start[tile_e]) * EXPERT_TILE, 0, EXPERT_TILE)
    tile_rows = jnp.where(ti < tile_end[-1], tile_rows, 0)

    first_row = jnp.zeros_like(sel_e)
    for e in range(N_EXPERTS):
        first_row = first_row + jnp.where(sel_e == e, off[e], 0)
    dest = first_row + sel_r
    return tile_e, tile_rows, dest


def _sc_mesh():
    return plsc.VectorSubcoreMesh(core_axis_name="c", subcore_axis_name="s",
                                  num_cores=SC_CORES, num_subcores=SC_SUBCORES)


def _dispatch_rows(xn, dest, n_rows):
    t, d = xn.shape
    per_worker = t // SC_WORKERS
    chunks = per_worker // SC_ROWS
    dest4 = dest.reshape(TOP_K, SC_WORKERS, chunks, SC_ROWS)

    @functools.partial(
        pl.kernel, mesh=_sc_mesh(),
        out_type=jax.ShapeDtypeStruct((n_rows, d), xn.dtype),
        scratch_types=[pltpu.VMEM((TOP_K, chunks, SC_ROWS), I32),
                       pltpu.VMEM((SC_ROWS, d), xn.dtype)],
        name="moe_dispatch",
    )
    def run(x_hbm, dest_hbm, o_hbm, dest_v, rows_v):
        worker = lax.axis_index("s") * SC_CORES + lax.axis_index("c")
        for k in range(TOP_K):
            pltpu.sync_copy(dest_hbm.at[k, worker], dest_v.at[k])

        @pl.loop(0, chunks)
        def _(j):
            pltpu.sync_copy(x_hbm.at[pl.ds(worker * per_worker + j * SC_ROWS, SC_ROWS)], rows_v)
            for k in range(TOP_K):
                pltpu.sync_copy(rows_v, o_hbm.at[dest_v.at[k, j]])

    return run(xn, dest4)


def _collect_rows(y, dest):
    t = dest.shape[1]
    d = y.shape[1]
    per_worker = t // SC_WORKERS
    chunks = per_worker // SC_ROWS
    dest4 = dest.reshape(TOP_K, SC_WORKERS, chunks, SC_ROWS)

    @functools.partial(
        pl.kernel, mesh=_sc_mesh(),
        out_type=jax.ShapeDtypeStruct((TOP_K, t, d), y.dtype),
        scratch_types=[pltpu.VMEM((TOP_K, chunks, SC_ROWS), I32),
                       pltpu.VMEM((SC_ROWS, d), y.dtype)],
        name="moe_collect",
    )
    def run(y_hbm, dest_hbm, o_hbm, dest_v, rows_v):
        worker = lax.axis_index("s") * SC_CORES + lax.axis_index("c")
        for k in range(TOP_K):
            pltpu.sync_copy(dest_hbm.at[k, worker], dest_v.at[k])

        @pl.loop(0, chunks)
        def _(j):
            for k in range(TOP_K):
                pltpu.sync_copy(y_hbm.at[dest_v.at[k, j]], rows_v)
                pltpu.sync_copy(rows_v, o_hbm.at[k, pl.ds(worker * per_worker + j * SC_ROWS, SC_ROWS)])

    return run(y, dest4)


def _expert_kernel(tile_e_s, tile_rows_s, xs_ref, wg_ref, wu_ref, wd_ref, y_ref):
    n_rows = tile_rows_s[pl.program_id(0)]
    n_sub = EXPERT_TILE // ROW_SUB

    def run(sub_blocks):
        for r in range(sub_blocks):
            rows = pl.ds(r * ROW_SUB, ROW_SUB)
            real = lax.broadcasted_iota(jnp.int32, (ROW_SUB, PACKED), 0) < n_rows - r * ROW_SUB
            xs = _unpack_bf16_pair(jnp.where(real, xs_ref[rows, :], 0)).astype(BF16)
            y_ref[rows, :] = _pack_bf16_pair(_swiglu(xs, wg_ref.at[0], wu_ref.at[0], wd_ref.at[0]))
        if sub_blocks < n_sub:
            rest = pl.ds(sub_blocks * ROW_SUB, (n_sub - sub_blocks) * ROW_SUB)
            y_ref[rest, :] = jnp.zeros(((n_sub - sub_blocks) * ROW_SUB, PACKED), I32)

    for used in range(n_sub + 1):
        lo, hi = (used - 1) * ROW_SUB, used * ROW_SUB
        pl.when(jnp.logical_and(n_rows > lo, n_rows <= hi))(functools.partial(run, used))


def _experts(tile_e, tile_rows, xs, wg_bf, wu_bf, wd_bf):
    n_rows = xs.shape[0]
    weights = lambda shape, buffers: pl.BlockSpec(
        (1,) + shape, lambda i, te, tr: (te[i], 0, 0), pipeline_mode=pl.Buffered(buffers))
    grid_spec = pltpu.PrefetchScalarGridSpec(
        num_scalar_prefetch=2,
        grid=(n_rows // EXPERT_TILE,),
        in_specs=[
            pl.BlockSpec((EXPERT_TILE, PACKED), lambda i, te, tr: (i, 0)),
            weights((D_MODEL, D_FF), 1), weights((D_MODEL, D_FF), 2), weights((D_FF, D_MODEL), 2),
        ],
        out_specs=pl.BlockSpec((EXPERT_TILE, PACKED), lambda i, te, tr: (i, 0)),
    )
    return pl.pallas_call(
        _expert_kernel,
        grid_spec=grid_spec,
        out_shape=jax.ShapeDtypeStruct((n_rows, PACKED), I32),
        compiler_params=_params("arbitrary"),
        name="moe_experts",
    )(tile_e, tile_rows, xs, wg_bf, wu_bf, wd_bf)


def _final_kernel(x_ref, y_ref, sel_g_ref, gf_ref, buf_ref, o_ref):
    del buf_ref
    w = ROUTE_WINDOW
    diag = (lax.broadcasted_iota(jnp.int32, (w, w), 0)
            == lax.broadcasted_iota(jnp.int32, (w, w), 1))
    acc = x_ref[...]
    for k in range(TOP_K):
        gate = jnp.sum(jnp.where(diag, sel_g_ref[k:k + 1, :], 0.0), axis=1, keepdims=True)
        acc = acc + gate * _unpack_bf16_pair(y_ref[k])
    o_ref[...] = _rmsnorm_f32(acc, gf_ref[...])


def _final(x_part, y_part, sel_g_part, g_final, buf, part):
    tp = x_part.shape[0]
    w = ROUTE_WINDOW
    first = part * (tp // w)
    return pl.pallas_call(
        _final_kernel,
        grid=(tp // w,),
        in_specs=[
            pl.BlockSpec((w, D_MODEL), lambda i: (i, 0)),
            pl.BlockSpec((TOP_K, w, PACKED), lambda i: (0, i, 0)),
            pl.BlockSpec((TOP_K, w), lambda i: (0, i)),
            pl.BlockSpec((1, D_MODEL), lambda i: (0, 0)),
            pl.BlockSpec(memory_space=pl.ANY),
        ],
        out_specs=pl.BlockSpec((w, D_MODEL), lambda i: (first + i, 0)),
        out_shape=jax.ShapeDtypeStruct(buf.shape, F32),
        input_output_aliases={4: 0},
        compiler_params=_params("parallel"),
        name="moe_final",
    )(x_part, y_part, sel_g_part, g_final, buf)


def _rotary_tables(seq):
    inv_freq = ROPE_BASE ** (-jnp.arange(0, RET_QK_DIM, 2, dtype=F32) / RET_QK_DIM)
    ang = jnp.arange(seq, dtype=F32)[:, None] * inv_freq[None, :]
    return jnp.cos(ang), jnp.sin(ang)


def _decay_tables():
    c = RET_CHUNK
    log_gamma = jnp.log(1.0 - 2.0 ** (-5.0 - jnp.arange(RET_HEADS, dtype=F32)))
    idx = jnp.arange(c, dtype=F32)
    diff = idx[:, None] - idx[None, :]
    causal = diff >= 0
    intra = jnp.where(causal[None],
                      jnp.exp(jnp.where(causal, diff, 0.0)[None] * log_gamma[:, None, None]), 0.0)
    qdec = jnp.exp((idx[None, :] + 1.0) * log_gamma[:, None])[..., None]
    kdec = jnp.exp((c - 1.0 - idx[None, :]) * log_gamma[:, None])[..., None]
    cdec = jnp.exp(c * log_gamma)[:, None, None]
    return intra, qdec, kdec, cdec


def kernel(x, norm_mix0, ret_w_in, ret_w_out, norm_ffn0, ffn_w_gate, ffn_w_up, ffn_w_down,
           norm_mix1, conv_w_in, conv_w, conv_w_out, norm_ffn1, moe_router, moe_w_gate,
           moe_w_up, moe_w_down, norm_final):
    b, seq, d = x.shape
    t = b * seq
    row = lambda g: g.reshape(1, d).astype(F32)
    bf = lambda w: w.astype(BF16)

    cos, sin = _rotary_tables(seq)
    intra, qdec, kdec, cdec = _decay_tables()

    proj = _ret_in(x.reshape(t, d), row(norm_mix0), bf(ret_w_in), cos, sin, seq)
    h, expert_w = _ret_core(proj, x, bf(ret_w_out), intra, qdec, kdec, cdec,
                            (moe_w_gate, moe_w_up, moe_w_down))
    h_ffn = _ffn(h.reshape(t, d), row(norm_ffn0), bf(ffn_w_gate), bf(ffn_w_up), bf(ffn_w_down))
    h = h_ffn.reshape(b, seq, d)

    conv_args = (row(norm_mix1), bf(conv_w_in), conv_w.astype(F32), bf(conv_w_out),
                 row(norm_ffn1), moe_router.astype(F32).T)
    routed = [_conv_route(h, part, *conv_args) for part in range(MOE_PARTS)]

    tp = t // MOE_PARTS
    n_rows = (TOP_K * tp // EXPERT_TILE + N_EXPERTS) * EXPERT_TILE
    collected = []
    for h_p, xn, sel_e, sel_r, sel_g, cend in routed:
        tile_e, tile_rows, dest = _routing_plan(cend, sel_e, sel_r, tp)
        xs = _dispatch_rows(xn, dest, n_rows)
        y = _experts(tile_e, tile_rows, xs, *expert_w)
        collected.append((h_p.reshape(tp, d), _collect_rows(y, dest), sel_g))
    out = h_ffn
    for part, (h_p, y_tok, sel_g) in enumerate(collected):
        out = _final(h_p, y_tok, sel_g, row(norm_final), out, part)
    return out.reshape(b, seq, d)
```

```python
import functools

import jax
import jax.numpy as jnp
from jax import lax
from jax.experimental import pallas as pl
from jax.experimental.pallas import tpu as pltpu
from jax.experimental.pallas import tpu_sc as plsc

D_MODEL = 1024
RET_HEADS = 4
RET_QK_DIM = D_MODEL // RET_HEADS
RET_V_DIM = 2 * RET_QK_DIM
RET_QK_TOTAL = RET_HEADS * RET_QK_DIM
RET_V_TOTAL = RET_HEADS * RET_V_DIM
RET_IN_COLS = 2 * RET_QK_TOTAL + 2 * RET_V_TOTAL
ROPE_BASE = 10000.0
CONV_WIDTH = 3
D_FF = 7 * D_MODEL // 2
N_EXPERTS = 8
TOP_K = 2
NORM_EPS = 1e-6

RET_CHUNK = 256
HALF = RET_QK_DIM // 2

VMEM_LIMIT_BYTES = 56 * 1024 * 1024
SC_CORES = 2
SC_SUBCORES = 16
SC_WORKERS = SC_CORES * SC_SUBCORES
SC_ROWS = 64

TOKEN_TILE = 1024
RET_SEQ_TILE = 512
CONV_SEQ_TILE = 512
FF_TILE = 1792
ROW_SUB = 512
IN_TOKEN_TILE = 1024
IN_COL_TILE = 1024
ROUTE_WINDOW = 512
EXPERT_TILE = 1024
MOE_PARTS = 2
LANES = 128
BF16_SUBLANES = 16
SIDE_SLAB_BYTES = 2 * 1024 * 1024
PACKED = D_MODEL // 2

BF16 = jnp.bfloat16
F32 = jnp.float32
I32 = jnp.int32


def _params(*semantics):
    return pltpu.CompilerParams(dimension_semantics=semantics,
                                vmem_limit_bytes=VMEM_LIMIT_BYTES)


def _rmsnorm_f32(x, g):
    return x * lax.rsqrt(jnp.mean(x * x, axis=-1, keepdims=True) + NORM_EPS) * g


def _silu(x):
    return x * (1.0 / (1.0 + jnp.exp(-x)))


def _pack_bf16_pair(x):
    half = x.shape[1] // 2
    hi = lax.bitcast_convert_type(x[:, :half].astype(BF16).astype(F32), I32)
    lo = lax.bitcast_convert_type(x[:, half:].astype(BF16).astype(F32), I32)
    return hi | lax.shift_right_logical(lo, 16)


def _unpack_bf16_pair(w):
    hi = lax.bitcast_convert_type(w & jnp.int32(-65536), F32)
    lo = lax.bitcast_convert_type(lax.shift_left(w, 16), F32)
    return jnp.concatenate([hi, lo], axis=1)


def _ret_in_kernel(x_ref, g_ref, w_ref, cos_ref, sin_ref, o_ref):
    for r in range(IN_TOKEN_TILE // ROW_SUB):
        rows = pl.ds(r * ROW_SUB, ROW_SUB)
        xn = _rmsnorm_f32(x_ref[rows, :], g_ref[...]).astype(BF16)
        cos = cos_ref[rows, :]
        sin = sin_ref[rows, :]
        for n in range(RET_IN_COLS // IN_COL_TILE):
            cols = slice(n * IN_COL_TILE, (n + 1) * IN_COL_TILE)
            acc = jnp.dot(xn, w_ref[:, cols], preferred_element_type=F32)
            if n * IN_COL_TILE >= 2 * RET_QK_TOTAL:
                o_ref[rows, cols] = acc.astype(BF16)
                continue
            scale = 1.0 if n * IN_COL_TILE < RET_QK_TOTAL else RET_QK_DIM ** -0.5
            for h in range(IN_COL_TILE // RET_QK_DIM):
                lo = h * RET_QK_DIM
                t1 = acc[:, lo:lo + HALF]
                t2 = acc[:, lo + HALF:lo + RET_QK_DIM]
                base = n * IN_COL_TILE + lo
                o_ref[rows, base:base + HALF] = ((t1 * cos - t2 * sin) * scale).astype(BF16)
                o_ref[rows, base + HALF:base + RET_QK_DIM] = (
                    (t1 * sin + t2 * cos) * scale).astype(BF16)


def _ret_in(x2, g, w_bf, cos, sin, seq):
    t = x2.shape[0]
    tm = IN_TOKEN_TILE
    seq_tiles = seq // tm
    return pl.pallas_call(
        _ret_in_kernel,
        grid=(t // tm,),
        in_specs=[
            pl.BlockSpec((tm, D_MODEL), lambda i: (i, 0)),
            pl.BlockSpec((1, D_MODEL), lambda i: (0, 0)),
            pl.BlockSpec((D_MODEL, RET_IN_COLS), lambda i: (0, 0), pipeline_mode=pl.Buffered(1)),
            pl.BlockSpec((tm, HALF), lambda i: (i % seq_tiles, 0)),
            pl.BlockSpec((tm, HALF), lambda i: (i % seq_tiles, 0)),
        ],
        out_specs=pl.BlockSpec((tm, RET_IN_COLS), lambda i: (i, 0)),
        out_shape=jax.ShapeDtypeStruct((t, RET_IN_COLS), BF16),
        compiler_params=_params("parallel"),
        name="ret_in_proj",
    )(x2, g, w_bf, cos, sin)


def _ret_core_kernel(q_ref, k_ref, v_ref, gate_ref, x_ref, wout_ref,
                     intra_ref, qdec_ref, kdec_ref, cdec_ref, *rest):
    n_side = (len(rest) - 2) // 2
    side_in, o_ref, side_out, state_ref = (rest[:n_side], rest[n_side],
                                           rest[n_side + 1:2 * n_side + 1], rest[-1])

    for src, dst in zip(side_in, side_out):
        dst[...] = src[...].astype(BF16)

    @pl.when(pl.program_id(1) == 0)
    def _():
        state_ref[...] = jnp.zeros_like(state_ref)

    gated = []
    for c in range(RET_SEQ_TILE // RET_CHUNK):
        rows = pl.ds(c * RET_CHUNK, RET_CHUNK)
        heads = []
        for h in range(RET_HEADS):
            qk_cols = slice(h * RET_QK_DIM, (h + 1) * RET_QK_DIM)
            v_cols = slice(h * RET_V_DIM, (h + 1) * RET_V_DIM)
            q = q_ref[rows, qk_cols]
            k = k_ref[rows, qk_cols]
            v = v_ref[rows, v_cols]
            state = state_ref[h]
            scores = lax.dot_general(q, k, (((1,), (1,)), ((), ())),
                                     preferred_element_type=F32) * intra_ref[h]
            inner = jnp.dot(scores.astype(BF16), v, preferred_element_type=F32)
            qd = (q.astype(F32) * qdec_ref[h]).astype(BF16)
            cross = jnp.dot(qd, state.astype(BF16), preferred_element_type=F32)
            kd = (k.astype(F32) * kdec_ref[h]).astype(BF16)
            state_ref[h] = state * cdec_ref[h] + lax.dot_general(
                kd, v, (((0,), (0,)), ((), ())), preferred_element_type=F32)
            o = inner + cross
            o = o * lax.rsqrt(jnp.mean(o * o, axis=-1, keepdims=True) + NORM_EPS)
            gate = gate_ref[rows, v_cols].astype(F32)
            heads.append((gate * (0.5 * jnp.tanh(0.5 * gate) + 0.5) * o).astype(BF16))
        gated.append(jnp.concatenate(heads, axis=1))
    og = jnp.concatenate(gated, axis=0)
    o_ref[0] = x_ref[0] + jnp.dot(og, wout_ref[...], preferred_element_type=F32)


def _ret_core(proj, x, wout_bf, intra, qdec, kdec, cdec, side_weights):
    b, seq, _ = x.shape
    ts = RET_SEQ_TILE
    st = seq // ts
    steps = b * st
    whole = lambda a: pl.BlockSpec(a.shape, lambda bi, s: (0,) * a.ndim)

    def slab_rows(wt):
        rows = wt.shape[0] * wt.shape[1]
        per_step = rows // steps
        ok = (rows % steps == 0 and per_step % BF16_SUBLANES == 0 and wt.shape[1] % per_step == 0
              and per_step * wt.shape[2] * 4 <= SIDE_SLAB_BYTES)
        return per_step if ok else None

    hosted = all(slab_rows(wt) is not None for wt in side_weights)
    side = tuple(side_weights) if hosted else ()
    side_specs = []
    for wt in side:
        per_step = slab_rows(wt)
        per_mat = wt.shape[1] // per_step
        side_specs.append(pl.BlockSpec(
            (1, per_step, wt.shape[2]),
            lambda bi, s, per_mat=per_mat: ((bi * st + s) // per_mat, (bi * st + s) % per_mat, 0)))

    outs = pl.pallas_call(
        _ret_core_kernel,
        grid=(b, st),
        in_specs=[
            pl.BlockSpec((ts, RET_QK_TOTAL), lambda bi, s: (bi * st + s, 0)),
            pl.BlockSpec((ts, RET_QK_TOTAL), lambda bi, s: (bi * st + s, 1)),
            pl.BlockSpec((ts, RET_V_TOTAL), lambda bi, s: (bi * st + s, 1)),
            pl.BlockSpec((ts, RET_V_TOTAL), lambda bi, s: (bi * st + s, 2)),
            pl.BlockSpec((1, ts, D_MODEL), lambda bi, s: (bi, s, 0)),
            pl.BlockSpec((RET_V_TOTAL, D_MODEL), lambda bi, s: (0, 0), pipeline_mode=pl.Buffered(1)),
            whole(intra), whole(qdec), whole(kdec), whole(cdec),
        ] + side_specs,
        out_specs=[pl.BlockSpec((1, ts, D_MODEL), lambda bi, s: (bi, s, 0))] + side_specs,
        out_shape=[jax.ShapeDtypeStruct(x.shape, F32)]
                  + [jax.ShapeDtypeStruct(wt.shape, BF16) for wt in side],
        scratch_shapes=[pltpu.VMEM((RET_HEADS, RET_QK_DIM, RET_V_DIM), F32)],
        compiler_params=_params("arbitrary", "arbitrary"),
        name="ret_core",
    )(proj, proj, proj, proj, x, wout_bf, intra, qdec, kdec, cdec, *side)
    if hosted:
        return outs[0], tuple(outs[1:])
    return outs[0], tuple(wt.astype(BF16) for wt in side_weights)


def _swiglu(x_bf, wg_ref, wu_ref, wd_ref):
    out = None
    for c in range(D_FF // FF_TILE):
        cols = slice(c * FF_TILE, (c + 1) * FF_TILE)
        hg = jnp.dot(x_bf, wg_ref[:, cols], preferred_element_type=F32)
        hu = jnp.dot(x_bf, wu_ref[:, cols], preferred_element_type=F32)
        hid = (_silu(hg) * hu).astype(BF16)
        part = jnp.dot(hid, wd_ref[cols, :], preferred_element_type=F32)
        out = part if out is None else out + part
    return out


def _ffn_kernel(x_ref, g_ref, wg_ref, wu_ref, wd_ref, o_ref):
    for r in range(TOKEN_TILE // ROW_SUB):
        rows = pl.ds(r * ROW_SUB, ROW_SUB)
        x = x_ref[rows, :]
        xn = _rmsnorm_f32(x, g_ref[...]).astype(BF16)
        o_ref[rows, :] = x + _swiglu(xn, wg_ref, wu_ref, wd_ref)


def _ffn(x2, g, wg_bf, wu_bf, wd_bf):
    t = x2.shape[0]
    resident = lambda shape: pl.BlockSpec(shape, lambda i: (0, 0), pipeline_mode=pl.Buffered(1))
    return pl.pallas_call(
        _ffn_kernel,
        grid=(t // TOKEN_TILE,),
        in_specs=[
            pl.BlockSpec((TOKEN_TILE, D_MODEL), lambda i: (i, 0)),
            pl.BlockSpec((1, D_MODEL), lambda i: (0, 0)),
            resident((D_MODEL, D_FF)), resident((D_MODEL, D_FF)), resident((D_FF, D_MODEL)),
        ],
        out_specs=pl.BlockSpec((TOKEN_TILE, D_MODEL), lambda i: (i, 0)),
        out_shape=jax.ShapeDtypeStruct(x2.shape, F32),
        compiler_params=_params("parallel"),
        name="ffn_swiglu",
    )(x2, g, wg_bf, wu_bf, wd_bf)


def _split_bf16(x):
    hi = x.astype(BF16)
    return hi, (x - hi.astype(F32)).astype(BF16)


def _route_normalise(h, g_ref, xn_ref):
    xn = _rmsnorm_f32(h, g_ref[...])
    xn_ref[...] = _pack_bf16_pair(xn)
    return _split_bf16(xn)


def _route_select(x_hi, x_lo, wrt_ref, before_ref, count_ref,
                  sel_e_ref, sel_r_ref, sel_g_ref, cend_ref):
    w_hi, w_lo = _split_bf16(wrt_ref[...])
    nt = lambda a, b: lax.dot_general(a, b, (((1,), (1,)), ((), ())), preferred_element_type=F32)
    both = nt(jnp.concatenate([w_hi, w_lo], axis=0), x_hi)
    logits = both[:N_EXPERTS] + (both[N_EXPERTS:] + nt(w_hi, x_lo))
    sub = lax.broadcasted_iota(jnp.int32, logits.shape, 0).astype(F32)
    none = float(N_EXPERTS)
    m1 = jnp.max(logits, axis=0, keepdims=True)
    i1 = jnp.min(jnp.where(logits == m1, sub, none), axis=0, keepdims=True)
    rest = jnp.where(sub == i1, -jnp.inf, logits)
    m2 = jnp.max(rest, axis=0, keepdims=True)
    i2 = jnp.min(jnp.where(rest == m2, sub, none), axis=0, keepdims=True)
    e2 = jnp.exp(m2 - m1)
    denom = 1.0 + e2

    chosen = jnp.logical_or(sub == i1, sub == i2)
    prefix = jnp.dot(jnp.where(chosen, 1.0, 0.0).astype(BF16), before_ref[...],
                     preferred_element_type=F32)
    count = count_ref[...]
    rank = prefix + count[:, :1]
    r1 = jnp.sum(jnp.where(sub == i1, rank, 0.0), axis=0, keepdims=True)
    r2 = jnp.sum(jnp.where(sub == i2, rank, 0.0), axis=0, keepdims=True)
    sel_e_ref[0:1, :] = i1.astype(I32)
    sel_e_ref[1:2, :] = i2.astype(I32)
    sel_r_ref[0:1, :] = r1.astype(I32)
    sel_r_ref[1:2, :] = r2.astype(I32)
    sel_g_ref[0:1, :] = 1.0 / denom
    sel_g_ref[1:2, :] = e2 / denom
    count = count + jnp.sum(jnp.where(chosen, 1.0, 0.0), axis=1, keepdims=True)
    count_ref[...] = count
    cend_ref[0] = count


def _conv_route_kernel(x_ref, g_ref, win_ref, wc_ref, wout_ref, g2_ref, wrt_ref, before_ref,
                       o_ref, xn_ref, sel_e_ref, sel_r_ref, sel_g_ref, cend_ref,
                       carry_ref, count_ref):
    s = pl.program_id(1)
    ts = CONV_SEQ_TILE

    @pl.when(s == 0)
    def _():
        carry_ref[...] = jnp.zeros_like(carry_ref)

    @pl.when(jnp.logical_and(pl.program_id(0) == 0, s == 0))
    def _():
        count_ref[...] = jnp.zeros_like(count_ref)

    x = x_ref[0]
    xn = _rmsnorm_f32(x, g_ref[...]).astype(BF16)
    proj = jnp.dot(xn, win_ref[...], preferred_element_type=F32)
    gate_b = proj[:, :D_MODEL]
    gate_c = proj[:, D_MODEL:2 * D_MODEL]
    hid = proj[:, 2 * D_MODEL:]
    u = gate_c * hid

    row = lax.broadcasted_iota(jnp.int32, (ts, D_MODEL), 0)
    prev1 = carry_ref[7:8, :]
    prev2 = carry_ref[6:7, :]
    u1 = jnp.where(row == 0, prev1, pltpu.roll(u, 1, axis=0))
    u2 = jnp.where(row == 0, prev2, jnp.where(row == 1, prev1, pltpu.roll(u, 2, axis=0)))
    carry_ref[...] = u[ts - 8:, :]

    wc = wc_ref[...]
    y = wc[2:3, :] * u + wc[1:2, :] * u1 + wc[0:1, :] * u2
    y = (gate_b * y).astype(BF16)
    h = x + jnp.dot(y, wout_ref[...], preferred_element_type=F32)
    o_ref[0] = h
    x_hi, x_lo = _route_normalise(h, g2_ref, xn_ref)
    _route_select(x_hi, x_lo, wrt_ref, before_ref, count_ref,
                  sel_e_ref, sel_r_ref, sel_g_ref, cend_ref)


def _conv_route(x, part, g, win_bf, wc, wout_bf, g2, wrt):
    b, seq, _ = x.shape
    ts = CONV_SEQ_TILE
    w = ROUTE_WINDOW
    assert ts == w and b % MOE_PARTS == 0
    bp = b // MOE_PARTS
    b0 = part * bp
    st = seq // ts
    steps = bp * st
    tp = bp * seq
    flat = lambda bi, s: bi * st + s
    resident = lambda shape: pl.BlockSpec(shape, lambda bi, s: (0, 0), pipeline_mode=pl.Buffered(1))
    before = (jnp.arange(w)[:, None] < jnp.arange(w)[None, :]).astype(BF16)

    per_token = lambda dtype: jax.ShapeDtypeStruct((TOP_K, tp), dtype)
    return pl.pallas_call(
        _conv_route_kernel,
        grid=(bp, st),
        in_specs=[
            pl.BlockSpec((1, ts, D_MODEL), lambda bi, s: (b0 + bi, s, 0)),
            pl.BlockSpec((1, D_MODEL), lambda bi, s: (0, 0)),
            resident((D_MODEL, 3 * D_MODEL)),
            pl.BlockSpec((CONV_WIDTH, D_MODEL), lambda bi, s: (0, 0)),
            resident((D_MODEL, D_MODEL)),
            pl.BlockSpec((1, D_MODEL), lambda bi, s: (0, 0)),
            pl.BlockSpec((N_EXPERTS, D_MODEL), lambda bi, s: (0, 0)),
            resident((w, w)),
        ],
        out_specs=[
            pl.BlockSpec((1, ts, D_MODEL), lambda bi, s: (bi, s, 0)),
            pl.BlockSpec((w, PACKED), lambda bi, s: (flat(bi, s), 0)),
            pl.BlockSpec((TOP_K, w), lambda bi, s: (0, flat(bi, s))),
            pl.BlockSpec((TOP_K, w), lambda bi, s: (0, flat(bi, s))),
            pl.BlockSpec((TOP_K, w), lambda bi, s: (0, flat(bi, s))),
            pl.BlockSpec((1, N_EXPERTS, LANES), lambda bi, s: (flat(bi, s), 0, 0)),
        ],
        out_shape=[jax.ShapeDtypeStruct((bp, seq, D_MODEL), F32),
                   jax.ShapeDtypeStruct((tp, PACKED), I32),
                   per_token(I32), per_token(I32), per_token(F32),
                   jax.ShapeDtypeStruct((steps, N_EXPERTS, LANES), F32)],
        scratch_shapes=[pltpu.VMEM((8, D_MODEL), F32), pltpu.VMEM((N_EXPERTS, LANES), F32)],
        compiler_params=_params("arbitrary", "arbitrary"),
        name="conv_route",
    )(x, g, win_bf, wc, wout_bf, g2, wrt, before)


def _routing_plan(cend, sel_e, sel_r, t):
    counts = cend[-1, :, 0].astype(I32)
    ntiles = (counts + EXPERT_TILE - 1) // EXPERT_TILE
    tile_end = jnp.cumsum(ntiles)
    tile_start = tile_end - ntiles
    off = tile_start * EXPERT_TILE

    nt_max = TOP_K * t // EXPERT_TILE + N_EXPERTS
    ti = jnp.arange(nt_max, dtype=I32)
    tile_e = jnp.minimum(jnp.sum(tile_end[None, :] <= ti[:, None], axis=1).astype(I32),
                         N_EXPERTS - 1)
    tile_rows = jnp.clip(counts[tile_e] - (ti - tile_start[tile_e]) * EXPERT_TILE, 0, EXPERT_TILE)
    tile_rows = jnp.where(ti < tile_end[-1], tile_rows, 0)

    first_row = jnp.zeros_like(sel_e)
    for e in range(N_EXPERTS):
        first_row = first_row + jnp.where(sel_e == e, off[e], 0)
    dest = first_row + sel_r
    return tile_e, tile_rows, dest


def _sc_mesh():
    return plsc.VectorSubcoreMesh(core_axis_name="c", subcore_axis_name="s",
                                  num_cores=SC_CORES, num_subcores=SC_SUBCORES)


def _dispatch_rows(xn, dest, n_rows):
    t, d = xn.shape
    per_worker = t // SC_WORKERS
    chunks = per_worker // SC_ROWS
    dest4 = dest.reshape(TOP_K, SC_WORKERS, chunks, SC_ROWS)

    @functools.partial(
        pl.kernel, mesh=_sc_mesh(),
        out_type=jax.ShapeDtypeStruct((n_rows, d), xn.dtype),
        scratch_types=[pltpu.VMEM((TOP_K, chunks, SC_ROWS), I32),
                       pltpu.VMEM((SC_ROWS, d), xn.dtype)],
        name="moe_dispatch",
    )
    def run(x_hbm, dest_hbm, o_hbm, dest_v, rows_v):
        worker = lax.axis_index("s") * SC_CORES + lax.axis_index("c")
        for k in range(TOP_K):
            pltpu.sync_copy(dest_hbm.at[k, worker], dest_v.at[k])

        @pl.loop(0, chunks)
        def _(j):
            pltpu.sync_copy(x_hbm.at[pl.ds(worker * per_worker + j * SC_ROWS, SC_ROWS)], rows_v)
            for k in range(TOP_K):
                pltpu.sync_copy(rows_v, o_hbm.at[dest_v.at[k, j]])

    return run(xn, dest4)


def _collect_rows(y, dest):
    t = dest.shape[1]
    d = y.shape[1]
    per_worker = t // SC_WORKERS
    chunks = per_worker // SC_ROWS
    dest4 = dest.reshape(TOP_K, SC_WORKERS, chunks, SC_ROWS)

    @functools.partial(
        pl.kernel, mesh=_sc_mesh(),
        out_type=jax.ShapeDtypeStruct((TOP_K, t, d), y.dtype),
        scratch_types=[pltpu.VMEM((TOP_K, chunks, SC_ROWS), I32),
                       pltpu.VMEM((SC_ROWS, d), y.dtype)],
        name="moe_collect",
    )
    def run(y_hbm, dest_hbm, o_hbm, dest_v, rows_v):
        worker = lax.axis_index("s") * SC_CORES + lax.axis_index("c")
        for k in range(TOP_K):
            pltpu.sync_copy(dest_hbm.at[k, worker], dest_v.at[k])

        @pl.loop(0, chunks)
        def _(j):
            for k in range(TOP_K):
                pltpu.sync_copy(y_hbm.at[dest_v.at[k, j]], rows_v)
                pltpu.sync_copy(rows_v, o_hbm.at[k, pl.ds(worker * per_worker + j * SC_ROWS, SC_ROWS)])

    return run(y, dest4)


def _expert_kernel(tile_e_s, tile_rows_s, xs_ref, wg_ref, wu_ref, wd_ref, y_ref):
    n_rows = tile_rows_s[pl.program_id(0)]
    n_sub = EXPERT_TILE // ROW_SUB

    def run(sub_blocks):
        for r in range(sub_blocks):
            rows = pl.ds(r * ROW_SUB, ROW_SUB)
            real = lax.broadcasted_iota(jnp.int32, (ROW_SUB, PACKED), 0) < n_rows - r * ROW_SUB
            xs = _unpack_bf16_pair(jnp.where(real, xs_ref[rows, :], 0)).astype(BF16)
            y_ref[rows, :] = _pack_bf16_pair(_swiglu(xs, wg_ref.at[0], wu_ref.at[0], wd_ref.at[0]))
        if sub_blocks < n_sub:
            rest = pl.ds(sub_blocks * ROW_SUB, (n_sub - sub_blocks) * ROW_SUB)
            y_ref[rest, :] = jnp.zeros(((n_sub - sub_blocks) * ROW_SUB, PACKED), I32)

    for used in range(n_sub + 1):
        lo, hi = (used - 1) * ROW_SUB, used * ROW_SUB
        pl.when(jnp.logical_and(n_rows > lo, n_rows <= hi))(functools.partial(run, used))


def _experts(tile_e, tile_rows, xs, wg_bf, wu_bf, wd_bf):
    n_rows = xs.shape[0]
    weights = lambda shape, buffers: pl.BlockSpec(
        (1,) + shape, lambda i, te, tr: (te[i], 0, 0), pipeline_mode=pl.Buffered(buffers))
    grid_spec = pltpu.PrefetchScalarGridSpec(
        num_scalar_prefetch=2,
        grid=(n_rows // EXPERT_TILE,),
        in_specs=[
            pl.BlockSpec((EXPERT_TILE, PACKED), lambda i, te, tr: (i, 0)),
            weights((D_MODEL, D_FF), 1), weights((D_MODEL, D_FF), 1), weights((D_FF, D_MODEL), 2),
        ],
        out_specs=pl.BlockSpec((EXPERT_TILE, PACKED), lambda i, te, tr: (i, 0)),
    )
    return pl.pallas_call(
        _expert_kernel,
        grid_spec=grid_spec,
        out_shape=jax.ShapeDtypeStruct((n_rows, PACKED), I32),
        compiler_params=_params("arbitrary"),
        name="moe_experts",
    )(tile_e, tile_rows, xs, wg_bf, wu_bf, wd_bf)


def _final_kernel(x_ref, y_ref, sel_g_ref, gf_ref, buf_ref, o_ref):
    del buf_ref
    w = ROUTE_WINDOW
    diag = (lax.broadcasted_iota(jnp.int32, (w, w), 0)
            == lax.broadcasted_iota(jnp.int32, (w, w), 1))
    acc = x_ref[...]
    for k in range(TOP_K):
        gate = jnp.sum(jnp.where(diag, sel_g_ref[k:k + 1, :], 0.0), axis=1, keepdims=True)
        acc = acc + gate * _unpack_bf16_pair(y_ref[k])
    o_ref[...] = _rmsnorm_f32(acc, gf_ref[...])


def _final(x_part, y_part, sel_g_part, g_final, buf, part):
    tp = x_part.shape[0]
    w = ROUTE_WINDOW
    first = part * (tp // w)
    return pl.pallas_call(
        _final_kernel,
        grid=(tp // w,),
        in_specs=[
            pl.BlockSpec((w, D_MODEL), lambda i: (i, 0)),
            pl.BlockSpec((TOP_K, w, PACKED), lambda i: (0, i, 0)),
            pl.BlockSpec((TOP_K, w), lambda i: (0, i)),
            pl.BlockSpec((1, D_MODEL), lambda i: (0, 0)),
            pl.BlockSpec(memory_space=pl.ANY),
        ],
        out_specs=pl.BlockSpec((w, D_MODEL), lambda i: (first + i, 0)),
        out_shape=jax.ShapeDtypeStruct(buf.shape, F32),
        input_output_aliases={4: 0},
        compiler_params=_params("parallel"),
        name="moe_final",
    )(x_part, y_part, sel_g_part, g_final, buf)


def _rotary_tables(seq):
    inv_freq = ROPE_BASE ** (-jnp.arange(0, RET_QK_DIM, 2, dtype=F32) / RET_QK_DIM)
    ang = jnp.arange(seq, dtype=F32)[:, None] * inv_freq[None, :]
    return jnp.cos(ang), jnp.sin(ang)


def _decay_tables():
    c = RET_CHUNK
    log_gamma = jnp.log(1.0 - 2.0 ** (-5.0 - jnp.arange(RET_HEADS, dtype=F32)))
    idx = jnp.arange(c, dtype=F32)
    diff = idx[:, None] - idx[None, :]
    causal = diff >= 0
    intra = jnp.where(causal[None],
                      jnp.exp(jnp.where(causal, diff, 0.0)[None] * log_gamma[:, None, None]), 0.0)
    qdec = jnp.exp((idx[None, :] + 1.0) * log_gamma[:, None])[..., None]
    kdec = jnp.exp((c - 1.0 - idx[None, :]) * log_gamma[:, None])[..., None]
    cdec = jnp.exp(c * log_gamma)[:, None, None]
    return intra, qdec, kdec, cdec


def kernel(x, norm_mix0, ret_w_in, ret_w_out, norm_ffn0, ffn_w_gate, ffn_w_up, ffn_w_down,
           norm_mix1, conv_w_in, conv_w, conv_w_out, norm_ffn1, moe_router, moe_w_gate,
           moe_w_up, moe_w_down, norm_final):
    b, seq, d = x.shape
    t = b * seq
    row = lambda g: g.reshape(1, d).astype(F32)
    bf = lambda w: w.astype(BF16)

    cos, sin = _rotary_tables(seq)
    intra, qdec, kdec, cdec = _decay_tables()

    proj = _ret_in(x.reshape(t, d), row(norm_mix0), bf(ret_w_in), cos, sin, seq)
    h, expert_w = _ret_core(proj, x, bf(ret_w_out), intra, qdec, kdec, cdec,
                            (moe_w_gate, moe_w_up, moe_w_down))
    h_ffn = _ffn(h.reshape(t, d), row(norm_ffn0), bf(ffn_w_gate), bf(ffn_w_up), bf(ffn_w_down))
    h = h_ffn.reshape(b, seq, d)

    conv_args = (row(norm_mix1), bf(conv_w_in), conv_w.astype(F32), bf(conv_w_out),
                 row(norm_ffn1), moe_router.astype(F32).T)
    routed = [_conv_route(h, part, *conv_args) for part in range(MOE_PARTS)]

    tp = t // MOE_PARTS
    n_rows = (TOP_K * tp // EXPERT_TILE + N_EXPERTS) * EXPERT_TILE
    collected = []
    for h_p, xn, sel_e, sel_r, sel_g, cend in routed:
        tile_e, tile_rows, dest = _routing_plan(cend, sel_e, sel_r, tp)
        xs = _dispatch_rows(xn, dest, n_rows)
        y = _experts(tile_e, tile_rows, xs, *expert_w)
        collected.append((h_p.reshape(tp, d), _collect_rows(y, dest), sel_g))
    out = h_ffn
    for part, (h_p, y_tok, sel_g) in enumerate(collected):
        out = _final(h_p, y_tok, sel_g, row(norm_final), out, part)
    return out.reshape(b, seq, d)
```
